```python
import math
import jax
import jax.numpy as jnp
from jax import lax
import numpy as np

D_MODEL = 1024
BATCH = 4
SEQ = 8192
DEPTH = 1
DEC_BATCH = 8
DEC_SEQ = 4096
PAST_LEN = 128

D_MIX = 1024
D_CONV = 512
CONV_WIDTH = 31
CONV_PAD = CONV_WIDTH // 2
N_HEADS = 8
N_KV_HEADS = 2
HEAD_DIM = 64
KV_GROUP = N_HEADS // N_KV_HEADS
D_Q = N_HEADS * HEAD_DIM
D_KV = N_KV_HEADS * HEAD_DIM
D_IN = 2 * D_CONV + D_Q + 2 * D_KV
HALF_ROT = HEAD_DIM // 2
ROPE_THETA = 10000.0
GRID_W = 64
Q_BLOCK = 128
N_EXPERTS = 16
CAPACITY_FACTOR = 2
D_FF_EXPERT = 2048
D_PLE = 256
EPS = 1e-6

kernel_name = "hybrid_conv_attn_ec_encoder"


def rms_norm(x, g):
    xf = x.astype(jnp.float32)
    y = xf * lax.rsqrt(jnp.mean(xf * xf, axis=-1, keepdims=True) + EPS)
    return (y * g.astype(jnp.float32)).astype(x.dtype)


def layer_norm(x, g, b):
    xf = x.astype(jnp.float32)
    mu = jnp.mean(xf, axis=-1, keepdims=True)
    var = jnp.mean(jnp.square(xf - mu), axis=-1, keepdims=True)
    y = (xf - mu) * lax.rsqrt(var + EPS) * g.astype(jnp.float32) + b.astype(jnp.float32)
    return y.astype(x.dtype)


def axial_rope_tables(n_tok):
    rows = n_tok // GRID_W
    row_idx = jnp.repeat(jnp.arange(rows, dtype=jnp.float32), GRID_W)
    col_idx = jnp.tile(jnp.arange(GRID_W, dtype=jnp.float32), rows)
    freqs = 1.0 / (ROPE_THETA ** (jnp.arange(0, HALF_ROT, 2, dtype=jnp.float32) / HALF_ROT))
    ang_r = row_idx[:, None] * freqs[None, :]
    ang_c = col_idx[:, None] * freqs[None, :]
    return jnp.cos(ang_r), jnp.sin(ang_r), jnp.cos(ang_c), jnp.sin(ang_c)


def _rotate(x, cos, sin):
    half = x.shape[-1] // 2
    x1, x2 = x[..., :half], x[..., half:]
    c = cos[None, :, None, :].astype(x.dtype)
    s = sin[None, :, None, :].astype(x.dtype)
    return jnp.concatenate([x1 * c - x2 * s, x2 * c + x1 * s], axis=-1)


def apply_axial_rope(x, tables):
    cr, sr, cc, sc = tables
    return jnp.concatenate([_rotate(x[..., :HALF_ROT], cr, sr),
                            _rotate(x[..., HALF_ROT:], cc, sc)], axis=-1)


def conformer_conv(u, dw_w, dw_b, ln_g, ln_b):
    val, gate = u[..., :D_CONV], u[..., D_CONV:]
    g = val * jax.nn.sigmoid(gate)
    kern = dw_w[:, None, :].astype(g.dtype)
    c = lax.conv_general_dilated(g, kern, window_strides=(1,), padding=[(CONV_PAD, CONV_PAD)],
                                 dimension_numbers=('NWC', 'WIO', 'NWC'),
                                 feature_group_count=D_CONV)
    c = layer_norm(c + dw_b, ln_g, ln_b)
    return c * jax.nn.sigmoid(c)


def block_attention(q, k, v):
    B, T = q.shape[0], q.shape[1]
    nb = T // Q_BLOCK
    scale = HEAD_DIM ** -0.5
    qb = q.reshape(B, nb, Q_BLOCK, N_KV_HEADS, KV_GROUP, HEAD_DIM).transpose(1, 0, 2, 3, 4, 5)

    def one_block(qblk):
        s = jnp.einsum('bqkgd,bskd->bkgqs', qblk, k).astype(jnp.float32) * scale
        p = jax.nn.softmax(s, axis=-1).astype(v.dtype)
        return jnp.einsum('bkgqs,bskd->bqkgd', p, v)

    o = lax.map(one_block, qb)
    return o.transpose(1, 0, 2, 3, 4, 5).reshape(B, T, D_Q)


def expert_choice_ffn(h, w_router, w_gate, w_up, w_down):
    B, T, D = h.shape
    n_tok = B * T
    cap = CAPACITY_FACTOR * n_tok // N_EXPERTS
    xt = h.reshape(n_tok, D)
    aff = jax.nn.softmax((xt @ w_router).astype(jnp.float32), axis=-1)
    gates, idx = lax.top_k(aff.T, cap)
    xe = xt[idx]
    hid = jax.nn.silu(jnp.einsum('ecd,edf->ecf', xe, w_gate)) * jnp.einsum('ecd,edf->ecf', xe, w_up)
    ye = jnp.einsum('ecf,efd->ecd', hid, w_down) * gates[..., None].astype(h.dtype)
    out = jnp.zeros_like(xt).at[idx.reshape(-1)].add(ye.reshape(-1, D))
    return out.reshape(B, T, D)


def run_trunk(x, p, norm1_g, w_in, conv_dw_w, conv_dw_b, conv_ln_g, conv_ln_b, q_norm_g, k_norm_g,
              conv_out_g, attn_out_g, w_out, norm2_g, w_router, w_gate, w_up, w_down,
              ple_proj, ple_norm_g, ple_gate_w, ple_gate_b):
    B, T, _ = x.shape
    rope = axial_rope_tables(T)
    h = x
    for i in range(DEPTH):
        a = rms_norm(h, norm1_g[i])
        z = a @ w_in[i]
        o0 = 2 * D_CONV
        u = z[..., :o0]
        q = z[..., o0:o0 + D_Q].reshape(B, T, N_HEADS, HEAD_DIM)
        k = z[..., o0 + D_Q:o0 + D_Q + D_KV].reshape(B, T, N_KV_HEADS, HEAD_DIM)
        v = z[..., o0 + D_Q + D_KV:].reshape(B, T, N_KV_HEADS, HEAD_DIM)
        c_out = conformer_conv(u, conv_dw_w[i], conv_dw_b[i], conv_ln_g[i], conv_ln_b[i])
        q = apply_axial_rope(rms_norm(q, q_norm_g[i]), rope)
        k = apply_axial_rope(rms_norm(k, k_norm_g[i]), rope)
        a_out = block_attention(q, k, v)
        mix = jnp.concatenate([rms_norm(c_out, conv_out_g[i]), rms_norm(a_out, attn_out_g[i])], axis=-1)
        h = h + mix @ w_out[i]
        h = h + expert_choice_ffn(rms_norm(h, norm2_g[i]), w_router[i], w_gate[i], w_up[i], w_down[i])
        e = p[i] @ ple_proj[i]
        gate = jax.nn.sigmoid(rms_norm(h, ple_norm_g[i]) @ ple_gate_w[i] + ple_gate_b[i])
        h = h + gate * e
    return h


def setup_inputs(seed: int = 0) -> dict:
    key = jax.random.key(seed)
    ks = jax.random.split(key, 24)
    f32 = jnp.float32

    def nrm(k, shape, scale):
        return jax.random.normal(k, shape, f32) * scale

    def gain(k, shape):
        return 1.0 + 0.02 * jax.random.normal(k, shape, f32)

    return {
        "x_prompt": nrm(ks[0], (BATCH, SEQ, D_MODEL), 1.0),
        "x_sample": nrm(ks[1], (DEC_BATCH, DEC_SEQ, D_MODEL), 1.0),
        "p_prompt": nrm(ks[2], (DEPTH, BATCH, SEQ, D_PLE), 1.0),
        "p_sample": nrm(ks[3], (DEPTH, DEC_BATCH, DEC_SEQ, D_PLE), 1.0),
        "norm1_g": gain(ks[4], (DEPTH, D_MODEL)),
        "w_in": nrm(ks[5], (DEPTH, D_MODEL, D_IN), D_MODEL ** -0.5),
        "conv_dw_w": nrm(ks[6], (DEPTH, CONV_WIDTH, D_CONV), CONV_WIDTH ** -0.5),
        "conv_dw_b": nrm(ks[7], (DEPTH, D_CONV), 0.02),
        "conv_ln_g": gain(ks[8], (DEPTH, D_CONV)),
        "conv_ln_b": nrm(ks[9], (DEPTH, D_CONV), 0.02),
        "q_norm_g": gain(ks[10], (DEPTH, HEAD_DIM)),
        "k_norm_g": gain(ks[11], (DEPTH, HEAD_DIM)),
        "conv_out_g": gain(ks[12], (DEPTH, D_CONV)),
        "attn_out_g": gain(ks[13], (DEPTH, D_Q)),
        "w_out": nrm(ks[14], (DEPTH, D_MIX, D_MODEL), D_MIX ** -0.5),
        "norm2_g": gain(ks[15], (DEPTH, D_MODEL)),
        "w_router": nrm(ks[16], (DEPTH, D_MODEL, N_EXPERTS), D_MODEL ** -0.5),
        "w_gate": nrm(ks[17], (DEPTH, N_EXPERTS, D_MODEL, D_FF_EXPERT), D_MODEL ** -0.5),
        "w_up": nrm(ks[18], (DEPTH, N_EXPERTS, D_MODEL, D_FF_EXPERT), D_MODEL ** -0.5),
        "w_down": nrm(ks[19], (DEPTH, N_EXPERTS, D_FF_EXPERT, D_MODEL), D_FF_EXPERT ** -0.5),
        "ple_proj": nrm(ks[20], (DEPTH, D_PLE, D_MODEL), D_PLE ** -0.5),
        "ple_norm_g": gain(ks[21], (DEPTH, D_MODEL)),
        "ple_gate_w": nrm(ks[22], (DEPTH, D_MODEL, D_MODEL), D_MODEL ** -0.5),
        "ple_gate_b": nrm(ks[23], (DEPTH, D_MODEL), 0.02),
    }


def reference(x_prompt, x_sample, p_prompt, p_sample, norm1_g, w_in, conv_dw_w, conv_dw_b, conv_ln_g,
              conv_ln_b, q_norm_g, k_norm_g, conv_out_g, attn_out_g, w_out, norm2_g, w_router, w_gate,
              w_up, w_down, ple_proj, ple_norm_g, ple_gate_w, ple_gate_b):
    y_prompt = run_trunk(x_prompt, p_prompt, norm1_g, w_in, conv_dw_w, conv_dw_b, conv_ln_g, conv_ln_b,
                         q_norm_g, k_norm_g, conv_out_g, attn_out_g, w_out, norm2_g, w_router, w_gate,
                         w_up, w_down, ple_proj, ple_norm_g, ple_gate_w, ple_gate_b)
    y_sample = run_trunk(x_sample, p_sample, norm1_g, w_in, conv_dw_w, conv_dw_b, conv_ln_g, conv_ln_b,
                         q_norm_g, k_norm_g, conv_out_g, attn_out_g, w_out, norm2_g, w_router, w_gate,
                         w_up, w_down, ple_proj, ple_norm_g, ple_gate_w, ple_gate_b)
    return (y_prompt, y_sample)
```

```python
import functools
import math

import jax
import jax.numpy as jnp
from jax import lax
from jax.experimental import pallas as pl
from jax.experimental.pallas import tpu as pltpu

D_MODEL = 1024
D_CONV = 512
CONV_WIDTH = 31
CONV_PAD = CONV_WIDTH // 2
N_HEADS = 8
N_KV_HEADS = 2
HEAD_DIM = 64
KV_GROUP = N_HEADS // N_KV_HEADS
D_Q = N_HEADS * HEAD_DIM
D_KV = N_KV_HEADS * HEAD_DIM
D_IN = 2 * D_CONV + D_Q + 2 * D_KV
HALF_ROT = HEAD_DIM // 2
ROPE_THETA = 10000.0
GRID_W = 64
N_EXPERTS = 16
CAPACITY_FACTOR = 2
D_FF_EXPERT = 2048
D_PLE = 256
EPS = 1e-6

LANES = 128
BF16_SUBLANES = 16
VMEM_LIMIT = 56 * 1024 * 1024

TM = 512
TT = 256
HALO = 16
TQ = 128
TK = 512
TR = 512
CH = 128
TMF = 512

F32 = jnp.float32
BF16 = jnp.bfloat16


def _cparams(sem):
    return pltpu.CompilerParams(dimension_semantics=sem, vmem_limit_bytes=VMEM_LIMIT)


def _sigmoid(x):
    return 1.0 / (1.0 + jnp.exp(-x))


def _inproj_body(x_ref, g1_ref, w_ref, qkg_ref, cos_ref, sin_ref, ones_ref,
                 gl_ref, q_ref, k_ref, v_ref):
    x = x_ref[...]
    a = x * lax.rsqrt(jnp.mean(x * x, axis=-1, keepdims=True) + EPS) * g1_ref[...]
    z = jnp.dot(a.astype(BF16), w_ref[...], preferred_element_type=F32)
    val = z[:, :D_CONV]
    gate = z[:, D_CONV:2 * D_CONV]
    gl_ref[...] = (val * _sigmoid(gate)).astype(BF16)

    lane = lax.broadcasted_iota(jnp.int32, (x.shape[0], LANES), 1)
    first_half = (lane % HALF_ROT) < (HALF_ROT // 2)
    low_head = lane < HEAD_DIM
    cos = cos_ref[...]
    sin = sin_ref[...]
    o0 = 2 * D_CONV
    n_chunks = (D_Q + D_KV) // LANES
    for c in range(n_chunks):
        qc = z[:, o0 + c * LANES:o0 + (c + 1) * LANES]
        ssum = jnp.dot((qc * qc).astype(BF16), ones_ref[...], preferred_element_type=F32)
        qn = qc * lax.rsqrt(ssum * (1.0 / HEAD_DIM) + EPS) * qkg_ref[:, c * LANES:(c + 1) * LANES]
        partner = jnp.where(first_half, pltpu.roll(qn, LANES - HALF_ROT // 2, 1),
                            pltpu.roll(qn, HALF_ROT // 2, 1))
        qr = qn * cos + partner * sin
        even = jnp.where(low_head, qr, 0.0).astype(BF16)
        odd = jnp.where(low_head, pltpu.roll(qr, HEAD_DIM, 1), 0.0).astype(BF16)
        if c < D_Q // LANES:
            q_ref[:, (2 * c) * LANES:(2 * c + 1) * LANES] = even
            q_ref[:, (2 * c + 1) * LANES:(2 * c + 2) * LANES] = odd
        else:
            k_ref[:, 0:LANES] = even
            k_ref[:, LANES:2 * LANES] = odd
    vv = z[:, o0 + D_Q + D_KV:]
    v_ref[:, 0:LANES] = jnp.where(low_head, vv, 1.0).astype(BF16)
    v_ref[:, LANES:2 * LANES] = jnp.where(low_head, pltpu.roll(vv, HEAD_DIM, 1), 1.0).astype(BF16)


def _inproj(x2, g1, w_in_bf, qkg, cos_t, sin_t, ones_bd, seq_len):
    n = x2.shape[0]
    pos_tiles = seq_len // TM
    row = lambda i: (i, 0)
    const = lambda i: (0, 0)
    return pl.pallas_call(
        _inproj_body,
        grid=(n // TM,),
        in_specs=[
            pl.BlockSpec((TM, D_MODEL), row),
            pl.BlockSpec((1, D_MODEL), const),
            pl.BlockSpec((D_MODEL, D_IN), const),
            pl.BlockSpec((1, D_Q + D_KV), const),
            pl.BlockSpec((TM, LANES), lambda i: (i % pos_tiles, 0)),
            pl.BlockSpec((TM, LANES), lambda i: (i % pos_tiles, 0)),
            pl.BlockSpec((LANES, LANES), const),
        ],
        out_specs=[
            pl.BlockSpec((TM, D_CONV), row),
            pl.BlockSpec((TM, N_HEADS * LANES), row),
            pl.BlockSpec((TM, N_KV_HEADS * LANES), row),
            pl.BlockSpec((TM, N_KV_HEADS * LANES), row),
        ],
        out_shape=[
            jax.ShapeDtypeStruct((n, D_CONV), BF16),
            jax.ShapeDtypeStruct((n, N_HEADS * LANES), BF16),
            jax.ShapeDtypeStruct((n, N_KV_HEADS * LANES), BF16),
            jax.ShapeDtypeStruct((n, N_KV_HEADS * LANES), BF16),
        ],
        compiler_params=_cparams(("parallel",)),
        name="inproj",
    )(x2, g1, w_in_bf, qkg, cos_t, sin_t, ones_bd)


CONV_ROWS = 64


def _conv_body(left_ref, main_ref, right_ref, w_ref, b_ref, lng_ref, lnb_ref, og_ref,
               out_ref, win_ref, conv_ref):
    i = pl.program_id(1)
    last = pl.num_programs(1) - 1
    left = left_ref[0].astype(F32)
    right = right_ref[0].astype(F32)
    win_ref[0:HALO, :] = jnp.where(i > 0, left, 0.0)
    win_ref[HALO:HALO + TT, :] = main_ref[0].astype(F32)
    win_ref[HALO + TT:HALO + TT + HALO, :] = jnp.where(i < last, right, 0.0)
    base = HALO - CONV_PAD
    for c in range(D_CONV // LANES):
        ls = slice(c * LANES, (c + 1) * LANES)
        for r in range(TT // CONV_ROWS):
            acc = jnp.zeros((CONV_ROWS, LANES), F32)
            for k in range(CONV_WIDTH):
                acc = acc + win_ref[r * CONV_ROWS + base + k:r * CONV_ROWS + base + k + CONV_ROWS, ls] * w_ref[k:k + 1, ls]
            conv_ref[r * CONV_ROWS:(r + 1) * CONV_ROWS, ls] = acc
    cv = conv_ref[...] + b_ref[...]
    mu = jnp.mean(cv, axis=-1, keepdims=True)
    d = cv - mu
    var = jnp.mean(d * d, axis=-1, keepdims=True)
    y = d * lax.rsqrt(var + EPS) * lng_ref[...] + lnb_ref[...]
    y = y * _sigmoid(y)
    y = y * lax.rsqrt(jnp.mean(y * y, axis=-1, keepdims=True) + EPS) * og_ref[...]
    out_ref[0] = y.astype(BF16)


def _conv(gl3, dw_w, dw_b, ln_g, ln_b, out_g):
    b, t, _ = gl3.shape
    hb = TT // HALO
    n_halo = t // HALO
    const = lambda bb, i: (0, 0)
    return pl.pallas_call(
        _conv_body,
        grid=(b, t // TT),
        in_specs=[
            pl.BlockSpec((1, HALO, D_CONV), lambda bb, i: (bb, jnp.maximum(i * hb - 1, 0), 0)),
            pl.BlockSpec((1, TT, D_CONV), lambda bb, i: (bb, i, 0)),
            pl.BlockSpec((1, HALO, D_CONV), lambda bb, i: (bb, jnp.minimum((i + 1) * hb, n_halo - 1), 0)),
            pl.BlockSpec((CONV_WIDTH, D_CONV), const),
            pl.BlockSpec((1, D_CONV), const),
            pl.BlockSpec((1, D_CONV), const),
            pl.BlockSpec((1, D_CONV), const),
            pl.BlockSpec((1, D_CONV), const),
        ],
        out_specs=pl.BlockSpec((1, TT, D_CONV), lambda bb, i: (bb, i, 0)),
        out_shape=jax.ShapeDtypeStruct((b, t, D_CONV), BF16),
        scratch_shapes=[pltpu.VMEM((TT + 2 * HALO, D_CONV), F32),
                        pltpu.VMEM((TT, D_CONV), F32)],
        compiler_params=_cparams(("parallel", "parallel")),
        name="conv",
    )(gl3, gl3, gl3, dw_w, dw_b, ln_g, ln_b, out_g)


def _attn_body(q_ref, k_ref, v_ref, o_ref, qs_ref, m_ref, acc_ref):
    t = k_ref.shape[1]
    for h in range(KV_GROUP):
        qs_ref[h * TQ:(h + 1) * TQ, :] = q_ref[0, :, h * LANES:(h + 1) * LANES]
    m_ref[...] = jnp.full(m_ref.shape, -jnp.inf, F32)
    acc_ref[...] = jnp.zeros(acc_ref.shape, F32)

    def step(kt, carry):
        start = pl.multiple_of(kt * TK, TK)
        kk = k_ref[0, pl.ds(start, TK), :]
        vv = v_ref[0, pl.ds(start, TK), :]
        s = lax.dot_general(qs_ref[...], kk, (((1,), (1,)), ((), ())),
                            preferred_element_type=F32)
        m_old = m_ref[...]
        m_new = jnp.maximum(m_old, jnp.max(s, axis=-1, keepdims=True))
        p = jnp.exp2(s - m_new[:, 0:1])
        alpha = jnp.exp2(m_old - m_new)
        acc_ref[...] = alpha * acc_ref[...] + jnp.dot(p.astype(BF16), vv, preferred_element_type=F32)
        m_ref[...] = m_new
        return carry

    lax.fori_loop(0, t // TK, step, 0)
    acc = acc_ref[...]
    res = acc / pltpu.roll(acc, HEAD_DIM, 1)
    lane = lax.broadcasted_iota(jnp.int32, (TQ, LANES), 1)
    low = lane < HEAD_DIM
    for hp in range(KV_GROUP // 2):
        r0 = res[(2 * hp) * TQ:(2 * hp + 1) * TQ, :]
        r1 = res[(2 * hp + 1) * TQ:(2 * hp + 2) * TQ, :]
        o_ref[0, :, hp * LANES:(hp + 1) * LANES] = jnp.where(low, r0, pltpu.roll(r1, HEAD_DIM, 1)).astype(BF16)


def _attention(q3, k3, v3):
    b, t, _ = q3.shape
    gw = KV_GROUP * LANES
    return pl.pallas_call(
        _attn_body,
        grid=(b, N_KV_HEADS, t // TQ),
        in_specs=[
            pl.BlockSpec((1, TQ, gw), lambda bb, j, i: (bb, i, j)),
            pl.BlockSpec((1, t, LANES), lambda bb, j, i: (bb, 0, j)),
            pl.BlockSpec((1, t, LANES), lambda bb, j, i: (bb, 0, j)),
        ],
        out_specs=pl.BlockSpec((1, TQ, KV_GROUP * HEAD_DIM), lambda bb, j, i: (bb, i, j)),
        out_shape=jax.ShapeDtypeStruct((b, t, D_Q), BF16),
        scratch_shapes=[pltpu.VMEM((KV_GROUP * TQ, LANES), BF16),
                        pltpu.VMEM((KV_GROUP * TQ, LANES), F32),
                        pltpu.VMEM((KV_GROUP * TQ, LANES), F32)],
        compiler_params=_cparams(("parallel", "parallel", "parallel")),
        name="attention",
    )(q3, k3, v3)


def _outproj_body(x_ref, cn_ref, ao_ref, ag_ref, wc_ref, wa_ref, g2_ref, wrt_ref,
                  h_ref, xn_ref, aff_ref):
    ao = ao_ref[...].astype(F32)
    an = ao * lax.rsqrt(jnp.mean(ao * ao, axis=-1, keepdims=True) + EPS) * ag_ref[...]
    h = (x_ref[...]
         + jnp.dot(cn_ref[...], wc_ref[...], preferred_element_type=F32)
         + jnp.dot(an.astype(BF16), wa_ref[...], preferred_element_type=F32))
    h_ref[...] = h
    xn = h * lax.rsqrt(jnp.mean(h * h, axis=-1, keepdims=True) + EPS) * g2_ref[...]
    xn_ref[...] = xn.astype(BF16)
    logits = lax.dot_general(wrt_ref[...], xn, (((1,), (1,)), ((), ())),
                             precision=lax.Precision.HIGHEST,
                             preferred_element_type=F32)
    mx = jnp.max(logits, axis=0, keepdims=True)
    ex = jnp.exp(logits - mx)
    aff_ref[...] = ex / jnp.sum(ex, axis=0, keepdims=True)


def _outproj(x2, cn, ao, ag, wc, wa, g2, wrt):
    n = x2.shape[0]
    row = lambda i: (i, 0)
    const = lambda i: (0, 0)
    return pl.pallas_call(
        _outproj_body,
        grid=(n // TM,),
        in_specs=[
            pl.BlockSpec((TM, D_MODEL), row),
            pl.BlockSpec((TM, D_CONV), row),
            pl.BlockSpec((TM, D_Q), row),
            pl.BlockSpec((1, D_Q), const),
            pl.BlockSpec((D_CONV, D_MODEL), const),
            pl.BlockSpec((D_Q, D_MODEL), const),
            pl.BlockSpec((1, D_MODEL), const),
            pl.BlockSpec((N_EXPERTS, D_MODEL), const),
        ],
        out_specs=[
            pl.BlockSpec((TM, D_MODEL), row),
            pl.BlockSpec((TM, D_MODEL), row),
            pl.BlockSpec((N_EXPERTS, TM), lambda i: (0, i)),
        ],
        out_shape=[
            jax.ShapeDtypeStruct((n, D_MODEL), F32),
            jax.ShapeDtypeStruct((n, D_MODEL), BF16),
            jax.ShapeDtypeStruct((N_EXPERTS, n), F32),
        ],
        compiler_params=_cparams(("parallel",)),
        name="outproj",
    )(x2, cn, ao, ag, wc, wa, g2, wrt)


def _threshold_body(cap, aff_ref, thr_ref, need_ref):
    def step(it, lo):
        cand = lo | (jnp.int32(1) << (30 - it))
        bits = pltpu.bitcast(aff_ref[...], jnp.int32)
        cnt = jnp.sum((bits >= cand).astype(jnp.int32), axis=1, keepdims=True)
        return jnp.where(cnt >= cap, cand, lo)

    thr = lax.fori_loop(0, 31, step, jnp.zeros((N_EXPERTS, 1), jnp.int32))
    bits = pltpu.bitcast(aff_ref[...], jnp.int32)
    n_gt = jnp.sum((bits > thr).astype(jnp.int32), axis=1, keepdims=True)
    thr_ref[...] = jnp.broadcast_to(thr, thr_ref.shape)
    need_ref[...] = jnp.broadcast_to(cap - n_gt, need_ref.shape)


def _threshold(aff_t, cap):
    n = aff_t.shape[1]
    full = lambda: (0, 0)
    return pl.pallas_call(
        functools.partial(_threshold_body, cap),
        in_specs=[pl.BlockSpec((N_EXPERTS, n), full)],
        out_specs=[pl.BlockSpec((N_EXPERTS, LANES), full), pl.BlockSpec((N_EXPERTS, LANES), full)],
        out_shape=[jax.ShapeDtypeStruct((N_EXPERTS, LANES), jnp.int32),
                   jax.ShapeDtypeStruct((N_EXPERTS, LANES), jnp.int32)],
        compiler_params=pltpu.CompilerParams(vmem_limit_bytes=VMEM_LIMIT),
        name="threshold",
    )(aff_t)


def _mask_body(aff_ref, thr_ref, need_ref, tri_ref, gsel_ref, lpos_ref, cnt_ref, eqc_ref):
    @pl.when(pl.program_id(0) == 0)
    def _():
        eqc_ref[...] = jnp.zeros(eqc_ref.shape, F32)

    aff = aff_ref[...]
    bits = pltpu.bitcast(aff, jnp.int32)
    thr = thr_ref[:, 0:1]
    need = need_ref[:, 0:1].astype(F32)
    gt = bits > thr
    eq = bits == thr
    eq_f = jnp.where(eq, 1.0, 0.0)
    eq_rank = eqc_ref[:, 0:1] + jnp.dot(eq_f.astype(BF16), tri_ref[...], preferred_element_type=F32)
    sel = gt | (eq & (eq_rank < need))
    sel_f = jnp.where(sel, 1.0, 0.0)
    gsel_ref[...] = jnp.where(sel, aff, -1.0)
    lpos_ref[...] = jnp.dot(sel_f.astype(BF16), tri_ref[...], preferred_element_type=F32)
    cnt = jnp.sum(sel_f, axis=1, keepdims=True)
    cnt_ref[0] = jnp.broadcast_to(cnt, (N_EXPERTS, LANES)).astype(jnp.int32)
    eqc_ref[...] = eqc_ref[...] + jnp.sum(eq_f, axis=1, keepdims=True)


def _masks(aff_t, thr, need, tri):
    n = aff_t.shape[1]
    nt = n // TR
    const = lambda i: (0, 0)
    tile = lambda i: (0, i)
    return pl.pallas_call(
        _mask_body,
        grid=(nt,),
        in_specs=[
            pl.BlockSpec((N_EXPERTS, TR), tile),
            pl.BlockSpec((N_EXPERTS, LANES), const),
            pl.BlockSpec((N_EXPERTS, LANES), const),
            pl.BlockSpec((TR, TR), const),
        ],
        out_specs=[
            pl.BlockSpec((N_EXPERTS, TR), tile),
            pl.BlockSpec((N_EXPERTS, TR), tile),
            pl.BlockSpec((1, N_EXPERTS, LANES), lambda i: (i, 0, 0)),
        ],
        out_shape=[
            jax.ShapeDtypeStruct((N_EXPERTS, n), F32),
            jax.ShapeDtypeStruct((N_EXPERTS, n), F32),
            jax.ShapeDtypeStruct((nt, N_EXPERTS, LANES), jnp.int32),
        ],
        scratch_shapes=[pltpu.VMEM((N_EXPERTS, LANES), F32)],
        compiler_params=_cparams(("arbitrary",)),
        name="masks",
    )(aff_t, thr, need, tri)


def _onehot_rows(gsel_row, lpos_row, chunk):
    slot = lax.broadcasted_iota(jnp.int32, (CH, TR), 0).astype(F32) + float(chunk * CH)
    return (gsel_row >= 0.0) & (lpos_row == slot)


def _dispatch_body(off_ref, nch_ref, xn_ref, gsel_ref, lpos_ref, xe_in_ref, xe_ref, stage_ref, sem_ref, xsem_ref):
    del xe_in_ref
    n = pl.program_id(0)
    nt = pl.num_programs(0)
    xn = xn_ref[...]

    def copy(e, slot, sem, chunk):
        row0 = pl.multiple_of(off_ref[e * nt + n] + chunk * CH, BF16_SUBLANES)
        return pltpu.make_async_copy(stage_ref.at[slot], xe_ref.at[e, pl.ds(row0, CH)], sem)

    for e in range(N_EXPERTS):
        hot = _onehot_rows(gsel_ref[e:e + 1, :], lpos_ref[e:e + 1, :], 0)
        rows = jnp.dot(jnp.where(hot, 1.0, 0.0).astype(BF16), xn, preferred_element_type=F32)
        stage_ref[e] = rows.astype(BF16)
        copy(e, e, sem_ref.at[e], 0).start()
    for e in range(N_EXPERTS):
        copy(e, e, sem_ref.at[e], 0).wait()
    for e in range(N_EXPERTS):
        for chunk in range(1, TR // CH):
            @pl.when(chunk < nch_ref[e * nt + n])
            def _():
                hot = _onehot_rows(gsel_ref[e:e + 1, :], lpos_ref[e:e + 1, :], chunk)
                rows = jnp.dot(jnp.where(hot, 1.0, 0.0).astype(BF16), xn, preferred_element_type=F32)
                stage_ref[N_EXPERTS] = rows.astype(BF16)
                cp = copy(e, N_EXPERTS, xsem_ref.at[0], chunk)
                cp.start()
                cp.wait()


def _dispatch(off, nch, xn2, gsel, lpos, xe_init):
    n = xn2.shape[0]
    nt = n // TR
    grid_spec = pltpu.PrefetchScalarGridSpec(
        num_scalar_prefetch=2,
        grid=(nt,),
        in_specs=[
            pl.BlockSpec((TR, D_MODEL), lambda i, *_: (i, 0)),
            pl.BlockSpec((N_EXPERTS, TR), lambda i, *_: (0, i)),
            pl.BlockSpec((N_EXPERTS, TR), lambda i, *_: (0, i)),
            pl.BlockSpec(memory_space=pl.ANY),
        ],
        out_specs=pl.BlockSpec(memory_space=pl.ANY),
        scratch_shapes=[pltpu.VMEM((N_EXPERTS + 1, CH, D_MODEL), BF16),
                        pltpu.SemaphoreType.DMA((N_EXPERTS,)),
                        pltpu.SemaphoreType.DMA((1,))],
    )
    return pl.pallas_call(
        _dispatch_body,
        grid_spec=grid_spec,
        out_shape=jax.ShapeDtypeStruct(xe_init.shape, BF16),
        input_output_aliases={5: 0},
        compiler_params=_cparams(("arbitrary",)),
        name="dispatch",
    )(off, nch, xn2, gsel, lpos, xe_init)


def _ffn_body(ntile_ref, x_ref, wg_ref, wu_ref, wd_ref, y_ref):
    e = pl.program_id(0)
    i = pl.program_id(1)

    @pl.when(i < ntile_ref[e])
    def _():
        x = x_ref[0]
        g = jnp.dot(x, wg_ref[0], preferred_element_type=F32)
        u = jnp.dot(x, wu_ref[0], preferred_element_type=F32)
        hid = (g * _sigmoid(g) * u).astype(BF16)
        y_ref[0] = jnp.dot(hid, wd_ref[0], preferred_element_type=F32).astype(BF16)

    @pl.when(i >= ntile_ref[e])
    def _():
        y_ref[...] = jnp.zeros(y_ref.shape, BF16)


def _ffn(ntile, xe, wg, wu, wd):
    cap_rows = xe.shape[1]
    rows = lambda e, i, nt_ref: (e, jnp.minimum(i, nt_ref[e] - 1), 0)
    wmap = lambda e, i, nt_ref: (e, 0, 0)
    grid_spec = pltpu.PrefetchScalarGridSpec(
        num_scalar_prefetch=1,
        grid=(N_EXPERTS, cap_rows // TMF),
        in_specs=[
            pl.BlockSpec((1, TMF, D_MODEL), rows),
            pl.BlockSpec((1, D_MODEL, D_FF_EXPERT), wmap),
            pl.BlockSpec((1, D_MODEL, D_FF_EXPERT), wmap),
            pl.BlockSpec((1, D_FF_EXPERT, D_MODEL), wmap),
        ],
        out_specs=pl.BlockSpec((1, TMF, D_MODEL), lambda e, i, nt_ref: (e, i, 0)),
    )
    return pl.pallas_call(
        _ffn_body,
        grid_spec=grid_spec,
        out_shape=jax.ShapeDtypeStruct(xe.shape, BF16),
        compiler_params=_cparams(("arbitrary", "arbitrary")),
        name="expert_ffn",
    )(ntile, xe, wg, wu, wd)


def _combine_body(off_ref, nch_ref, h_ref, gsel_ref, lpos_ref, p_ref, eye_ref, wpp_ref, pg_ref,
                  wpg_ref, bpg_ref, ye_ref, y_ref, pad_ref, ybuf_ref, hot_ref, acc_ref, sem_ref, xsem_ref):
    n = pl.program_id(0)
    nt = pl.num_programs(0)

    def copy(e, slot, sem, chunk):
        row0 = pl.multiple_of(off_ref[e * nt + n] + chunk * CH, BF16_SUBLANES)
        return pltpu.make_async_copy(ye_ref.at[e, pl.ds(row0, CH)], ybuf_ref.at[pl.ds(slot * CH, CH)], sem)

    for e in range(N_EXPERTS):
        copy(e, e, sem_ref.at[e], 0).start()

    pad_ref[...] = jnp.zeros(pad_ref.shape, F32)
    pad_ref[0:N_EXPERTS, :] = gsel_ref[...]
    pad_ref[N_EXPERTS:2 * N_EXPERTS, :] = lpos_ref[...]
    cols = lax.dot_general(eye_ref[...], pad_ref[...], (((1,), (1,)), ((), ())),
                           precision=lax.Precision.HIGHEST, preferred_element_type=F32)
    slot = lax.broadcasted_iota(jnp.int32, (TR, CH), 1).astype(F32)

    def gated_hot(e, chunk):
        g = cols[:, e:e + 1]
        lp = cols[:, N_EXPERTS + e:N_EXPERTS + e + 1]
        return jnp.where((g >= 0.0) & (lp == slot + float(chunk * CH)), g, 0.0).astype(BF16)

    for e in range(N_EXPERTS):
        hot_ref[:, e * CH:(e + 1) * CH] = gated_hot(e, 0)
    for e in range(N_EXPERTS):
        copy(e, e, sem_ref.at[e], 0).wait()
    acc_ref[...] = h_ref[...] + jnp.dot(hot_ref[...], ybuf_ref[pl.ds(0, N_EXPERTS * CH), :],
                                        preferred_element_type=F32)
    for e in range(N_EXPERTS):
        for chunk in range(1, TR // CH):
            @pl.when(chunk < nch_ref[e * nt + n])
            def _():
                cp = copy(e, N_EXPERTS, xsem_ref.at[0], chunk)
                cp.start()
                cp.wait()
                acc_ref[...] += jnp.dot(gated_hot(e, chunk), ybuf_ref[pl.ds(N_EXPERTS * CH, CH), :],
                                        preferred_element_type=F32)

    h2 = acc_ref[...]
    emb = jnp.dot(p_ref[...].astype(BF16), wpp_ref[...], preferred_element_type=F32)
    hn = h2 * lax.rsqrt(jnp.mean(h2 * h2, axis=-1, keepdims=True) + EPS) * pg_ref[...]
    gate = _sigmoid(jnp.dot(hn.astype(BF16), wpg_ref[...], preferred_element_type=F32) + bpg_ref[...])
    y_ref[...] = h2 + gate * emb


def _combine(off, nch, h1, gsel, lpos, p2, eye, wpp, pg, wpg, bpg, ye):
    n = h1.shape[0]
    nt = n // TR
    row = lambda i, *_: (i, 0)
    tile = lambda i, *_: (0, i)
    const = lambda i, *_: (0, 0)
    grid_spec = pltpu.PrefetchScalarGridSpec(
        num_scalar_prefetch=2,
        grid=(nt,),
        in_specs=[
            pl.BlockSpec((TR, D_MODEL), row),
            pl.BlockSpec((N_EXPERTS, TR), tile),
            pl.BlockSpec((N_EXPERTS, TR), tile),
            pl.BlockSpec((TR, D_PLE), row),
            pl.BlockSpec((TR, TR), const),
            pl.BlockSpec((D_PLE, D_MODEL), const),
            pl.BlockSpec((1, D_MODEL), const),
            pl.BlockSpec((D_MODEL, D_MODEL), const),
            pl.BlockSpec((1, D_MODEL), const),
            pl.BlockSpec(memory_space=pl.ANY),
        ],
        out_specs=pl.BlockSpec((TR, D_MODEL), row),
        scratch_shapes=[pltpu.VMEM((LANES, TR), F32),
                        pltpu.VMEM(((N_EXPERTS + 1) * CH, D_MODEL), BF16),
                        pltpu.VMEM((TR, N_EXPERTS * CH), BF16),
                        pltpu.VMEM((TR, D_MODEL), F32),
                        pltpu.SemaphoreType.DMA((N_EXPERTS,)),
                        pltpu.SemaphoreType.DMA((1,))],
    )
    return pl.pallas_call(
        _combine_body,
        grid_spec=grid_spec,
        out_shape=jax.ShapeDtypeStruct((n, D_MODEL), F32),
        compiler_params=_cparams(("arbitrary",)),
        name="combine",
    )(off, nch, h1, gsel, lpos, p2, eye, wpp, pg, wpg, bpg, ye)


def _rope_tables(t):
    rows = t // GRID_W
    row_idx = jnp.repeat(jnp.arange(rows, dtype=F32), GRID_W)
    col_idx = jnp.tile(jnp.arange(GRID_W, dtype=F32), rows)
    freqs = 1.0 / (ROPE_THETA ** (jnp.arange(0, HALF_ROT, 2, dtype=F32) / HALF_ROT))
    ang_r = row_idx[:, None] * freqs[None, :]
    ang_c = col_idx[:, None] * freqs[None, :]
    cr, sr, cc, sc = jnp.cos(ang_r), jnp.sin(ang_r), jnp.cos(ang_c), jnp.sin(ang_c)
    cos_h = jnp.concatenate([cr, cr, cc, cc], axis=-1)
    sin_h = jnp.concatenate([-sr, sr, -sc, sc], axis=-1)
    return jnp.tile(cos_h, (1, LANES // HEAD_DIM)), jnp.tile(sin_h, (1, LANES // HEAD_DIM))


def _prepare_weights(norm1_g, w_in, conv_dw_w, conv_dw_b, conv_ln_g, conv_ln_b, q_norm_g, k_norm_g,
                     conv_out_g, attn_out_g, w_out, norm2_g, w_router, w_gate, w_up, w_down,
                     ple_proj, ple_norm_g, ple_gate_w, ple_gate_b):
    i = 0
    q_scale = (HEAD_DIM ** -0.5) * math.log2(math.e)
    qkg = jnp.concatenate([jnp.tile(q_norm_g[i] * q_scale, N_HEADS), jnp.tile(k_norm_g[i], N_KV_HEADS)])
    lane = jnp.arange(LANES)
    ones_bd = (lane[:, None] // HEAD_DIM == lane[None, :] // HEAD_DIM).astype(BF16)
    tok = jnp.arange(TR)
    return dict(
        g1=norm1_g[i][None, :], w_in=w_in[i].astype(BF16), qkg=qkg[None, :], ones_bd=ones_bd,
        dw_w=conv_dw_w[i], dw_b=conv_dw_b[i][None, :], ln_g=conv_ln_g[i][None, :], ln_b=conv_ln_b[i][None, :],
        conv_out_g=conv_out_g[i][None, :], attn_out_g=attn_out_g[i][None, :],
        w_out_c=w_out[i, :D_CONV].astype(BF16), w_out_a=w_out[i, D_CONV:].astype(BF16),
        g2=norm2_g[i][None, :], w_router_t=w_router[i].T,
        wg=w_gate[i].astype(BF16), wu=w_up[i].astype(BF16), wd=w_down[i].astype(BF16),
        tri=(tok[:, None] < tok[None, :]).astype(BF16),
        eye=jnp.eye(TR, dtype=F32),
        wpp=ple_proj[i].astype(BF16), pg=ple_norm_g[i][None, :], wpg=ple_gate_w[i].astype(BF16),
        bpg=ple_gate_b[i][None, :],
    )


def _trunk(x, p, w):
    b, t, _ = x.shape
    n = b * t
    assert t % TM == 0 and t % TK == 0 and t % TT == 0 and n % TR == 0
    cap = CAPACITY_FACTOR * n // N_EXPERTS
    nt = n // TR
    x2 = x.reshape(n, D_MODEL)
    cos_t, sin_t = _rope_tables(t)

    gl, q, k, v = _inproj(x2, w["g1"], w["w_in"], w["qkg"], cos_t, sin_t, w["ones_bd"], t)
    cn = _conv(gl.reshape(b, t, D_CONV), w["dw_w"], w["dw_b"], w["ln_g"], w["ln_b"], w["conv_out_g"])
    ao = _attention(q.reshape(b, t, -1), k.reshape(b, t, -1), v.reshape(b, t, -1))
    h1, xn2, aff_t = _outproj(x2, cn.reshape(n, D_CONV), ao.reshape(n, D_Q), w["attn_out_g"],
                              w["w_out_c"], w["w_out_a"], w["g2"], w["w_router_t"])

    thr, need = _threshold(aff_t, cap)
    gsel, lpos, cnt = _masks(aff_t, thr, need, w["tri"])
    cnt = cnt[:, :, 0].T
    seg = (cnt + BF16_SUBLANES - 1) // BF16_SUBLANES * BF16_SUBLANES
    ends = jnp.cumsum(seg, axis=1)
    off = (ends - seg).astype(jnp.int32).reshape(-1)
    nch = ((cnt + CH - 1) // CH).astype(jnp.int32).reshape(-1)
    cap_rows = (cap + nt * (BF16_SUBLANES - 1) + CH + TMF - 1) // TMF * TMF
    ntile = ((ends[:, -1] + CH + TMF - 1) // TMF).astype(jnp.int32)

    xe = _dispatch(off, nch, xn2, gsel, lpos, jnp.zeros((N_EXPERTS, cap_rows, D_MODEL), BF16))
    ye = _ffn(ntile, xe, w["wg"], w["wu"], w["wd"])
    y = _combine(off, nch, h1, gsel, lpos, p.reshape(n, D_PLE), w["eye"], w["wpp"], w["pg"], w["wpg"],
                 w["bpg"], ye)
    return y.reshape(b, t, D_MODEL)


def kernel(x_prompt, x_sample, p_prompt, p_sample, norm1_g, w_in, conv_dw_w, conv_dw_b, conv_ln_g, conv_ln_b, q_norm_g, k_norm_g, conv_out_g, attn_out_g, w_out, norm2_g, w_router, w_gate, w_up, w_down, ple_proj, ple_norm_g, ple_gate_w, ple_gate_b):
    w = _prepare_weights(norm1_g, w_in, conv_dw_w, conv_dw_b, conv_ln_g, conv_ln_b, q_norm_g, k_norm_g,
                         conv_out_g, attn_out_g, w_out, norm2_g, w_router, w_gate, w_up, w_down,
                         ple_proj, ple_norm_g, ple_gate_w, ple_gate_b)
    y_prompt = _trunk(x_prompt, p_prompt[0], w)
    y_sample = _trunk(x_sample, p_sample[0], w)
    return (y_prompt, y_sample)
```

```python
import functools
import math

import jax
import jax.numpy as jnp
from jax import lax
from jax.experimental import pallas as pl
from jax.experimental.pallas import tpu as pltpu

D_MODEL = 1024
D_CONV = 512
CONV_WIDTH = 31
CONV_PAD = CONV_WIDTH // 2
N_HEADS = 8
N_KV_HEADS = 2
HEAD_DIM = 64
KV_GROUP = N_HEADS // N_KV_HEADS
D_Q = N_HEADS * HEAD_DIM
D_KV = N_KV_HEADS * HEAD_DIM
D_IN = 2 * D_CONV + D_Q + 2 * D_KV
HALF_ROT = HEAD_DIM // 2
ROPE_THETA = 10000.0
GRID_W = 64
N_EXPERTS = 16
CAPACITY_FACTOR = 2
D_FF_EXPERT = 2048
D_PLE = 256
EPS = 1e-6

LANES = 128
F32_SUBLANES = 8
BF16_SUBLANES = 16
VMEM_LIMIT = 56 * 1024 * 1024

TM = 512
TT = 256
HALO = 16
TQ = 128
TK = 512
KV_CHUNKS = 8
MAX_SCORE_SHIFT = 40.0
TR = 512
CH = 128
TMF = 512

F32 = jnp.float32
BF16 = jnp.bfloat16


def _cparams(sem):
    return pltpu.CompilerParams(dimension_semantics=sem, vmem_limit_bytes=VMEM_LIMIT)


def _sigmoid(x):
    return 1.0 / (1.0 + jnp.exp(-x))


def _inproj_body(x_ref, g1_ref, w_ref, qkg_ref, cos_ref, sin_ref, ones_ref, fill_ref,
                 gl_ref, q_ref, k_ref, v_ref):
    x = x_ref[...]
    a = x * lax.rsqrt(jnp.mean(x * x, axis=-1, keepdims=True) + EPS) * g1_ref[...]
    z = jnp.dot(a.astype(BF16), w_ref[...], preferred_element_type=F32)
    val = z[:, :D_CONV]
    gate = z[:, D_CONV:2 * D_CONV]
    gl_ref[...] = (val * _sigmoid(gate)).astype(BF16)

    lane = lax.broadcasted_iota(jnp.int32, (x.shape[0], LANES), 1)
    first_half = (lane % HALF_ROT) < (HALF_ROT // 2)
    low_head = lane < HEAD_DIM
    cos = cos_ref[...]
    sin = sin_ref[...]
    o0 = 2 * D_CONV
    n_chunks = (D_Q + D_KV) // LANES
    for c in range(n_chunks):
        qc = z[:, o0 + c * LANES:o0 + (c + 1) * LANES]
        ssum = jnp.dot((qc * qc).astype(BF16), ones_ref[...], preferred_element_type=F32)
        qn = qc * lax.rsqrt(ssum * (1.0 / HEAD_DIM) + EPS) * qkg_ref[:, c * LANES:(c + 1) * LANES]
        partner = jnp.where(first_half, pltpu.roll(qn, LANES - HALF_ROT // 2, 1),
                            pltpu.roll(qn, HALF_ROT // 2, 1))
        qr = qn * cos + partner * sin
        fill = fill_ref[0:1, :] if c < D_Q // LANES else fill_ref[1:2, :]
        even = jnp.where(low_head, qr, fill).astype(BF16)
        odd = jnp.where(low_head, pltpu.roll(qr, HEAD_DIM, 1), fill).astype(BF16)
        if c < D_Q // LANES:
            q_ref[:, (2 * c) * LANES:(2 * c + 1) * LANES] = even
            q_ref[:, (2 * c + 1) * LANES:(2 * c + 2) * LANES] = odd
        else:
            k_ref[:, 0:LANES] = even
            k_ref[:, LANES:2 * LANES] = odd
    vv = z[:, o0 + D_Q + D_KV:]
    v_ref[:, 0:LANES] = jnp.where(low_head, vv, 1.0).astype(BF16)
    v_ref[:, LANES:2 * LANES] = jnp.where(low_head, pltpu.roll(vv, HEAD_DIM, 1), 1.0).astype(BF16)


def _inproj(x2, g1, w_in_bf, qkg, cos_t, sin_t, ones_bd, fill, seq_len):
    n = x2.shape[0]
    pos_tiles = seq_len // TM
    row = lambda i: (i, 0)
    const = lambda i: (0, 0)
    return pl.pallas_call(
        _inproj_body,
        grid=(n // TM,),
        in_specs=[
            pl.BlockSpec((TM, D_MODEL), row),
            pl.BlockSpec((1, D_MODEL), const),
            pl.BlockSpec((D_MODEL, D_IN), const),
            pl.BlockSpec((1, D_Q + D_KV), const),
            pl.BlockSpec((TM, LANES), lambda i: (i % pos_tiles, 0)),
            pl.BlockSpec((TM, LANES), lambda i: (i % pos_tiles, 0)),
            pl.BlockSpec((LANES, LANES), const),
            pl.BlockSpec((2, LANES), const),
        ],
        out_specs=[
            pl.BlockSpec((TM, D_CONV), row),
            pl.BlockSpec((TM, N_HEADS * LANES), row),
            pl.BlockSpec((TM, N_KV_HEADS * LANES), row),
            pl.BlockSpec((TM, N_KV_HEADS * LANES), row),
        ],
        out_shape=[
            jax.ShapeDtypeStruct((n, D_CONV), BF16),
            jax.ShapeDtypeStruct((n, N_HEADS * LANES), BF16),
            jax.ShapeDtypeStruct((n, N_KV_HEADS * LANES), BF16),
            jax.ShapeDtypeStruct((n, N_KV_HEADS * LANES), BF16),
        ],
        compiler_params=_cparams(("parallel",)),
        name="inproj",
    )(x2, g1, w_in_bf, qkg, cos_t, sin_t, ones_bd, fill)


CONV_ROWS = 64
CONV_SPAN = TT + 2 * HALO - F32_SUBLANES


def _conv_body(left_ref, main_ref, right_ref, w_ref, b_ref, lng_ref, lnb_ref, og_ref,
               out_ref, win_ref, conv_ref):
    i = pl.program_id(1)
    last = pl.num_programs(1) - 1
    left = left_ref[0].astype(F32)
    right = right_ref[0].astype(F32)
    win_ref[0, 0:HALO, :] = jnp.where(i > 0, left, 0.0)
    win_ref[0, HALO:HALO + TT, :] = main_ref[0].astype(F32)
    win_ref[0, HALO + TT:HALO + TT + HALO, :] = jnp.where(i < last, right, 0.0)
    for r in range(1, F32_SUBLANES):
        win_ref[r, 0:CONV_SPAN, :] = win_ref[0, r:r + CONV_SPAN, :]
    base = HALO - CONV_PAD

    def row_block(rr, carry):
        r0 = pl.multiple_of(rr * CONV_ROWS, CONV_ROWS)
        groups = CONV_ROWS // F32_SUBLANES
        reach = (base + CONV_WIDTH - 1) // F32_SUBLANES + 1
        for c in range(D_CONV // LANES):
            ls = slice(c * LANES, (c + 1) * LANES)
            accs = [None] * groups
            for shift in range(F32_SUBLANES):
                for q in range(groups + reach - 1):
                    uses = [(q - a, a * F32_SUBLANES + shift - base) for a in range(reach)
                            if 0 <= a * F32_SUBLANES + shift - base < CONV_WIDTH and 0 <= q - a < groups]
                    if not uses:
                        continue
                    tile = win_ref[shift, pl.ds(r0 + q * F32_SUBLANES, F32_SUBLANES), ls]
                    for g, k in uses:
                        term = tile * w_ref[k:k + 1, ls]
                        accs[g] = term if accs[g] is None else accs[g] + term
            for g in range(groups):
                conv_ref[pl.ds(r0 + g * F32_SUBLANES, F32_SUBLANES), ls] = accs[g]
        return carry

    lax.fori_loop(0, TT // CONV_ROWS, row_block, 0)
    cv = conv_ref[...] + b_ref[...]
    mu = jnp.mean(cv, axis=-1, keepdims=True)
    d = cv - mu
    var = jnp.mean(d * d, axis=-1, keepdims=True)
    y = d * lax.rsqrt(var + EPS) * lng_ref[...] + lnb_ref[...]
    y = y * _sigmoid(y)
    y = y * lax.rsqrt(jnp.mean(y * y, axis=-1, keepdims=True) + EPS) * og_ref[...]
    out_ref[0] = y.astype(BF16)


def _conv(gl3, dw_w, dw_b, ln_g, ln_b, out_g):
    b, t, _ = gl3.shape
    hb = TT // HALO
    n_halo = t // HALO
    const = lambda bb, i: (0, 0)
    return pl.pallas_call(
        _conv_body,
        grid=(b, t // TT),
        in_specs=[
            pl.BlockSpec((1, HALO, D_CONV), lambda bb, i: (bb, jnp.maximum(i * hb - 1, 0), 0)),
            pl.BlockSpec((1, TT, D_CONV), lambda bb, i: (bb, i, 0)),
            pl.BlockSpec((1, HALO, D_CONV), lambda bb, i: (bb, jnp.minimum((i + 1) * hb, n_halo - 1), 0)),
            pl.BlockSpec((CONV_WIDTH, D_CONV), const),
            pl.BlockSpec((1, D_CONV), const),
            pl.BlockSpec((1, D_CONV), const),
            pl.BlockSpec((1, D_CONV), const),
            pl.BlockSpec((1, D_CONV), const),
        ],
        out_specs=pl.BlockSpec((1, TT, D_CONV), lambda bb, i: (bb, i, 0)),
        out_shape=jax.ShapeDtypeStruct((b, t, D_CONV), BF16),
        scratch_shapes=[pltpu.VMEM((F32_SUBLANES, TT + 2 * HALO, D_CONV), F32),
                        pltpu.VMEM((TT, D_CONV), F32)],
        compiler_params=_cparams(("parallel", "parallel")),
        name="conv",
    )(gl3, gl3, gl3, dw_w, dw_b, ln_g, ln_b, out_g)


def _attn_body(q_ref, k_ref, v_ref, o_ref, qs_ref, m_ref, acc_ref):
    t = k_ref.shape[1]
    for h in range(KV_GROUP):
        qs_ref[h * TQ:(h + 1) * TQ, :] = q_ref[0, :, h * LANES:(h + 1) * LANES]
    m_ref[...] = jnp.full(m_ref.shape, -jnp.inf, F32)
    acc_ref[...] = jnp.zeros(acc_ref.shape, F32)

    def step(kt, carry):
        start = pl.multiple_of(kt * TK, TK)
        kk = k_ref[0, pl.ds(start, TK), :]
        vv = v_ref[0, pl.ds(start, TK), :]
        s = lax.dot_general(qs_ref[...], kk, (((1,), (1,)), ((), ())),
                            preferred_element_type=F32)
        m_old = m_ref[...]
        m_new = jnp.maximum(m_old, jnp.max(s, axis=-1, keepdims=True))
        p = jnp.exp2(s - m_new[:, 0:1])
        alpha = jnp.exp2(m_old - m_new)
        acc_ref[...] = alpha * acc_ref[...] + jnp.dot(p.astype(BF16), vv, preferred_element_type=F32)
        m_ref[...] = m_new
        return carry

    lax.fori_loop(0, t // TK, step, 0)
    _attn_finish(acc_ref, o_ref)


def _attn_shifted_body(q_ref, k_ref, v_ref, o_ref, qs_ref, acc_ref):
    t = k_ref.shape[1]
    for h in range(KV_GROUP):
        qs_ref[h * TQ:(h + 1) * TQ, :] = q_ref[0, :, h * LANES:(h + 1) * LANES]
    acc_ref[...] = jnp.zeros(acc_ref.shape, F32)

    chunks = math.gcd(KV_CHUNKS, t // TK)

    def step(kt, carry):
        qs = qs_ref[...]
        part = None
        for c in range(chunks):
            start = pl.multiple_of(kt * (chunks * TK) + c * TK, TK)
            s = lax.dot_general(qs, k_ref[0, pl.ds(start, TK), :], (((1,), (1,)), ((), ())),
                                preferred_element_type=F32)
            pv = jnp.dot(jnp.exp2(s).astype(BF16), v_ref[0, pl.ds(start, TK), :],
                         preferred_element_type=F32)
            part = pv if part is None else part + pv
        acc_ref[...] += part
        return carry

    lax.fori_loop(0, t // (chunks * TK), step, 0)
    _attn_finish(acc_ref, o_ref)


def _attn_finish(acc_ref, o_ref):
    acc = acc_ref[...]
    res = acc / pltpu.roll(acc, HEAD_DIM, 1)
    lane = lax.broadcasted_iota(jnp.int32, (TQ, LANES), 1)
    low = lane < HEAD_DIM
    for hp in range(KV_GROUP // 2):
        r0 = res[(2 * hp) * TQ:(2 * hp + 1) * TQ, :]
        r1 = res[(2 * hp + 1) * TQ:(2 * hp + 2) * TQ, :]
        o_ref[0, :, hp * LANES:(hp + 1) * LANES] = jnp.where(low, r0, pltpu.roll(r1, HEAD_DIM, 1)).astype(BF16)


def _attention_shifted(q3, k3, v3):
    b, t, _ = q3.shape
    gw = KV_GROUP * LANES
    return pl.pallas_call(
        _attn_shifted_body,
        grid=(b, N_KV_HEADS, t // TQ),
        in_specs=[
            pl.BlockSpec((1, TQ, gw), lambda bb, j, i: (bb, i, j)),
            pl.BlockSpec((1, t, LANES), lambda bb, j, i: (bb, 0, j)),
            pl.BlockSpec((1, t, LANES), lambda bb, j, i: (bb, 0, j)),
        ],
        out_specs=pl.BlockSpec((1, TQ, KV_GROUP * HEAD_DIM), lambda bb, j, i: (bb, i, j)),
        out_shape=jax.ShapeDtypeStruct((b, t, D_Q), BF16),
        scratch_shapes=[pltpu.VMEM((KV_GROUP * TQ, LANES), BF16),
                        pltpu.VMEM((KV_GROUP * TQ, LANES), F32)],
        compiler_params=_cparams(("parallel", "parallel", "parallel")),
        name="attention_shifted",
    )(q3, k3, v3)


def _attention(q3, k3, v3):
    b, t, _ = q3.shape
    gw = KV_GROUP * LANES
    return pl.pallas_call(
        _attn_body,
        grid=(b, N_KV_HEADS, t // TQ),
        in_specs=[
            pl.BlockSpec((1, TQ, gw), lambda bb, j, i: (bb, i, j)),
            pl.BlockSpec((1, t, LANES), lambda bb, j, i: (bb, 0, j)),
            pl.BlockSpec((1, t, LANES), lambda bb, j, i: (bb, 0, j)),
        ],
        out_specs=pl.BlockSpec((1, TQ, KV_GROUP * HEAD_DIM), lambda bb, j, i: (bb, i, j)),
        out_shape=jax.ShapeDtypeStruct((b, t, D_Q), BF16),
        scratch_shapes=[pltpu.VMEM((KV_GROUP * TQ, LANES), BF16),
                        pltpu.VMEM((KV_GROUP * TQ, LANES), F32),
                        pltpu.VMEM((KV_GROUP * TQ, LANES), F32)],
        compiler_params=_cparams(("parallel", "parallel", "parallel")),
        name="attention",
    )(q3, k3, v3)


def _outproj_body(x_ref, cn_ref, ao_ref, ag_ref, wc_ref, wa_ref, g2_ref, wrt_ref,
                  h_ref, xn_ref, aff_ref):
    ao = ao_ref[...].astype(F32)
    an = ao * lax.rsqrt(jnp.mean(ao * ao, axis=-1, keepdims=True) + EPS) * ag_ref[...]
    h = (x_ref[...]
         + jnp.dot(cn_ref[...], wc_ref[...], preferred_element_type=F32)
         + jnp.dot(an.astype(BF16), wa_ref[...], preferred_element_type=F32))
    h_ref[...] = h
    xn = h * lax.rsqrt(jnp.mean(h * h, axis=-1, keepdims=True) + EPS) * g2_ref[...]
    xn_ref[...] = xn.astype(BF16)
    logits = lax.dot_general(wrt_ref[...], xn, (((1,), (1,)), ((), ())),
                             precision=lax.Precision.HIGHEST,
                             preferred_element_type=F32)
    mx = jnp.max(logits, axis=0, keepdims=True)
    ex = jnp.exp(logits - mx)
    aff_ref[...] = ex / jnp.sum(ex, axis=0, keepdims=True)


def _outproj(x2, cn, ao, ag, wc, wa, g2, wrt):
    n = x2.shape[0]
    row = lambda i: (i, 0)
    const = lambda i: (0, 0)
    return pl.pallas_call(
        _outproj_body,
        grid=(n // TM,),
        in_specs=[
            pl.BlockSpec((TM, D_MODEL), row),
            pl.BlockSpec((TM, D_CONV), row),
            pl.BlockSpec((TM, D_Q), row),
            pl.BlockSpec((1, D_Q), const),
            pl.BlockSpec((D_CONV, D_MODEL), const),
            pl.BlockSpec((D_Q, D_MODEL), const),
            pl.BlockSpec((1, D_MODEL), const),
            pl.BlockSpec((N_EXPERTS, D_MODEL), const),
        ],
        out_specs=[
            pl.BlockSpec((TM, D_MODEL), row),
            pl.BlockSpec((TM, D_MODEL), row),
            pl.BlockSpec((N_EXPERTS, TM), lambda i: (0, i)),
        ],
        out_shape=[
            jax.ShapeDtypeStruct((n, D_MODEL), F32),
            jax.ShapeDtypeStruct((n, D_MODEL), BF16),
            jax.ShapeDtypeStruct((N_EXPERTS, n), F32),
        ],
        compiler_params=_cparams(("parallel",)),
        name="outproj",
    )(x2, cn, ao, ag, wc, wa, g2, wrt)


def _threshold_body(cap, aff_ref, thr_ref, need_ref):
    def step(it, lo):
        cand = lo | (jnp.int32(1) << (30 - it))
        bits = pltpu.bitcast(aff_ref[...], jnp.int32)
        cnt = jnp.sum((bits >= cand).astype(jnp.int32), axis=1, keepdims=True)
        return jnp.where(cnt >= cap, cand, lo)

    thr = lax.fori_loop(0, 31, step, jnp.zeros((N_EXPERTS, 1), jnp.int32))
    bits = pltpu.bitcast(aff_ref[...], jnp.int32)
    n_gt = jnp.sum((bits > thr).astype(jnp.int32), axis=1, keepdims=True)
    thr_ref[...] = jnp.broadcast_to(thr, thr_ref.shape)
    need_ref[...] = jnp.broadcast_to(cap - n_gt, need_ref.shape)


def _threshold(aff_t, cap):
    n = aff_t.shape[1]
    full = lambda: (0, 0)
    return pl.pallas_call(
        functools.partial(_threshold_body, cap),
        in_specs=[pl.BlockSpec((N_EXPERTS, n), full)],
        out_specs=[pl.BlockSpec((N_EXPERTS, LANES), full), pl.BlockSpec((N_EXPERTS, LANES), full)],
        out_shape=[jax.ShapeDtypeStruct((N_EXPERTS, LANES), jnp.int32),
                   jax.ShapeDtypeStruct((N_EXPERTS, LANES), jnp.int32)],
        compiler_params=pltpu.CompilerParams(vmem_limit_bytes=VMEM_LIMIT),
        name="threshold",
    )(aff_t)


def _mask_body(aff_ref, thr_ref, need_ref, tri_ref, gsel_ref, lpos_ref, cnt_ref, eqc_ref):
    @pl.when(pl.program_id(0) == 0)
    def _():
        eqc_ref[...] = jnp.zeros(eqc_ref.shape, F32)

    aff = aff_ref[...]
    bits = pltpu.bitcast(aff, jnp.int32)
    thr = thr_ref[:, 0:1]
    need = need_ref[:, 0:1].astype(F32)
    gt = bits > thr
    eq = bits == thr
    eq_f = jnp.where(eq, 1.0, 0.0)
    eq_rank = eqc_ref[:, 0:1] + jnp.dot(eq_f.astype(BF16), tri_ref[...], preferred_element_type=F32)
    sel = gt | (eq & (eq_rank < need))
    sel_f = jnp.where(sel, 1.0, 0.0)
    gsel_ref[...] = jnp.where(sel, aff, -1.0)
    lpos_ref[...] = jnp.dot(sel_f.astype(BF16), tri_ref[...], preferred_element_type=F32)
    cnt = jnp.sum(sel_f, axis=1, keepdims=True)
    cnt_ref[0] = jnp.broadcast_to(cnt, (N_EXPERTS, LANES)).astype(jnp.int32)
    eqc_ref[...] = eqc_ref[...] + jnp.sum(eq_f, axis=1, keepdims=True)


def _masks(aff_t, thr, need, tri):
    n = aff_t.shape[1]
    nt = n // TR
    const = lambda i: (0, 0)
    tile = lambda i: (0, i)
    return pl.pallas_call(
        _mask_body,
        grid=(nt,),
        in_specs=[
            pl.BlockSpec((N_EXPERTS, TR), tile),
            pl.BlockSpec((N_EXPERTS, LANES), const),
            pl.BlockSpec((N_EXPERTS, LANES), const),
            pl.BlockSpec((TR, TR), const),
        ],
        out_specs=[
            pl.BlockSpec((N_EXPERTS, TR), tile),
            pl.BlockSpec((N_EXPERTS, TR), tile),
            pl.BlockSpec((1, N_EXPERTS, LANES), lambda i: (i, 0, 0)),
        ],
        out_shape=[
            jax.ShapeDtypeStruct((N_EXPERTS, n), F32),
            jax.ShapeDtypeStruct((N_EXPERTS, n), F32),
            jax.ShapeDtypeStruct((nt, N_EXPERTS, LANES), jnp.int32),
        ],
        scratch_shapes=[pltpu.VMEM((N_EXPERTS, LANES), F32)],
        compiler_params=_cparams(("arbitrary",)),
        name="masks",
    )(aff_t, thr, need, tri)


def _onehot_rows(gsel_row, lpos_row, chunk, value):
    slot = lax.broadcasted_iota(jnp.int32, (CH, TR), 0).astype(F32) + (chunk * CH).astype(F32)
    return jnp.where((gsel_row >= 0.0) & (lpos_row == slot), value, 0.0).astype(BF16)


def _dispatch_body(off_ref, nch_ref, xn_ref, gsel_ref, lpos_ref, xe_in_ref, xe_ref, stage_ref, sem_ref, xsem_ref):
    del xe_in_ref
    n = pl.program_id(0)
    nt = pl.num_programs(0)

    def copy(e, slot, sem, chunk):
        row0 = pl.multiple_of(off_ref[e * nt + n] + chunk * CH, BF16_SUBLANES)
        return pltpu.make_async_copy(stage_ref.at[slot], xe_ref.at[e, pl.ds(row0, CH)], sem)

    def rows_of(e, chunk):
        hot = _onehot_rows(gsel_ref[pl.ds(e, 1), :], lpos_ref[pl.ds(e, 1), :], chunk, 1.0)
        return jnp.dot(hot, xn_ref[...], preferred_element_type=F32).astype(BF16)

    for e in range(N_EXPERTS):
        stage_ref[e] = rows_of(e, jnp.int32(0))
        copy(e, e, sem_ref.at[e], 0).start()
    for e in range(N_EXPERTS):
        copy(e, e, sem_ref.at[e], 0).wait()

    def per_expert(e, carry):
        def per_chunk(chunk, c):
            stage_ref[N_EXPERTS] = rows_of(e, chunk)
            cp = copy(e, N_EXPERTS, xsem_ref.at[0], chunk)
            cp.start()
            cp.wait()
            return c
        return lax.fori_loop(1, nch_ref[e * nt + n], per_chunk, carry)

    lax.fori_loop(0, N_EXPERTS, per_expert, 0)


def _dispatch(off, nch, xn2, gsel, lpos, xe_init):
    n = xn2.shape[0]
    nt = n // TR
    grid_spec = pltpu.PrefetchScalarGridSpec(
        num_scalar_prefetch=2,
        grid=(nt,),
        in_specs=[
            pl.BlockSpec((TR, D_MODEL), lambda i, *_: (i, 0)),
            pl.BlockSpec((N_EXPERTS, TR), lambda i, *_: (0, i)),
            pl.BlockSpec((N_EXPERTS, TR), lambda i, *_: (0, i)),
            pl.BlockSpec(memory_space=pl.ANY),
        ],
        out_specs=pl.BlockSpec(memory_space=pl.ANY),
        scratch_shapes=[pltpu.VMEM((N_EXPERTS + 1, CH, D_MODEL), BF16),
                        pltpu.SemaphoreType.DMA((N_EXPERTS,)),
                        pltpu.SemaphoreType.DMA((1,))],
    )
    return pl.pallas_call(
        _dispatch_body,
        grid_spec=grid_spec,
        out_shape=jax.ShapeDtypeStruct(xe_init.shape, BF16),
        input_output_aliases={5: 0},
        compiler_params=_cparams(("arbitrary",)),
        name="dispatch",
    )(off, nch, xn2, gsel, lpos, xe_init)


def _ffn_body(ntile_ref, x_ref, wg_ref, wu_ref, wd_ref, y_ref):
    e = pl.program_id(0)
    i = pl.program_id(1)

    @pl.when(i < ntile_ref[e])
    def _():
        x = x_ref[0]
        g = jnp.dot(x, wg_ref[0], preferred_element_type=F32)
        u = jnp.dot(x, wu_ref[0], preferred_element_type=F32)
        hid = (g * _sigmoid(g) * u).astype(BF16)
        y_ref[0] = jnp.dot(hid, wd_ref[0], preferred_element_type=F32).astype(BF16)

    @pl.when(i >= ntile_ref[e])
    def _():
        y_ref[...] = jnp.zeros(y_ref.shape, BF16)


def _ffn(ntile, xe, wg, wu, wd):
    cap_rows = xe.shape[1]
    rows = lambda e, i, nt_ref: (e, jnp.minimum(i, nt_ref[e] - 1), 0)
    wmap = lambda e, i, nt_ref: (e, 0, 0)
    grid_spec = pltpu.PrefetchScalarGridSpec(
        num_scalar_prefetch=1,
        grid=(N_EXPERTS, cap_rows // TMF),
        in_specs=[
            pl.BlockSpec((1, TMF, D_MODEL), rows),
            pl.BlockSpec((1, D_MODEL, D_FF_EXPERT), wmap),
            pl.BlockSpec((1, D_MODEL, D_FF_EXPERT), wmap),
            pl.BlockSpec((1, D_FF_EXPERT, D_MODEL), wmap),
        ],
        out_specs=pl.BlockSpec((1, TMF, D_MODEL), lambda e, i, nt_ref: (e, i, 0)),
    )
    return pl.pallas_call(
        _ffn_body,
        grid_spec=grid_spec,
        out_shape=jax.ShapeDtypeStruct(xe.shape, BF16),
        compiler_params=_cparams(("arbitrary", "arbitrary")),
        name="expert_ffn",
    )(ntile, xe, wg, wu, wd)


def _combine_body(off_ref, nch_ref, h_ref, gsel_ref, lpos_ref, p_ref, wpp_ref, pg_ref,
                  wpg_ref, bpg_ref, ye_ref, y_ref, ybuf_ref, hot_ref, acc_ref, sem_ref, xsem_ref):
    n = pl.program_id(0)
    nt = pl.num_programs(0)
    tn = (((0,), (0,)), ((), ()))

    def copy(e, slot, sem, chunk):
        row0 = pl.multiple_of(off_ref[e * nt + n] + chunk * CH, BF16_SUBLANES)
        return pltpu.make_async_copy(ye_ref.at[e, pl.ds(row0, CH)], ybuf_ref.at[pl.ds(slot * CH, CH)], sem)

    def gated_hot(e, chunk):
        g = gsel_ref[pl.ds(e, 1), :]
        return _onehot_rows(g, lpos_ref[pl.ds(e, 1), :], chunk, g)

    for e in range(N_EXPERTS):
        copy(e, e, sem_ref.at[e], 0).start()
    for e in range(N_EXPERTS):
        hot_ref[e * CH:(e + 1) * CH, :] = gated_hot(e, jnp.int32(0))
    for e in range(N_EXPERTS):
        copy(e, e, sem_ref.at[e], 0).wait()
    acc_ref[...] = h_ref[...] + lax.dot_general(hot_ref[...], ybuf_ref[pl.ds(0, N_EXPERTS * CH), :], tn,
                                                preferred_element_type=F32)

    def per_expert(e, carry):
        def per_chunk(chunk, c):
            cp = copy(e, N_EXPERTS, xsem_ref.at[0], chunk)
            cp.start()
            cp.wait()
            acc_ref[...] += lax.dot_general(gated_hot(e, chunk), ybuf_ref[pl.ds(N_EXPERTS * CH, CH), :], tn,
                                            preferred_element_type=F32)
            return c
        return lax.fori_loop(1, nch_ref[e * nt + n], per_chunk, carry)

    lax.fori_loop(0, N_EXPERTS, per_expert, 0)

    h2 = acc_ref[...]
    emb = jnp.dot(p_ref[...].astype(BF16), wpp_ref[...], preferred_element_type=F32)
    hn = h2 * lax.rsqrt(jnp.mean(h2 * h2, axis=-1, keepdims=True) + EPS) * pg_ref[...]
    gate = _sigmoid(jnp.dot(hn.astype(BF16), wpg_ref[...], preferred_element_type=F32) + bpg_ref[...])
    y_ref[...] = h2 + gate * emb


def _combine(off, nch, h1, gsel, lpos, p2, wpp, pg, wpg, bpg, ye):
    n = h1.shape[0]
    nt = n // TR
    row = lambda i, *_: (i, 0)
    tile = lambda i, *_: (0, i)
    const = lambda i, *_: (0, 0)
    grid_spec = pltpu.PrefetchScalarGridSpec(
        num_scalar_prefetch=2,
        grid=(nt,),
        in_specs=[
            pl.BlockSpec((TR, D_MODEL), row),
            pl.BlockSpec((N_EXPERTS, TR), tile),
            pl.BlockSpec((N_EXPERTS, TR), tile),
            pl.BlockSpec((TR, D_PLE), row),
            pl.BlockSpec((D_PLE, D_MODEL), const),
            pl.BlockSpec((1, D_MODEL), const),
            pl.BlockSpec((D_MODEL, D_MODEL), const),
            pl.BlockSpec((1, D_MODEL), const),
            pl.BlockSpec(memory_space=pl.ANY),
        ],
        out_specs=pl.BlockSpec((TR, D_MODEL), row),
        scratch_shapes=[pltpu.VMEM(((N_EXPERTS + 1) * CH, D_MODEL), BF16),
                        pltpu.VMEM((N_EXPERTS * CH, TR), BF16),
                        pltpu.VMEM((TR, D_MODEL), F32),
                        pltpu.SemaphoreType.DMA((N_EXPERTS,)),
                        pltpu.SemaphoreType.DMA((1,))],
    )
    return pl.pallas_call(
        _combine_body,
        grid_spec=grid_spec,
        out_shape=jax.ShapeDtypeStruct((n, D_MODEL), F32),
        compiler_params=_cparams(("arbitrary",)),
        name="combine",
    )(off, nch, h1, gsel, lpos, p2, wpp, pg, wpg, bpg, ye)


def _rope_tables(t):
    rows = t // GRID_W
    row_idx = jnp.repeat(jnp.arange(rows, dtype=F32), GRID_W)
    col_idx = jnp.tile(jnp.arange(GRID_W, dtype=F32), rows)
    freqs = 1.0 / (ROPE_THETA ** (jnp.arange(0, HALF_ROT, 2, dtype=F32) / HALF_ROT))
    ang_r = row_idx[:, None] * freqs[None, :]
    ang_c = col_idx[:, None] * freqs[None, :]
    cr, sr, cc, sc = jnp.cos(ang_r), jnp.sin(ang_r), jnp.cos(ang_c), jnp.sin(ang_c)
    cos_h = jnp.concatenate([cr, cr, cc, cc], axis=-1)
    sin_h = jnp.concatenate([-sr, sr, -sc, sc], axis=-1)
    return jnp.tile(cos_h, (1, LANES // HEAD_DIM)), jnp.tile(sin_h, (1, LANES // HEAD_DIM))


def _prepare_weights(norm1_g, w_in, conv_dw_w, conv_dw_b, conv_ln_g, conv_ln_b, q_norm_g, k_norm_g,
                     conv_out_g, attn_out_g, w_out, norm2_g, w_router, w_gate, w_up, w_down,
                     ple_proj, ple_norm_g, ple_gate_w, ple_gate_b):
    i = 0
    q_scale = (HEAD_DIM ** -0.5) * math.log2(math.e)
    qkg = jnp.concatenate([jnp.tile(q_norm_g[i] * q_scale, N_HEADS), jnp.tile(k_norm_g[i], N_KV_HEADS)])
    bound = HEAD_DIM * jnp.max(jnp.abs(q_norm_g[i] * q_scale)) * jnp.max(jnp.abs(k_norm_g[i]))
    use_shift = bound <= MAX_SCORE_SHIFT
    lane = jnp.arange(LANES)
    fill = jnp.stack([jnp.where(lane == HEAD_DIM, 1.0, 0.0),
                      jnp.where(lane == HEAD_DIM, -jnp.where(use_shift, bound, 0.0), 0.0)]).astype(F32)
    ones_bd = (lane[:, None] // HEAD_DIM == lane[None, :] // HEAD_DIM).astype(BF16)
    tok = jnp.arange(TR)
    return dict(
        g1=norm1_g[i][None, :], w_in=w_in[i].astype(BF16), qkg=qkg[None, :], ones_bd=ones_bd,
        fill=fill, use_shift=use_shift,
        dw_w=conv_dw_w[i], dw_b=conv_dw_b[i][None, :], ln_g=conv_ln_g[i][None, :], ln_b=conv_ln_b[i][None, :],
        conv_out_g=conv_out_g[i][None, :], attn_out_g=attn_out_g[i][None, :],
        w_out_c=w_out[i, :D_CONV].astype(BF16), w_out_a=w_out[i, D_CONV:].astype(BF16),
        g2=norm2_g[i][None, :], w_router_t=w_router[i].T,
        wg=w_gate[i].astype(BF16), wu=w_up[i].astype(BF16), wd=w_down[i].astype(BF16),
        tri=(tok[:, None] < tok[None, :]).astype(BF16),
        wpp=ple_proj[i].astype(BF16), pg=ple_norm_g[i][None, :], wpg=ple_gate_w[i].astype(BF16),
        bpg=ple_gate_b[i][None, :],
    )


def _trunk(x, p, w):
    b, t, _ = x.shape
    n = b * t
    assert t % TM == 0 and t % TK == 0 and t % TT == 0 and n % TR == 0
    cap = CAPACITY_FACTOR * n // N_EXPERTS
    nt = n // TR
    x2 = x.reshape(n, D_MODEL)
    cos_t, sin_t = _rope_tables(t)

    gl, q, k, v = _inproj(x2, w["g1"], w["w_in"], w["qkg"], cos_t, sin_t, w["ones_bd"], w["fill"], t)
    cn = _conv(gl.reshape(b, t, D_CONV), w["dw_w"], w["dw_b"], w["ln_g"], w["ln_b"], w["conv_out_g"])
    ao = lax.cond(w["use_shift"], _attention_shifted, _attention,
                  q.reshape(b, t, -1), k.reshape(b, t, -1), v.reshape(b, t, -1))
    h1, xn2, aff_t = _outproj(x2, cn.reshape(n, D_CONV), ao.reshape(n, D_Q), w["attn_out_g"],
                              w["w_out_c"], w["w_out_a"], w["g2"], w["w_router_t"])

    thr, need = _threshold(aff_t, cap)
    gsel, lpos, cnt = _masks(aff_t, thr, need, w["tri"])
    cnt = cnt[:, :, 0].T
    seg = (cnt + BF16_SUBLANES - 1) // BF16_SUBLANES * BF16_SUBLANES
    ends = jnp.cumsum(seg, axis=1)
    off = (ends - seg).astype(jnp.int32).reshape(-1)
    nch = ((cnt + CH - 1) // CH).astype(jnp.int32).reshape(-1)
    cap_rows = (cap + nt * (BF16_SUBLANES - 1) + CH + TMF - 1) // TMF * TMF
    ntile = jnp.maximum((ends[:, -1] + TMF - 1) // TMF, 1).astype(jnp.int32)

    xe = _dispatch(off, nch, xn2, gsel, lpos, jnp.zeros((N_EXPERTS, cap_rows, D_MODEL), BF16))
    ye = _ffn(ntile, xe, w["wg"], w["wu"], w["wd"])
    y = _combine(off, nch, h1, gsel, lpos, p.reshape(n, D_PLE), w["wpp"], w["pg"], w["wpg"],
                 w["bpg"], ye)
    return y.reshape(b, t, D_MODEL)


def kernel(x_prompt, x_sample, p_prompt, p_sample, norm1_g, w_in, conv_dw_w, conv_dw_b, conv_ln_g, conv_ln_b, q_norm_g, k_norm_g, conv_out_g, attn_out_g, w_out, norm2_g, w_router, w_gate, w_up, w_down, ple_proj, ple_norm_g, ple_gate_w, ple_gate_b):
    w = _prepare_weights(norm1_g, w_in, conv_dw_w, conv_dw_b, conv_ln_g, conv_ln_b, q_norm_g, k_norm_g,
                         conv_out_g, attn_out_g, w_out, norm2_g, w_router, w_gate, w_up, w_down,
                         ple_proj, ple_norm_g, ple_gate_w, ple_gate_b)
    y_prompt = _trunk(x_prompt, p_prompt[0], w)
    y_sample = _trunk(x_sample, p_sample[0], w)
    return (y_prompt, y_sample)
```

```python
import functools
import math

import jax
import jax.numpy as jnp
from jax import lax
from jax.experimental import pallas as pl
from jax.experimental.pallas import tpu as pltpu

D_MODEL = 1024
D_CONV = 512
CONV_WIDTH = 31
CONV_PAD = CONV_WIDTH // 2
N_HEADS = 8
N_KV_HEADS = 2
HEAD_DIM = 64
KV_GROUP = N_HEADS // N_KV_HEADS
D_Q = N_HEADS * HEAD_DIM
D_KV = N_KV_HEADS * HEAD_DIM
D_IN = 2 * D_CONV + D_Q + 2 * D_KV
HALF_ROT = HEAD_DIM // 2
ROPE_THETA = 10000.0
GRID_W = 64
N_EXPERTS = 16
CAPACITY_FACTOR = 2
D_FF_EXPERT = 2048
D_PLE = 256
EPS = 1e-6

LANES = 128
F32_SUBLANES = 8
BF16_SUBLANES = 16
VMEM_LIMIT = 56 * 1024 * 1024

TM = 512
TM_OUT = 1024
TT = 256
HALO = 16
TQ = 256
TK = 512
KV_CHUNKS = 8
MAX_SCORE_SHIFT = 40.0
TR = 512
CH = 128
TMF = 512

F32 = jnp.float32
BF16 = jnp.bfloat16


def _cparams(sem):
    return pltpu.CompilerParams(dimension_semantics=sem, vmem_limit_bytes=VMEM_LIMIT)


def _sigmoid(x):
    return 1.0 / (1.0 + jnp.exp(-x))


def _inproj_body(x_ref, g1_ref, w_ref, qkg_ref, cos_ref, sin_ref, ones_ref, fill_ref,
                 gl_ref, q_ref, k_ref, v_ref):
    for blk in range(TM_OUT // TM):
        _inproj_rows(slice(blk * TM, (blk + 1) * TM), x_ref, g1_ref, w_ref, qkg_ref, cos_ref, sin_ref,
                     ones_ref, fill_ref, gl_ref, q_ref, k_ref, v_ref)


def _inproj_rows(rows, x_ref, g1_ref, w_ref, qkg_ref, cos_ref, sin_ref, ones_ref, fill_ref,
                 gl_ref, q_ref, k_ref, v_ref):
    x = x_ref[rows, :]
    a = x * lax.rsqrt(jnp.mean(x * x, axis=-1, keepdims=True) + EPS) * g1_ref[...]
    z = jnp.dot(a.astype(BF16), w_ref[...], preferred_element_type=F32)
    val = z[:, :D_CONV]
    gate = z[:, D_CONV:2 * D_CONV]
    gl_ref[rows, :] = (val * _sigmoid(gate)).astype(BF16)

    lane = lax.broadcasted_iota(jnp.int32, (x.shape[0], LANES), 1)
    first_half = (lane % HALF_ROT) < (HALF_ROT // 2)
    low_head = lane < HEAD_DIM
    cos = cos_ref[rows, :]
    sin = sin_ref[rows, :]
    o0 = 2 * D_CONV
    n_chunks = (D_Q + D_KV) // LANES
    for c in range(n_chunks):
        qc = z[:, o0 + c * LANES:o0 + (c + 1) * LANES]
        ssum = jnp.dot((qc * qc).astype(BF16), ones_ref[...], preferred_element_type=F32)
        qn = qc * lax.rsqrt(ssum * (1.0 / HEAD_DIM) + EPS) * qkg_ref[:, c * LANES:(c + 1) * LANES]
        partner = jnp.where(first_half, pltpu.roll(qn, LANES - HALF_ROT // 2, 1),
                            pltpu.roll(qn, HALF_ROT // 2, 1))
        qr = qn * cos + partner * sin
        fill = fill_ref[0:1, :] if c < D_Q // LANES else fill_ref[1:2, :]
        even = jnp.where(low_head, qr, fill).astype(BF16)
        odd = jnp.where(low_head, pltpu.roll(qr, HEAD_DIM, 1), fill).astype(BF16)
        if c < D_Q // LANES:
            q_ref[rows, (2 * c) * LANES:(2 * c + 1) * LANES] = even
            q_ref[rows, (2 * c + 1) * LANES:(2 * c + 2) * LANES] = odd
        else:
            k_ref[rows, 0:LANES] = even
            k_ref[rows, LANES:2 * LANES] = odd
    vv = z[:, o0 + D_Q + D_KV:]
    v_ref[rows, 0:LANES] = jnp.where(low_head, vv, 1.0).astype(BF16)
    v_ref[rows, LANES:2 * LANES] = jnp.where(low_head, pltpu.roll(vv, HEAD_DIM, 1), 1.0).astype(BF16)


def _inproj(x2, g1, w_in_bf, qkg, cos_t, sin_t, ones_bd, fill, seq_len):
    n = x2.shape[0]
    pos_tiles = seq_len // TM_OUT
    row = lambda i: (i, 0)
    const = lambda i: (0, 0)
    return pl.pallas_call(
        _inproj_body,
        grid=(n // TM_OUT,),
        in_specs=[
            pl.BlockSpec((TM_OUT, D_MODEL), row),
            pl.BlockSpec((1, D_MODEL), const),
            pl.BlockSpec((D_MODEL, D_IN), const),
            pl.BlockSpec((1, D_Q + D_KV), const),
            pl.BlockSpec((TM_OUT, LANES), lambda i: (i % pos_tiles, 0)),
            pl.BlockSpec((TM_OUT, LANES), lambda i: (i % pos_tiles, 0)),
            pl.BlockSpec((LANES, LANES), const),
            pl.BlockSpec((2, LANES), const),
        ],
        out_specs=[
            pl.BlockSpec((TM_OUT, D_CONV), row),
            pl.BlockSpec((TM_OUT, N_HEADS * LANES), row),
            pl.BlockSpec((TM_OUT, N_KV_HEADS * LANES), row),
            pl.BlockSpec((TM_OUT, N_KV_HEADS * LANES), row),
        ],
        out_shape=[
            jax.ShapeDtypeStruct((n, D_CONV), BF16),
            jax.ShapeDtypeStruct((n, N_HEADS * LANES), BF16),
            jax.ShapeDtypeStruct((n, N_KV_HEADS * LANES), BF16),
            jax.ShapeDtypeStruct((n, N_KV_HEADS * LANES), BF16),
        ],
        compiler_params=_cparams(("parallel",)),
        name="inproj",
    )(x2, g1, w_in_bf, qkg, cos_t, sin_t, ones_bd, fill)


CONV_ROWS = 64
CONV_SPAN = TT + 2 * HALO - F32_SUBLANES


def _conv_body(left_ref, main_ref, right_ref, w_ref, b_ref, lng_ref, lnb_ref, og_ref,
               out_ref, win_ref, conv_ref):
    i = pl.program_id(1)
    last = pl.num_programs(1) - 1
    left = left_ref[0].astype(F32)
    right = right_ref[0].astype(F32)
    win_ref[0, 0:HALO, :] = jnp.where(i > 0, left, 0.0)
    win_ref[0, HALO:HALO + TT, :] = main_ref[0].astype(F32)
    win_ref[0, HALO + TT:HALO + TT + HALO, :] = jnp.where(i < last, right, 0.0)
    for r in range(1, F32_SUBLANES):
        win_ref[r, 0:CONV_SPAN, :] = win_ref[0, r:r + CONV_SPAN, :]
    base = HALO - CONV_PAD

    def row_block(rr, carry):
        r0 = pl.multiple_of(rr * CONV_ROWS, CONV_ROWS)
        groups = CONV_ROWS // F32_SUBLANES
        reach = (base + CONV_WIDTH - 1) // F32_SUBLANES + 1
        for c in range(D_CONV // LANES):
            ls = slice(c * LANES, (c + 1) * LANES)
            accs = [None] * groups
            for shift in range(F32_SUBLANES):
                for q in range(groups + reach - 1):
                    uses = [(q - a, a * F32_SUBLANES + shift - base) for a in range(reach)
                            if 0 <= a * F32_SUBLANES + shift - base < CONV_WIDTH and 0 <= q - a < groups]
                    if not uses:
                        continue
                    tile = win_ref[shift, pl.ds(r0 + q * F32_SUBLANES, F32_SUBLANES), ls]
                    for g, k in uses:
                        term = tile * w_ref[k:k + 1, ls]
                        accs[g] = term if accs[g] is None else accs[g] + term
            for g in range(groups):
                conv_ref[pl.ds(r0 + g * F32_SUBLANES, F32_SUBLANES), ls] = accs[g]
        return carry

    lax.fori_loop(0, TT // CONV_ROWS, row_block, 0)
    cv = conv_ref[...] + b_ref[...]
    mu = jnp.mean(cv, axis=-1, keepdims=True)
    d = cv - mu
    var = jnp.mean(d * d, axis=-1, keepdims=True)
    y = d * lax.rsqrt(var + EPS) * lng_ref[...] + lnb_ref[...]
    y = y * _sigmoid(y)
    y = y * lax.rsqrt(jnp.mean(y * y, axis=-1, keepdims=True) + EPS) * og_ref[...]
    out_ref[0] = y.astype(BF16)


def _conv(gl3, dw_w, dw_b, ln_g, ln_b, out_g):
    b, t, _ = gl3.shape
    hb = TT // HALO
    n_halo = t // HALO
    const = lambda bb, i: (0, 0)
    return pl.pallas_call(
        _conv_body,
        grid=(b, t // TT),
        in_specs=[
            pl.BlockSpec((1, HALO, D_CONV), lambda bb, i: (bb, jnp.maximum(i * hb - 1, 0), 0)),
            pl.BlockSpec((1, TT, D_CONV), lambda bb, i: (bb, i, 0)),
            pl.BlockSpec((1, HALO, D_CONV), lambda bb, i: (bb, jnp.minimum((i + 1) * hb, n_halo - 1), 0)),
            pl.BlockSpec((CONV_WIDTH, D_CONV), const),
            pl.BlockSpec((1, D_CONV), const),
            pl.BlockSpec((1, D_CONV), const),
            pl.BlockSpec((1, D_CONV), const),
            pl.BlockSpec((1, D_CONV), const),
        ],
        out_specs=pl.BlockSpec((1, TT, D_CONV), lambda bb, i: (bb, i, 0)),
        out_shape=jax.ShapeDtypeStruct((b, t, D_CONV), BF16),
        scratch_shapes=[pltpu.VMEM((F32_SUBLANES, TT + 2 * HALO, D_CONV), F32),
                        pltpu.VMEM((TT, D_CONV), F32)],
        compiler_params=_cparams(("parallel", "parallel")),
        name="conv",
    )(gl3, gl3, gl3, dw_w, dw_b, ln_g, ln_b, out_g)


def _attn_body(q_ref, k_ref, v_ref, o_ref, qs_ref, m_ref, acc_ref):
    t = k_ref.shape[1]
    for h in range(KV_GROUP):
        qs_ref[h * TQ:(h + 1) * TQ, :] = q_ref[0, :, h * LANES:(h + 1) * LANES]
    m_ref[...] = jnp.full(m_ref.shape, -jnp.inf, F32)
    acc_ref[...] = jnp.zeros(acc_ref.shape, F32)

    def step(kt, carry):
        start = pl.multiple_of(kt * TK, TK)
        kk = k_ref[0, pl.ds(start, TK), :]
        vv = v_ref[0, pl.ds(start, TK), :]
        s = lax.dot_general(qs_ref[...], kk, (((1,), (1,)), ((), ())),
                            preferred_element_type=F32)
        m_old = m_ref[...]
        m_new = jnp.maximum(m_old, jnp.max(s, axis=-1, keepdims=True))
        p = jnp.exp2(s - m_new[:, 0:1])
        alpha = jnp.exp2(m_old - m_new)
        acc_ref[...] = alpha * acc_ref[...] + jnp.dot(p.astype(BF16), vv, preferred_element_type=F32)
        m_ref[...] = m_new
        return carry

    lax.fori_loop(0, t // TK, step, 0)
    _attn_finish(acc_ref, o_ref)


def _attn_shifted_body(q_ref, k_ref, v_ref, o_ref, qs_ref, acc_ref):
    t = k_ref.shape[1]
    for h in range(KV_GROUP):
        qs_ref[h * TQ:(h + 1) * TQ, :] = q_ref[0, :, h * LANES:(h + 1) * LANES]
    acc_ref[...] = jnp.zeros(acc_ref.shape, F32)

    chunks = math.gcd(KV_CHUNKS, t // TK)

    def step(kt, carry):
        qs = qs_ref[...]
        part = None
        for c in range(chunks):
            start = pl.multiple_of(kt * (chunks * TK) + c * TK, TK)
            s = lax.dot_general(qs, k_ref[0, pl.ds(start, TK), :], (((1,), (1,)), ((), ())),
                                preferred_element_type=F32)
            pv = jnp.dot(jnp.exp2(s).astype(BF16), v_ref[0, pl.ds(start, TK), :],
                         preferred_element_type=F32)
            part = pv if part is None else part + pv
        acc_ref[...] += part
        return carry

    lax.fori_loop(0, t // (chunks * TK), step, 0)
    _attn_finish(acc_ref, o_ref)


def _attn_finish(acc_ref, o_ref):
    acc = acc_ref[...]
    res = acc / pltpu.roll(acc, HEAD_DIM, 1)
    lane = lax.broadcasted_iota(jnp.int32, (TQ, LANES), 1)
    low = lane < HEAD_DIM
    for hp in range(KV_GROUP // 2):
        r0 = res[(2 * hp) * TQ:(2 * hp + 1) * TQ, :]
        r1 = res[(2 * hp + 1) * TQ:(2 * hp + 2) * TQ, :]
        o_ref[0, :, hp * LANES:(hp + 1) * LANES] = jnp.where(low, r0, pltpu.roll(r1, HEAD_DIM, 1)).astype(BF16)


def _attention_shifted(q3, k3, v3):
    b, t, _ = q3.shape
    gw = KV_GROUP * LANES
    return pl.pallas_call(
        _attn_shifted_body,
        grid=(b, N_KV_HEADS, t // TQ),
        in_specs=[
            pl.BlockSpec((1, TQ, gw), lambda bb, j, i: (bb, i, j)),
            pl.BlockSpec((1, t, LANES), lambda bb, j, i: (bb, 0, j)),
            pl.BlockSpec((1, t, LANES), lambda bb, j, i: (bb, 0, j)),
        ],
        out_specs=pl.BlockSpec((1, TQ, KV_GROUP * HEAD_DIM), lambda bb, j, i: (bb, i, j)),
        out_shape=jax.ShapeDtypeStruct((b, t, D_Q), BF16),
        scratch_shapes=[pltpu.VMEM((KV_GROUP * TQ, LANES), BF16),
                        pltpu.VMEM((KV_GROUP * TQ, LANES), F32)],
        compiler_params=_cparams(("parallel", "parallel", "parallel")),
        name="attention_shifted",
    )(q3, k3, v3)


def _attention(q3, k3, v3):
    b, t, _ = q3.shape
    gw = KV_GROUP * LANES
    return pl.pallas_call(
        _attn_body,
        grid=(b, N_KV_HEADS, t // TQ),
        in_specs=[
            pl.BlockSpec((1, TQ, gw), lambda bb, j, i: (bb, i, j)),
            pl.BlockSpec((1, t, LANES), lambda bb, j, i: (bb, 0, j)),
            pl.BlockSpec((1, t, LANES), lambda bb, j, i: (bb, 0, j)),
        ],
        out_specs=pl.BlockSpec((1, TQ, KV_GROUP * HEAD_DIM), lambda bb, j, i: (bb, i, j)),
        out_shape=jax.ShapeDtypeStruct((b, t, D_Q), BF16),
        scratch_shapes=[pltpu.VMEM((KV_GROUP * TQ, LANES), BF16),
                        pltpu.VMEM((KV_GROUP * TQ, LANES), F32),
                        pltpu.VMEM((KV_GROUP * TQ, LANES), F32)],
        compiler_params=_cparams(("parallel", "parallel", "parallel")),
        name="attention",
    )(q3, k3, v3)


def _outproj_body(x_ref, cn_ref, ao_ref, ag_ref, wc_ref, wa_ref, g2_ref, wr_ref,
                  h_ref, xn_ref, aff_ref):
    for blk in range(TM_OUT // TM):
        rows = slice(blk * TM, (blk + 1) * TM)
        ao = ao_ref[rows, :].astype(F32)
        an = ao * lax.rsqrt(jnp.mean(ao * ao, axis=-1, keepdims=True) + EPS) * ag_ref[...]
        h = (x_ref[rows, :]
             + jnp.dot(cn_ref[rows, :], wc_ref[...], preferred_element_type=F32)
             + jnp.dot(an.astype(BF16), wa_ref[...], preferred_element_type=F32))
        h_ref[rows, :] = h
        xn = h * lax.rsqrt(jnp.mean(h * h, axis=-1, keepdims=True) + EPS) * g2_ref[...]
        xn_hi = xn.astype(BF16)
        xn_ref[rows, :] = xn_hi
        xn_lo = (xn - xn_hi.astype(F32)).astype(BF16)
        parts = (jnp.dot(xn_hi, wr_ref[...], preferred_element_type=F32)
                 + jnp.dot(xn_lo, wr_ref[...], preferred_element_type=F32))
        parts_t = parts.T
        logits = parts_t[0:N_EXPERTS, :] + parts_t[N_EXPERTS:2 * N_EXPERTS, :]
        mx = jnp.max(logits, axis=0, keepdims=True)
        ex = jnp.exp(logits - mx)
        aff_ref[:, rows] = ex / jnp.sum(ex, axis=0, keepdims=True)


def _outproj(x2, cn, ao, ag, wc, wa, g2, wr):
    n = x2.shape[0]
    row = lambda i: (i, 0)
    const = lambda i: (0, 0)
    return pl.pallas_call(
        _outproj_body,
        grid=(n // TM_OUT,),
        in_specs=[
            pl.BlockSpec((TM_OUT, D_MODEL), row),
            pl.BlockSpec((TM_OUT, D_CONV), row),
            pl.BlockSpec((TM_OUT, D_Q), row),
            pl.BlockSpec((1, D_Q), const),
            pl.BlockSpec((D_CONV, D_MODEL), const),
            pl.BlockSpec((D_Q, D_MODEL), const),
            pl.BlockSpec((1, D_MODEL), const),
            pl.BlockSpec((D_MODEL, LANES), const),
        ],
        out_specs=[
            pl.BlockSpec((TM_OUT, D_MODEL), row),
            pl.BlockSpec((TM_OUT, D_MODEL), row),
            pl.BlockSpec((N_EXPERTS, TM_OUT), lambda i: (0, i)),
        ],
        out_shape=[
            jax.ShapeDtypeStruct((n, D_MODEL), F32),
            jax.ShapeDtypeStruct((n, D_MODEL), BF16),
            jax.ShapeDtypeStruct((N_EXPERTS, n), F32),
        ],
        compiler_params=_cparams(("parallel",)),
        name="outproj",
    )(x2, cn, ao, ag, wc, wa, g2, wr)


def _threshold_body(cap, aff_ref, thr_ref, need_ref):
    def step(it, lo):
        cand = lo | (jnp.int32(1) << (30 - it))
        bits = pltpu.bitcast(aff_ref[...], jnp.int32)
        cnt = jnp.sum((bits >= cand).astype(jnp.int32), axis=1, keepdims=True)
        return jnp.where(cnt >= cap, cand, lo)

    thr = lax.fori_loop(0, 31, step, jnp.zeros((N_EXPERTS, 1), jnp.int32))
    bits = pltpu.bitcast(aff_ref[...], jnp.int32)
    n_gt = jnp.sum((bits > thr).astype(jnp.int32), axis=1, keepdims=True)
    thr_ref[...] = jnp.broadcast_to(thr, thr_ref.shape)
    need_ref[...] = jnp.broadcast_to(cap - n_gt, need_ref.shape)


def _threshold(aff_t, cap):
    n = aff_t.shape[1]
    full = lambda: (0, 0)
    return pl.pallas_call(
        functools.partial(_threshold_body, cap),
        in_specs=[pl.BlockSpec((N_EXPERTS, n), full)],
        out_specs=[pl.BlockSpec((N_EXPERTS, LANES), full), pl.BlockSpec((N_EXPERTS, LANES), full)],
        out_shape=[jax.ShapeDtypeStruct((N_EXPERTS, LANES), jnp.int32),
                   jax.ShapeDtypeStruct((N_EXPERTS, LANES), jnp.int32)],
        compiler_params=pltpu.CompilerParams(vmem_limit_bytes=VMEM_LIMIT),
        name="threshold",
    )(aff_t)


def _mask_body(aff_ref, thr_ref, need_ref, tri_ref, gsel_ref, lpos_ref, cnt_ref, eqc_ref):
    @pl.when(pl.program_id(0) == 0)
    def _():
        eqc_ref[...] = jnp.zeros(eqc_ref.shape, F32)

    aff = aff_ref[...]
    bits = pltpu.bitcast(aff, jnp.int32)
    thr = thr_ref[:, 0:1]
    need = need_ref[:, 0:1].astype(F32)
    gt = bits > thr
    eq = bits == thr
    eq_f = jnp.where(eq, 1.0, 0.0)
    eq_rank = eqc_ref[:, 0:1] + jnp.dot(eq_f.astype(BF16), tri_ref[...], preferred_element_type=F32)
    sel = gt | (eq & (eq_rank < need))
    sel_f = jnp.where(sel, 1.0, 0.0)
    gsel_ref[...] = jnp.where(sel, aff, -1.0)
    lpos_ref[...] = jnp.dot(sel_f.astype(BF16), tri_ref[...], preferred_element_type=F32)
    cnt = jnp.sum(sel_f, axis=1, keepdims=True)
    cnt_ref[0] = jnp.broadcast_to(cnt, (N_EXPERTS, LANES)).astype(jnp.int32)
    eqc_ref[...] = eqc_ref[...] + jnp.sum(eq_f, axis=1, keepdims=True)


def _masks(aff_t, thr, need, tri):
    n = aff_t.shape[1]
    nt = n // TR
    const = lambda i: (0, 0)
    tile = lambda i: (0, i)
    return pl.pallas_call(
        _mask_body,
        grid=(nt,),
        in_specs=[
            pl.BlockSpec((N_EXPERTS, TR), tile),
            pl.BlockSpec((N_EXPERTS, LANES), const),
            pl.BlockSpec((N_EXPERTS, LANES), const),
            pl.BlockSpec((TR, TR), const),
        ],
        out_specs=[
            pl.BlockSpec((N_EXPERTS, TR), tile),
            pl.BlockSpec((N_EXPERTS, TR), tile),
            pl.BlockSpec((1, N_EXPERTS, LANES), lambda i: (i, 0, 0)),
        ],
        out_shape=[
            jax.ShapeDtypeStruct((N_EXPERTS, n), F32),
            jax.ShapeDtypeStruct((N_EXPERTS, n), F32),
            jax.ShapeDtypeStruct((nt, N_EXPERTS, LANES), jnp.int32),
        ],
        scratch_shapes=[pltpu.VMEM((N_EXPERTS, LANES), F32)],
        compiler_params=_cparams(("arbitrary",)),
        name="masks",
    )(aff_t, thr, need, tri)


def _onehot_rows(gsel_row, lpos_row, chunk, value):
    slot = lax.broadcasted_iota(jnp.int32, (CH, TR), 0).astype(F32) + (chunk * CH).astype(F32)
    return jnp.where((gsel_row >= 0.0) & (lpos_row == slot), value, 0.0).astype(BF16)


def _dispatch_body(off_ref, nch_ref, xn_ref, gsel_ref, lpos_ref, xe_in_ref, xe_ref, stage_ref, sem_ref, xsem_ref):
    del xe_in_ref
    n = pl.program_id(0)
    nt = pl.num_programs(0)

    def copy(e, slot, sem, chunk):
        row0 = pl.multiple_of(off_ref[e * nt + n] + chunk * CH, BF16_SUBLANES)
        return pltpu.make_async_copy(stage_ref.at[slot], xe_ref.at[e, pl.ds(row0, CH)], sem)

    def rows_of(e, chunk):
        hot = _onehot_rows(gsel_ref[pl.ds(e, 1), :], lpos_ref[pl.ds(e, 1), :], chunk, 1.0)
        return jnp.dot(hot, xn_ref[...], preferred_element_type=F32).astype(BF16)

    for e in range(N_EXPERTS):
        stage_ref[e] = rows_of(e, jnp.int32(0))
        copy(e, e, sem_ref.at[e], 0).start()
    for e in range(N_EXPERTS):
        copy(e, e, sem_ref.at[e], 0).wait()

    def per_expert(e, carry):
        def per_chunk(chunk, c):
            stage_ref[N_EXPERTS] = rows_of(e, chunk)
            cp = copy(e, N_EXPERTS, xsem_ref.at[0], chunk)
            cp.start()
            cp.wait()
            return c
        return lax.fori_loop(1, nch_ref[e * nt + n], per_chunk, carry)

    lax.fori_loop(0, N_EXPERTS, per_expert, 0)


def _dispatch(off, nch, xn2, gsel, lpos, xe_init):
    n = xn2.shape[0]
    nt = n // TR
    grid_spec = pltpu.PrefetchScalarGridSpec(
        num_scalar_prefetch=2,
        grid=(nt,),
        in_specs=[
            pl.BlockSpec((TR, D_MODEL), lambda i, *_: (i, 0)),
            pl.BlockSpec((N_EXPERTS, TR), lambda i, *_: (0, i)),
            pl.BlockSpec((N_EXPERTS, TR), lambda i, *_: (0, i)),
            pl.BlockSpec(memory_space=pl.ANY),
        ],
        out_specs=pl.BlockSpec(memory_space=pl.ANY),
        scratch_shapes=[pltpu.VMEM((N_EXPERTS + 1, CH, D_MODEL), BF16),
                        pltpu.SemaphoreType.DMA((N_EXPERTS,)),
                        pltpu.SemaphoreType.DMA((1,))],
    )
    return pl.pallas_call(
        _dispatch_body,
        grid_spec=grid_spec,
        out_shape=jax.ShapeDtypeStruct(xe_init.shape, BF16),
        input_output_aliases={5: 0},
        compiler_params=_cparams(("arbitrary",)),
        name="dispatch",
    )(off, nch, xn2, gsel, lpos, xe_init)


def _ffn_body(ntile_ref, x_ref, wg_ref, wu_ref, wd_ref, y_ref):
    e = pl.program_id(0)
    i = pl.program_id(1)

    @pl.when(i < ntile_ref[e])
    def _():
        x = x_ref[0]
        g = jnp.dot(x, wg_ref[0], preferred_element_type=F32)
        u = jnp.dot(x, wu_ref[0], preferred_element_type=F32)
        hid = (g * _sigmoid(g) * u).astype(BF16)
        y_ref[0] = jnp.dot(hid, wd_ref[0], preferred_element_type=F32).astype(BF16)

    @pl.when(i >= ntile_ref[e])
    def _():
        y_ref[...] = jnp.zeros(y_ref.shape, BF16)


def _ffn(ntile, xe, wg, wu, wd):
    cap_rows = xe.shape[1]
    rows = lambda e, i, nt_ref: (e, jnp.minimum(i, nt_ref[e] - 1), 0)
    wmap = lambda e, i, nt_ref: (e, 0, 0)
    grid_spec = pltpu.PrefetchScalarGridSpec(
        num_scalar_prefetch=1,
        grid=(N_EXPERTS, cap_rows // TMF),
        in_specs=[
            pl.BlockSpec((1, TMF, D_MODEL), rows),
            pl.BlockSpec((1, D_MODEL, D_FF_EXPERT), wmap),
            pl.BlockSpec((1, D_MODEL, D_FF_EXPERT), wmap),
            pl.BlockSpec((1, D_FF_EXPERT, D_MODEL), wmap),
        ],
        out_specs=pl.BlockSpec((1, TMF, D_MODEL), lambda e, i, nt_ref: (e, i, 0)),
    )
    return pl.pallas_call(
        _ffn_body,
        grid_spec=grid_spec,
        out_shape=jax.ShapeDtypeStruct(xe.shape, BF16),
        compiler_params=_cparams(("arbitrary", "arbitrary")),
        name="expert_ffn",
    )(ntile, xe, wg, wu, wd)


def _combine_body(off_ref, nch_ref, h_ref, gsel_ref, lpos_ref, p_ref, wpp_ref, pg_ref,
                  wpg_ref, bpg_ref, ye_ref, y_ref, ybuf_ref, xbuf_ref, hot_ref, acc_ref, sem_ref, xsem_ref):
    n = pl.program_id(0)
    nt = pl.num_programs(0)
    tn = (((0,), (0,)), ((), ()))
    cur = n % 2

    def first_chunk(e, step, buf):
        row0 = pl.multiple_of(off_ref[e * nt + step], BF16_SUBLANES)
        return pltpu.make_async_copy(ye_ref.at[e, pl.ds(row0, CH)], ybuf_ref.at[buf, pl.ds(e * CH, CH)],
                                     sem_ref.at[buf, e])

    def later_chunk(e, chunk):
        row0 = pl.multiple_of(off_ref[e * nt + n] + chunk * CH, BF16_SUBLANES)
        return pltpu.make_async_copy(ye_ref.at[e, pl.ds(row0, CH)], xbuf_ref, xsem_ref.at[0])

    def gated_hot(e, chunk):
        g = gsel_ref[pl.ds(e, 1), :]
        return _onehot_rows(g, lpos_ref[pl.ds(e, 1), :], chunk, g)

    @pl.when(n == 0)
    def _():
        for e in range(N_EXPERTS):
            first_chunk(e, 0, 0).start()

    @pl.when(n + 1 < nt)
    def _():
        for e in range(N_EXPERTS):
            first_chunk(e, n + 1, 1 - cur).start()

    for e in range(N_EXPERTS):
        hot_ref[e * CH:(e + 1) * CH, :] = gated_hot(e, jnp.int32(0))
    for e in range(N_EXPERTS):
        first_chunk(e, n, cur).wait()
    acc_ref[...] = h_ref[...] + lax.dot_general(hot_ref[...], ybuf_ref[cur], tn, preferred_element_type=F32)

    def per_expert(e, carry):
        def per_chunk(chunk, c):
            cp = later_chunk(e, chunk)
            cp.start()
            cp.wait()
            acc_ref[...] += lax.dot_general(gated_hot(e, chunk), xbuf_ref[...], tn, preferred_element_type=F32)
            return c
        return lax.fori_loop(1, nch_ref[e * nt + n], per_chunk, carry)

    lax.fori_loop(0, N_EXPERTS, per_expert, 0)

    h2 = acc_ref[...]
    emb = jnp.dot(p_ref[...].astype(BF16), wpp_ref[...], preferred_element_type=F32)
    hn = h2 * lax.rsqrt(jnp.mean(h2 * h2, axis=-1, keepdims=True) + EPS) * pg_ref[...]
    gate = _sigmoid(jnp.dot(hn.astype(BF16), wpg_ref[...], preferred_element_type=F32) + bpg_ref[...])
    y_ref[...] = h2 + gate * emb


def _combine(off, nch, h1, gsel, lpos, p2, wpp, pg, wpg, bpg, ye):
    n = h1.shape[0]
    nt = n // TR
    row = lambda i, *_: (i, 0)
    tile = lambda i, *_: (0, i)
    const = lambda i, *_: (0, 0)
    grid_spec = pltpu.PrefetchScalarGridSpec(
        num_scalar_prefetch=2,
        grid=(nt,),
        in_specs=[
            pl.BlockSpec((TR, D_MODEL), row),
            pl.BlockSpec((N_EXPERTS, TR), tile),
            pl.BlockSpec((N_EXPERTS, TR), tile),
            pl.BlockSpec((TR, D_PLE), row),
            pl.BlockSpec((D_PLE, D_MODEL), const),
            pl.BlockSpec((1, D_MODEL), const),
            pl.BlockSpec((D_MODEL, D_MODEL), const),
            pl.BlockSpec((1, D_MODEL), const),
            pl.BlockSpec(memory_space=pl.ANY),
        ],
        out_specs=pl.BlockSpec((TR, D_MODEL), row),
        scratch_shapes=[pltpu.VMEM((2, N_EXPERTS * CH, D_MODEL), BF16),
                        pltpu.VMEM((CH, D_MODEL), BF16),
                        pltpu.VMEM((N_EXPERTS * CH, TR), BF16),
                        pltpu.VMEM((TR, D_MODEL), F32),
                        pltpu.SemaphoreType.DMA((2, N_EXPERTS)),
                        pltpu.SemaphoreType.DMA((1,))],
    )
    return pl.pallas_call(
        _combine_body,
        grid_spec=grid_spec,
        out_shape=jax.ShapeDtypeStruct((n, D_MODEL), F32),
        compiler_params=_cparams(("arbitrary",)),
        name="combine",
    )(off, nch, h1, gsel, lpos, p2, wpp, pg, wpg, bpg, ye)


def _rope_tables(t):
    rows = t // GRID_W
    row_idx = jnp.repeat(jnp.arange(rows, dtype=F32), GRID_W)
    col_idx = jnp.tile(jnp.arange(GRID_W, dtype=F32), rows)
    freqs = 1.0 / (ROPE_THETA ** (jnp.arange(0, HALF_ROT, 2, dtype=F32) / HALF_ROT))
    ang_r = row_idx[:, None] * freqs[None, :]
    ang_c = col_idx[:, None] * freqs[None, :]
    cr, sr, cc, sc = jnp.cos(ang_r), jnp.sin(ang_r), jnp.cos(ang_c), jnp.sin(ang_c)
    cos_h = jnp.concatenate([cr, cr, cc, cc], axis=-1)
    sin_h = jnp.concatenate([-sr, sr, -sc, sc], axis=-1)
    return jnp.tile(cos_h, (1, LANES // HEAD_DIM)), jnp.tile(sin_h, (1, LANES // HEAD_DIM))


def _prepare_weights(norm1_g, w_in, conv_dw_w, conv_dw_b, conv_ln_g, conv_ln_b, q_norm_g, k_norm_g,
                     conv_out_g, attn_out_g, w_out, norm2_g, w_router, w_gate, w_up, w_down,
                     ple_proj, ple_norm_g, ple_gate_w, ple_gate_b):
    i = 0
    q_scale = (HEAD_DIM ** -0.5) * math.log2(math.e)
    qkg = jnp.concatenate([jnp.tile(q_norm_g[i] * q_scale, N_HEADS), jnp.tile(k_norm_g[i], N_KV_HEADS)])
    bound = HEAD_DIM * jnp.max(jnp.abs(q_norm_g[i] * q_scale)) * jnp.max(jnp.abs(k_norm_g[i]))
    use_shift = bound <= MAX_SCORE_SHIFT
    lane = jnp.arange(LANES)
    fill = jnp.stack([jnp.where(lane == HEAD_DIM, 1.0, 0.0),
                      jnp.where(lane == HEAD_DIM, -jnp.where(use_shift, bound, 0.0), 0.0)]).astype(F32)
    ones_bd = (lane[:, None] // HEAD_DIM == lane[None, :] // HEAD_DIM).astype(BF16)
    tok = jnp.arange(TR)
    wr_hi = w_router[i].astype(BF16)
    wr_lo = (w_router[i] - wr_hi.astype(F32)).astype(BF16)
    wr = jnp.concatenate([wr_hi, wr_lo, jnp.zeros((D_MODEL, LANES - 2 * N_EXPERTS), BF16)], axis=1)
    return dict(
        g1=norm1_g[i][None, :], w_in=w_in[i].astype(BF16), qkg=qkg[None, :], ones_bd=ones_bd,
        fill=fill, use_shift=use_shift,
        dw_w=conv_dw_w[i], dw_b=conv_dw_b[i][None, :], ln_g=conv_ln_g[i][None, :], ln_b=conv_ln_b[i][None, :],
        conv_out_g=conv_out_g[i][None, :], attn_out_g=attn_out_g[i][None, :],
        w_out_c=w_out[i, :D_CONV].astype(BF16), w_out_a=w_out[i, D_CONV:].astype(BF16),
        g2=norm2_g[i][None, :], w_router=wr,
        wg=w_gate[i].astype(BF16), wu=w_up[i].astype(BF16), wd=w_down[i].astype(BF16),
        tri=(tok[:, None] < tok[None, :]).astype(BF16),
        wpp=ple_proj[i].astype(BF16), pg=ple_norm_g[i][None, :], wpg=ple_gate_w[i].astype(BF16),
        bpg=ple_gate_b[i][None, :],
    )


def _trunk(x, p, w):
    b, t, _ = x.shape
    n = b * t
    assert t % TM_OUT == 0 and t % TK == 0 and t % TT == 0 and t % TQ == 0 and n % TR == 0
    cap = CAPACITY_FACTOR * n // N_EXPERTS
    nt = n // TR
    x2 = x.reshape(n, D_MODEL)
    cos_t, sin_t = _rope_tables(t)

    gl, q, k, v = _inproj(x2, w["g1"], w["w_in"], w["qkg"], cos_t, sin_t, w["ones_bd"], w["fill"], t)
    cn = _conv(gl.reshape(b, t, D_CONV), w["dw_w"], w["dw_b"], w["ln_g"], w["ln_b"], w["conv_out_g"])
    ao = lax.cond(w["use_shift"], _attention_shifted, _attention,
                  q.reshape(b, t, -1), k.reshape(b, t, -1), v.reshape(b, t, -1))
    h1, xn2, aff_t = _outproj(x2, cn.reshape(n, D_CONV), ao.reshape(n, D_Q), w["attn_out_g"],
                              w["w_out_c"], w["w_out_a"], w["g2"], w["w_router"])

    thr, need = _threshold(aff_t, cap)
    gsel, lpos, cnt = _masks(aff_t, thr, need, w["tri"])
    cnt = cnt[:, :, 0].T
    seg = (cnt + BF16_SUBLANES - 1) // BF16_SUBLANES * BF16_SUBLANES
    ends = jnp.cumsum(seg, axis=1)
    off = (ends - seg).astype(jnp.int32).reshape(-1)
    nch = ((cnt + CH - 1) // CH).astype(jnp.int32).reshape(-1)
    cap_rows = (cap + nt * (BF16_SUBLANES - 1) + CH + TMF - 1) // TMF * TMF
    ntile = jnp.maximum((ends[:, -1] + TMF - 1) // TMF, 1).astype(jnp.int32)

    xe = _dispatch(off, nch, xn2, gsel, lpos, jnp.zeros((N_EXPERTS, cap_rows, D_MODEL), BF16))
    ye = _ffn(ntile, xe, w["wg"], w["wu"], w["wd"])
    y = _combine(off, nch, h1, gsel, lpos, p.reshape(n, D_PLE), w["wpp"], w["pg"], w["wpg"],
                 w["bpg"], ye)
    return y.reshape(b, t, D_MODEL)


def kernel(x_prompt, x_sample, p_prompt, p_sample, norm1_g, w_in, conv_dw_w, conv_dw_b, conv_ln_g, conv_ln_b, q_norm_g, k_norm_g, conv_out_g, attn_out_g, w_out, norm2_g, w_router, w_gate, w_up, w_down, ple_proj, ple_norm_g, ple_gate_w, ple_gate_b):
    w = _prepare_weights(norm1_g, w_in, conv_dw_w, conv_dw_b, conv_ln_g, conv_ln_b, q_norm_g, k_norm_g,
                         conv_out_g, attn_out_g, w_out, norm2_g, w_router, w_gate, w_up, w_down,
                         ple_proj, ple_norm_g, ple_gate_w, ple_gate_b)
    y_prompt = _trunk(x_prompt, p_prompt[0], w)
    y_sample = _trunk(x_sample, p_sample[0], w)
    return (y_prompt, y_sample)
```

```python
import functools
import math

import jax
import jax.numpy as jnp
from jax import lax
from jax.experimental import pallas as pl
from jax.experimental.pallas import tpu as pltpu

D_MODEL = 1024
D_CONV = 512
CONV_WIDTH = 31
CONV_PAD = CONV_WIDTH // 2
N_HEADS = 8
N_KV_HEADS = 2
HEAD_DIM = 64
KV_GROUP = N_HEADS // N_KV_HEADS
D_Q = N_HEADS * HEAD_DIM
D_KV = N_KV_HEADS * HEAD_DIM
D_IN = 2 * D_CONV + D_Q + 2 * D_KV
HALF_ROT = HEAD_DIM // 2
ROPE_THETA = 10000.0
GRID_W = 64
N_EXPERTS = 16
CAPACITY_FACTOR = 2
D_FF_EXPERT = 2048
D_PLE = 256
EPS = 1e-6

LANES = 128
F32_SUBLANES = 8
BF16_SUBLANES = 16
VMEM_LIMIT = 56 * 1024 * 1024

TM = 512
TM_OUT = 1024
TT = 256
HALO = 16
TQ = 256
TK = 512
KV_CHUNKS = 8
MAX_SCORE_SHIFT = 40.0
TR = 512
CH = 128
TMF = 512

F32 = jnp.float32
BF16 = jnp.bfloat16


def _cparams(sem):
    return pltpu.CompilerParams(dimension_semantics=sem, vmem_limit_bytes=VMEM_LIMIT)


def _sigmoid(x):
    return 1.0 / (1.0 + jnp.exp(-x))


def _inproj_body(x_ref, g1_ref, w_ref, qkg_ref, cos_ref, sin_ref, ones_ref, fill_ref,
                 gl_ref, q_ref, k_ref, v_ref):
    for blk in range(TM_OUT // TM):
        _inproj_rows(slice(blk * TM, (blk + 1) * TM), x_ref, g1_ref, w_ref, qkg_ref, cos_ref, sin_ref,
                     ones_ref, fill_ref, gl_ref, q_ref, k_ref, v_ref)


def _inproj_rows(rows, x_ref, g1_ref, w_ref, qkg_ref, cos_ref, sin_ref, ones_ref, fill_ref,
                 gl_ref, q_ref, k_ref, v_ref):
    x = x_ref[rows, :]
    a = x * lax.rsqrt(jnp.mean(x * x, axis=-1, keepdims=True) + EPS) * g1_ref[...]
    z = jnp.dot(a.astype(BF16), w_ref[...], preferred_element_type=F32)
    val = z[:, :D_CONV]
    gate = z[:, D_CONV:2 * D_CONV]
    gl_ref[rows, :] = (val * _sigmoid(gate)).astype(BF16)

    lane = lax.broadcasted_iota(jnp.int32, (x.shape[0], LANES), 1)
    first_half = (lane % HALF_ROT) < (HALF_ROT // 2)
    low_head = lane < HEAD_DIM
    cos = cos_ref[rows, :]
    sin = sin_ref[rows, :]
    o0 = 2 * D_CONV
    n_chunks = (D_Q + D_KV) // LANES
    for c in range(n_chunks):
        qc = z[:, o0 + c * LANES:o0 + (c + 1) * LANES]
        ssum = jnp.dot((qc * qc).astype(BF16), ones_ref[...], preferred_element_type=F32)
        qn = qc * lax.rsqrt(ssum * (1.0 / HEAD_DIM) + EPS) * qkg_ref[:, c * LANES:(c + 1) * LANES]
        partner = jnp.where(first_half, pltpu.roll(qn, LANES - HALF_ROT // 2, 1),
                            pltpu.roll(qn, HALF_ROT // 2, 1))
        qr = qn * cos + partner * sin
        fill = fill_ref[0:1, :] if c < D_Q // LANES else fill_ref[1:2, :]
        even = jnp.where(low_head, qr, fill).astype(BF16)
        odd = jnp.where(low_head, pltpu.roll(qr, HEAD_DIM, 1), fill).astype(BF16)
        if c < D_Q // LANES:
            q_ref[rows, (2 * c) * LANES:(2 * c + 1) * LANES] = even
            q_ref[rows, (2 * c + 1) * LANES:(2 * c + 2) * LANES] = odd
        else:
            k_ref[rows, 0:LANES] = even
            k_ref[rows, LANES:2 * LANES] = odd
    vv = z[:, o0 + D_Q + D_KV:]
    v_ref[0:LANES, rows] = jnp.where(low_head, vv, 1.0).T.astype(BF16)
    v_ref[LANES:2 * LANES, rows] = jnp.where(low_head, pltpu.roll(vv, HEAD_DIM, 1), 1.0).T.astype(BF16)


def _inproj(x2, g1, w_in_bf, qkg, cos_t, sin_t, ones_bd, fill, seq_len):
    n = x2.shape[0]
    pos_tiles = seq_len // TM_OUT
    row = lambda i: (i, 0)
    const = lambda i: (0, 0)
    return pl.pallas_call(
        _inproj_body,
        grid=(n // TM_OUT,),
        in_specs=[
            pl.BlockSpec((TM_OUT, D_MODEL), row),
            pl.BlockSpec((1, D_MODEL), const),
            pl.BlockSpec((D_MODEL, D_IN), const),
            pl.BlockSpec((1, D_Q + D_KV), const),
            pl.BlockSpec((TM_OUT, LANES), lambda i: (i % pos_tiles, 0)),
            pl.BlockSpec((TM_OUT, LANES), lambda i: (i % pos_tiles, 0)),
            pl.BlockSpec((LANES, LANES), const),
            pl.BlockSpec((2, LANES), const),
        ],
        out_specs=[
            pl.BlockSpec((TM_OUT, D_CONV), row),
            pl.BlockSpec((TM_OUT, N_HEADS * LANES), row),
            pl.BlockSpec((TM_OUT, N_KV_HEADS * LANES), row),
            pl.BlockSpec((N_KV_HEADS * LANES, TM_OUT), lambda i: (0, i)),
        ],
        out_shape=[
            jax.ShapeDtypeStruct((n, D_CONV), BF16),
            jax.ShapeDtypeStruct((n, N_HEADS * LANES), BF16),
            jax.ShapeDtypeStruct((n, N_KV_HEADS * LANES), BF16),
            jax.ShapeDtypeStruct((N_KV_HEADS * LANES, n), BF16),
        ],
        compiler_params=_cparams(("parallel",)),
        name="inproj",
    )(x2, g1, w_in_bf, qkg, cos_t, sin_t, ones_bd, fill)


CONV_ROWS = 64
CONV_SPAN = TT + 2 * HALO - F32_SUBLANES


def _conv_body(left_ref, main_ref, right_ref, w_ref, b_ref, lng_ref, lnb_ref, og_ref,
               out_ref, win_ref, conv_ref):
    i = pl.program_id(1)
    last = pl.num_programs(1) - 1
    left = left_ref[0].astype(F32)
    right = right_ref[0].astype(F32)
    win_ref[0, 0:HALO, :] = jnp.where(i > 0, left, 0.0)
    win_ref[0, HALO:HALO + TT, :] = main_ref[0].astype(F32)
    win_ref[0, HALO + TT:HALO + TT + HALO, :] = jnp.where(i < last, right, 0.0)
    for r in range(1, F32_SUBLANES):
        win_ref[r, 0:CONV_SPAN, :] = win_ref[0, r:r + CONV_SPAN, :]
    base = HALO - CONV_PAD

    def row_block(rr, carry):
        r0 = pl.multiple_of(rr * CONV_ROWS, CONV_ROWS)
        groups = CONV_ROWS // F32_SUBLANES
        reach = (base + CONV_WIDTH - 1) // F32_SUBLANES + 1
        for c in range(D_CONV // LANES):
            ls = slice(c * LANES, (c + 1) * LANES)
            accs = [None] * groups
            for shift in range(F32_SUBLANES):
                for q in range(groups + reach - 1):
                    uses = [(q - a, a * F32_SUBLANES + shift - base) for a in range(reach)
                            if 0 <= a * F32_SUBLANES + shift - base < CONV_WIDTH and 0 <= q - a < groups]
                    if not uses:
                        continue
                    tile = win_ref[shift, pl.ds(r0 + q * F32_SUBLANES, F32_SUBLANES), ls]
                    for g, k in uses:
                        term = tile * w_ref[k:k + 1, ls]
                        accs[g] = term if accs[g] is None else accs[g] + term
            for g in range(groups):
                conv_ref[pl.ds(r0 + g * F32_SUBLANES, F32_SUBLANES), ls] = accs[g]
        return carry

    lax.fori_loop(0, TT // CONV_ROWS, row_block, 0)
    cv = conv_ref[...] + b_ref[...]
    mu = jnp.mean(cv, axis=-1, keepdims=True)
    d = cv - mu
    var = jnp.mean(d * d, axis=-1, keepdims=True)
    y = d * lax.rsqrt(var + EPS) * lng_ref[...] + lnb_ref[...]
    y = y * _sigmoid(y)
    y = y * lax.rsqrt(jnp.mean(y * y, axis=-1, keepdims=True) + EPS) * og_ref[...]
    out_ref[0] = y.astype(BF16)


def _conv(gl3, dw_w, dw_b, ln_g, ln_b, out_g):
    b, t, _ = gl3.shape
    hb = TT // HALO
    n_halo = t // HALO
    const = lambda bb, i: (0, 0)
    return pl.pallas_call(
        _conv_body,
        grid=(b, t // TT),
        in_specs=[
            pl.BlockSpec((1, HALO, D_CONV), lambda bb, i: (bb, jnp.maximum(i * hb - 1, 0), 0)),
            pl.BlockSpec((1, TT, D_CONV), lambda bb, i: (bb, i, 0)),
            pl.BlockSpec((1, HALO, D_CONV), lambda bb, i: (bb, jnp.minimum((i + 1) * hb, n_halo - 1), 0)),
            pl.BlockSpec((CONV_WIDTH, D_CONV), const),
            pl.BlockSpec((1, D_CONV), const),
            pl.BlockSpec((1, D_CONV), const),
            pl.BlockSpec((1, D_CONV), const),
            pl.BlockSpec((1, D_CONV), const),
        ],
        out_specs=pl.BlockSpec((1, TT, D_CONV), lambda bb, i: (bb, i, 0)),
        out_shape=jax.ShapeDtypeStruct((b, t, D_CONV), BF16),
        scratch_shapes=[pltpu.VMEM((F32_SUBLANES, TT + 2 * HALO, D_CONV), F32),
                        pltpu.VMEM((TT, D_CONV), F32)],
        compiler_params=_cparams(("parallel", "parallel")),
        name="conv",
    )(gl3, gl3, gl3, dw_w, dw_b, ln_g, ln_b, out_g)


def _attn_body(q_ref, k_ref, vt_ref, o_ref, qs_ref, m_ref, acc_ref):
    t = k_ref.shape[1]
    nt_dims = (((1,), (1,)), ((), ()))
    for h in range(KV_GROUP):
        qs_ref[h * TQ:(h + 1) * TQ, :] = q_ref[0, :, h * LANES:(h + 1) * LANES]
    m_ref[...] = jnp.full(m_ref.shape, -jnp.inf, F32)
    acc_ref[...] = jnp.zeros(acc_ref.shape, F32)

    def step(kt, carry):
        start = pl.multiple_of(kt * TK, TK)
        s = lax.dot_general(qs_ref[...], k_ref[0, pl.ds(start, TK), :], nt_dims,
                            preferred_element_type=F32)
        m_old = m_ref[...]
        m_new = jnp.maximum(m_old, jnp.max(s, axis=-1, keepdims=True))
        p = jnp.exp2(s - m_new[:, 0:1])
        alpha = jnp.exp2(m_old - m_new)
        acc_ref[...] = alpha * acc_ref[...] + lax.dot_general(
            p.astype(BF16), vt_ref[:, pl.ds(start, TK)], nt_dims, preferred_element_type=F32)
        m_ref[...] = m_new
        return carry

    lax.fori_loop(0, t // TK, step, 0)
    acc = acc_ref[...]
    res = acc / pltpu.roll(acc, HEAD_DIM, 1)
    _attn_store(o_ref, [res[h * TQ:(h + 1) * TQ, :] for h in range(KV_GROUP)])


def _attn_shifted_body(q_ref, k_ref, vt_ref, o_ref, qs_ref, acc_ref):
    t = k_ref.shape[1]
    for h in range(KV_GROUP):
        qs_ref[h * TQ:(h + 1) * TQ, :] = q_ref[0, :, h * LANES:(h + 1) * LANES]
    acc_ref[...] = jnp.zeros(acc_ref.shape, F32)

    chunks = math.gcd(KV_CHUNKS, t // TK)

    def step(kt, carry):
        qs = qs_ref[...]
        part = None
        for c in range(chunks):
            start = pl.multiple_of(kt * (chunks * TK) + c * TK, TK)
            st = lax.dot_general(k_ref[0, pl.ds(start, TK), :], qs, (((1,), (1,)), ((), ())),
                                 preferred_element_type=F32)
            pv = jnp.dot(vt_ref[:, pl.ds(start, TK)], jnp.exp2(st).astype(BF16),
                         preferred_element_type=F32)
            part = pv if part is None else part + pv
        acc_ref[...] += part
        return carry

    lax.fori_loop(0, t // (chunks * TK), step, 0)
    acc = acc_ref[...]
    res = acc * (1.0 / acc[HEAD_DIM:HEAD_DIM + 1, :])
    _attn_store(o_ref, [res[:, h * TQ:(h + 1) * TQ].T for h in range(KV_GROUP)])


def _attn_store(o_ref, heads):
    lane = lax.broadcasted_iota(jnp.int32, (TQ, LANES), 1)
    low = lane < HEAD_DIM
    for hp in range(KV_GROUP // 2):
        o_ref[0, :, hp * LANES:(hp + 1) * LANES] = jnp.where(
            low, heads[2 * hp], pltpu.roll(heads[2 * hp + 1], HEAD_DIM, 1)).astype(BF16)


def _attention_call(body, name, scratch, q3, k3, vt):
    b, t, _ = q3.shape
    gw = KV_GROUP * LANES
    return pl.pallas_call(
        body,
        grid=(b, N_KV_HEADS, t // TQ),
        in_specs=[
            pl.BlockSpec((1, TQ, gw), lambda bb, j, i: (bb, i, j)),
            pl.BlockSpec((1, t, LANES), lambda bb, j, i: (bb, 0, j)),
            pl.BlockSpec((LANES, t), lambda bb, j, i: (j, bb)),
        ],
        out_specs=pl.BlockSpec((1, TQ, KV_GROUP * HEAD_DIM), lambda bb, j, i: (bb, i, j)),
        out_shape=jax.ShapeDtypeStruct((b, t, D_Q), BF16),
        scratch_shapes=[pltpu.VMEM((KV_GROUP * TQ, LANES), BF16)] + scratch,
        compiler_params=_cparams(("parallel", "parallel", "parallel")),
        name=name,
    )(q3, k3, vt)


def _attention_shifted(q3, k3, vt):
    return _attention_call(_attn_shifted_body, "attention_shifted",
                           [pltpu.VMEM((LANES, KV_GROUP * TQ), F32)], q3, k3, vt)


def _attention(q3, k3, vt):
    return _attention_call(_attn_body, "attention",
                           [pltpu.VMEM((KV_GROUP * TQ, LANES), F32),
                            pltpu.VMEM((KV_GROUP * TQ, LANES), F32)], q3, k3, vt)


def _outproj_body(x_ref, cn_ref, ao_ref, ag_ref, wc_ref, wa_ref, g2_ref, wr_ref,
                  h_ref, xn_ref, aff_ref):
    for blk in range(TM_OUT // TM):
        rows = slice(blk * TM, (blk + 1) * TM)
        ao = ao_ref[rows, :].astype(F32)
        an = ao * lax.rsqrt(jnp.mean(ao * ao, axis=-1, keepdims=True) + EPS) * ag_ref[...]
        h = (x_ref[rows, :]
             + jnp.dot(cn_ref[rows, :], wc_ref[...], preferred_element_type=F32)
             + jnp.dot(an.astype(BF16), wa_ref[...], preferred_element_type=F32))
        h_ref[rows, :] = h
        xn = h * lax.rsqrt(jnp.mean(h * h, axis=-1, keepdims=True) + EPS) * g2_ref[...]
        xn_hi = xn.astype(BF16)
        xn_ref[rows, :] = xn_hi
        xn_lo = (xn - xn_hi.astype(F32)).astype(BF16)
        parts = (jnp.dot(xn_hi, wr_ref[...], preferred_element_type=F32)
                 + jnp.dot(xn_lo, wr_ref[...], preferred_element_type=F32))
        parts_t = parts.T
        logits = parts_t[0:N_EXPERTS, :] + parts_t[N_EXPERTS:2 * N_EXPERTS, :]
        mx = jnp.max(logits, axis=0, keepdims=True)
        ex = jnp.exp(logits - mx)
        aff_ref[:, rows] = ex / jnp.sum(ex, axis=0, keepdims=True)


def _outproj(x2, cn, ao, ag, wc, wa, g2, wr):
    n = x2.shape[0]
    row = lambda i: (i, 0)
    const = lambda i: (0, 0)
    return pl.pallas_call(
        _outproj_body,
        grid=(n // TM_OUT,),
        in_specs=[
            pl.BlockSpec((TM_OUT, D_MODEL), row),
            pl.BlockSpec((TM_OUT, D_CONV), row),
            pl.BlockSpec((TM_OUT, D_Q), row),
            pl.BlockSpec((1, D_Q), const),
            pl.BlockSpec((D_CONV, D_MODEL), const),
            pl.BlockSpec((D_Q, D_MODEL), const),
            pl.BlockSpec((1, D_MODEL), const),
            pl.BlockSpec((D_MODEL, LANES), const),
        ],
        out_specs=[
            pl.BlockSpec((TM_OUT, D_MODEL), row),
            pl.BlockSpec((TM_OUT, D_MODEL), row),
            pl.BlockSpec((N_EXPERTS, TM_OUT), lambda i: (0, i)),
        ],
        out_shape=[
            jax.ShapeDtypeStruct((n, D_MODEL), F32),
            jax.ShapeDtypeStruct((n, D_MODEL), BF16),
            jax.ShapeDtypeStruct((N_EXPERTS, n), F32),
        ],
        compiler_params=_cparams(("parallel",)),
        name="outproj",
    )(x2, cn, ao, ag, wc, wa, g2, wr)


def _threshold_body(cap, aff_ref, thr_ref, need_ref):
    def step(it, lo):
        cand = lo | (jnp.int32(1) << (30 - it))
        bits = pltpu.bitcast(aff_ref[...], jnp.int32)
        cnt = jnp.sum((bits >= cand).astype(jnp.int32), axis=1, keepdims=True)
        return jnp.where(cnt >= cap, cand, lo)

    thr = lax.fori_loop(0, 31, step, jnp.zeros((N_EXPERTS, 1), jnp.int32))
    bits = pltpu.bitcast(aff_ref[...], jnp.int32)
    n_gt = jnp.sum((bits > thr).astype(jnp.int32), axis=1, keepdims=True)
    thr_ref[...] = jnp.broadcast_to(thr, thr_ref.shape)
    need_ref[...] = jnp.broadcast_to(cap - n_gt, need_ref.shape)


def _threshold(aff_t, cap):
    n = aff_t.shape[1]
    full = lambda: (0, 0)
    return pl.pallas_call(
        functools.partial(_threshold_body, cap),
        in_specs=[pl.BlockSpec((N_EXPERTS, n), full)],
        out_specs=[pl.BlockSpec((N_EXPERTS, LANES), full), pl.BlockSpec((N_EXPERTS, LANES), full)],
        out_shape=[jax.ShapeDtypeStruct((N_EXPERTS, LANES), jnp.int32),
                   jax.ShapeDtypeStruct((N_EXPERTS, LANES), jnp.int32)],
        compiler_params=pltpu.CompilerParams(vmem_limit_bytes=VMEM_LIMIT),
        name="threshold",
    )(aff_t)


def _mask_body(aff_ref, thr_ref, need_ref, tri_ref, gsel_ref, lpos_ref, cnt_ref, eqc_ref):
    @pl.when(pl.program_id(0) == 0)
    def _():
        eqc_ref[...] = jnp.zeros(eqc_ref.shape, F32)

    aff = aff_ref[...]
    bits = pltpu.bitcast(aff, jnp.int32)
    thr = thr_ref[:, 0:1]
    need = need_ref[:, 0:1].astype(F32)
    gt = bits > thr
    eq = bits == thr
    eq_f = jnp.where(eq, 1.0, 0.0)
    eq_rank = eqc_ref[:, 0:1] + jnp.dot(eq_f.astype(BF16), tri_ref[...], preferred_element_type=F32)
    sel = gt | (eq & (eq_rank < need))
    sel_f = jnp.where(sel, 1.0, 0.0)
    gsel_ref[...] = jnp.where(sel, aff, -1.0)
    lpos_ref[...] = jnp.dot(sel_f.astype(BF16), tri_ref[...], preferred_element_type=F32)
    cnt = jnp.sum(sel_f, axis=1, keepdims=True)
    cnt_ref[0] = jnp.broadcast_to(cnt, (N_EXPERTS, LANES)).astype(jnp.int32)
    eqc_ref[...] = eqc_ref[...] + jnp.sum(eq_f, axis=1, keepdims=True)


def _masks(aff_t, thr, need, tri):
    n = aff_t.shape[1]
    nt = n // TR
    const = lambda i: (0, 0)
    tile = lambda i: (0, i)
    return pl.pallas_call(
        _mask_body,
        grid=(nt,),
        in_specs=[
            pl.BlockSpec((N_EXPERTS, TR), tile),
            pl.BlockSpec((N_EXPERTS, LANES), const),
            pl.BlockSpec((N_EXPERTS, LANES), const),
            pl.BlockSpec((TR, TR), const),
        ],
        out_specs=[
            pl.BlockSpec((N_EXPERTS, TR), tile),
            pl.BlockSpec((N_EXPERTS, TR), tile),
            pl.BlockSpec((1, N_EXPERTS, LANES), lambda i: (i, 0, 0)),
        ],
        out_shape=[
            jax.ShapeDtypeStruct((N_EXPERTS, n), F32),
            jax.ShapeDtypeStruct((N_EXPERTS, n), F32),
            jax.ShapeDtypeStruct((nt, N_EXPERTS, LANES), jnp.int32),
        ],
        scratch_shapes=[pltpu.VMEM((N_EXPERTS, LANES), F32)],
        compiler_params=_cparams(("arbitrary",)),
        name="masks",
    )(aff_t, thr, need, tri)


def _onehot_rows(gsel_row, lpos_row, chunk, value):
    slot = lax.broadcasted_iota(jnp.int32, (CH, TR), 0).astype(F32) + (chunk * CH).astype(F32)
    return jnp.where((gsel_row >= 0.0) & (lpos_row == slot), value, 0.0).astype(BF16)


def _dispatch_body(cap, off_ref, nch_ref, xn_ref, gsel_ref, lpos_ref, xe_ref, stage_ref, sem_ref, xsem_ref):
    n = pl.program_id(0)
    nt = pl.num_programs(0)

    def copy(e, slot, sem, chunk):
        row0 = pl.multiple_of(off_ref[e * nt + n] + chunk * CH, BF16_SUBLANES)
        return pltpu.make_async_copy(stage_ref.at[slot], xe_ref.at[e, pl.ds(row0, CH)], sem)

    @pl.when(n == 0)
    def _():
        stage_ref[N_EXPERTS] = jnp.zeros((CH, D_MODEL), BF16)
        first = cap // CH * CH
        fills = [pltpu.make_async_copy(stage_ref.at[N_EXPERTS], xe_ref.at[e, pl.ds(row0, CH)], xsem_ref.at[0])
                 for e in range(N_EXPERTS) for row0 in range(first, xe_ref.shape[1], CH)]
        for cp in fills:
            cp.start()
        for cp in fills:
            cp.wait()

    def rows_of(e, chunk):
        hot = _onehot_rows(gsel_ref[pl.ds(e, 1), :], lpos_ref[pl.ds(e, 1), :], chunk, 1.0)
        return jnp.dot(hot, xn_ref[...], preferred_element_type=F32).astype(BF16)

    for e in range(N_EXPERTS):
        stage_ref[e] = rows_of(e, jnp.int32(0))
        copy(e, e, sem_ref.at[e], 0).start()
    for e in range(N_EXPERTS):
        copy(e, e, sem_ref.at[e], 0).wait()

    def per_expert(e, carry):
        def per_chunk(chunk, c):
            stage_ref[N_EXPERTS] = rows_of(e, chunk)
            cp = copy(e, N_EXPERTS, xsem_ref.at[0], chunk)
            cp.start()
            cp.wait()
            return c
        return lax.fori_loop(1, nch_ref[e * nt + n], per_chunk, carry)

    lax.fori_loop(0, N_EXPERTS, per_expert, 0)


def _dispatch(off, nch, xn2, gsel, lpos, cap, cap_rows):
    n = xn2.shape[0]
    nt = n // TR
    grid_spec = pltpu.PrefetchScalarGridSpec(
        num_scalar_prefetch=2,
        grid=(nt,),
        in_specs=[
            pl.BlockSpec((TR, D_MODEL), lambda i, *_: (i, 0)),
            pl.BlockSpec((N_EXPERTS, TR), lambda i, *_: (0, i)),
            pl.BlockSpec((N_EXPERTS, TR), lambda i, *_: (0, i)),
        ],
        out_specs=pl.BlockSpec(memory_space=pl.ANY),
        scratch_shapes=[pltpu.VMEM((N_EXPERTS + 1, CH, D_MODEL), BF16),
                        pltpu.SemaphoreType.DMA((N_EXPERTS,)),
                        pltpu.SemaphoreType.DMA((1,))],
    )
    return pl.pallas_call(
        functools.partial(_dispatch_body, cap),
        grid_spec=grid_spec,
        out_shape=jax.ShapeDtypeStruct((N_EXPERTS, cap_rows, D_MODEL), BF16),
        compiler_params=_cparams(("arbitrary",)),
        name="dispatch",
    )(off, nch, xn2, gsel, lpos)


def _ffn_body(ntile_ref, x_ref, wg_ref, wu_ref, wd_ref, y_ref):
    e = pl.program_id(0)
    i = pl.program_id(1)

    @pl.when(i < ntile_ref[e])
    def _():
        x = x_ref[0]
        g = jnp.dot(x, wg_ref[0], preferred_element_type=F32)
        u = jnp.dot(x, wu_ref[0], preferred_element_type=F32)
        hid = (g * _sigmoid(g) * u).astype(BF16)
        y_ref[0] = jnp.dot(hid, wd_ref[0], preferred_element_type=F32).astype(BF16)

    @pl.when(i >= ntile_ref[e])
    def _():
        y_ref[...] = jnp.zeros(y_ref.shape, BF16)


def _ffn(ntile, xe, wg, wu, wd):
    cap_rows = xe.shape[1]
    rows = lambda e, i, nt_ref: (e, jnp.minimum(i, nt_ref[e] - 1), 0)
    wmap = lambda e, i, nt_ref: (e, 0, 0)
    grid_spec = pltpu.PrefetchScalarGridSpec(
        num_scalar_prefetch=1,
        grid=(N_EXPERTS, cap_rows // TMF),
        in_specs=[
            pl.BlockSpec((1, TMF, D_MODEL), rows),
            pl.BlockSpec((1, D_MODEL, D_FF_EXPERT), wmap),
            pl.BlockSpec((1, D_MODEL, D_FF_EXPERT), wmap),
            pl.BlockSpec((1, D_FF_EXPERT, D_MODEL), wmap),
        ],
        out_specs=pl.BlockSpec((1, TMF, D_MODEL), lambda e, i, nt_ref: (e, i, 0)),
    )
    return pl.pallas_call(
        _ffn_body,
        grid_spec=grid_spec,
        out_shape=jax.ShapeDtypeStruct(xe.shape, BF16),
        compiler_params=_cparams(("arbitrary", "arbitrary")),
        name="expert_ffn",
    )(ntile, xe, wg, wu, wd)


def _combine_body(off_ref, nch_ref, h_ref, gsel_ref, lpos_ref, p_ref, wpp_ref, pg_ref,
                  wpg_ref, bpg_ref, ye_ref, y_ref, ybuf_ref, xbuf_ref, hot_ref, acc_ref, sem_ref, xsem_ref):
    n = pl.program_id(0)
    nt = pl.num_programs(0)
    tn = (((0,), (0,)), ((), ()))
    cur = n % 2

    def first_chunk(e, step, buf):
        row0 = pl.multiple_of(off_ref[e * nt + step], BF16_SUBLANES)
        return pltpu.make_async_copy(ye_ref.at[e, pl.ds(row0, CH)], ybuf_ref.at[buf, pl.ds(e * CH, CH)],
                                     sem_ref.at[buf, e])

    def later_chunk(e, chunk):
        row0 = pl.multiple_of(off_ref[e * nt + n] + chunk * CH, BF16_SUBLANES)
        return pltpu.make_async_copy(ye_ref.at[e, pl.ds(row0, CH)], xbuf_ref, xsem_ref.at[0])

    def gated_hot(e, chunk):
        g = gsel_ref[pl.ds(e, 1), :]
        return _onehot_rows(g, lpos_ref[pl.ds(e, 1), :], chunk, g)

    @pl.when(n == 0)
    def _():
        for e in range(N_EXPERTS):
            first_chunk(e, 0, 0).start()

    @pl.when(n + 1 < nt)
    def _():
        for e in range(N_EXPERTS):
            first_chunk(e, n + 1, 1 - cur).start()

    for e in range(N_EXPERTS):
        hot_ref[e * CH:(e + 1) * CH, :] = gated_hot(e, jnp.int32(0))
    for e in range(N_EXPERTS):
        first_chunk(e, n, cur).wait()
    acc_ref[...] = h_ref[...] + lax.dot_general(hot_ref[...], ybuf_ref[cur], tn, preferred_element_type=F32)

    def per_expert(e, carry):
        def per_chunk(chunk, c):
            cp = later_chunk(e, chunk)
            cp.start()
            cp.wait()
            acc_ref[...] += lax.dot_general(gated_hot(e, chunk), xbuf_ref[...], tn, preferred_element_type=F32)
            return c
        return lax.fori_loop(1, nch_ref[e * nt + n], per_chunk, carry)

    lax.fori_loop(0, N_EXPERTS, per_expert, 0)

    h2 = acc_ref[...]
    emb = jnp.dot(p_ref[...].astype(BF16), wpp_ref[...], preferred_element_type=F32)
    hn = h2 * lax.rsqrt(jnp.mean(h2 * h2, axis=-1, keepdims=True) + EPS) * pg_ref[...]
    gate = _sigmoid(jnp.dot(hn.astype(BF16), wpg_ref[...], preferred_element_type=F32) + bpg_ref[...])
    y_ref[...] = h2 + gate * emb


def _combine(off, nch, h1, gsel, lpos, p2, wpp, pg, wpg, bpg, ye):
    n = h1.shape[0]
    nt = n // TR
    row = lambda i, *_: (i, 0)
    tile = lambda i, *_: (0, i)
    const = lambda i, *_: (0, 0)
    grid_spec = pltpu.PrefetchScalarGridSpec(
        num_scalar_prefetch=2,
        grid=(nt,),
        in_specs=[
            pl.BlockSpec((TR, D_MODEL), row),
            pl.BlockSpec((N_EXPERTS, TR), tile),
            pl.BlockSpec((N_EXPERTS, TR), tile),
            pl.BlockSpec((TR, D_PLE), row),
            pl.BlockSpec((D_PLE, D_MODEL), const),
            pl.BlockSpec((1, D_MODEL), const),
            pl.BlockSpec((D_MODEL, D_MODEL), const),
            pl.BlockSpec((1, D_MODEL), const),
            pl.BlockSpec(memory_space=pl.ANY),
        ],
        out_specs=pl.BlockSpec((TR, D_MODEL), row),
        scratch_shapes=[pltpu.VMEM((2, N_EXPERTS * CH, D_MODEL), BF16),
                        pltpu.VMEM((CH, D_MODEL), BF16),
                        pltpu.VMEM((N_EXPERTS * CH, TR), BF16),
                        pltpu.VMEM((TR, D_MODEL), F32),
                        pltpu.SemaphoreType.DMA((2, N_EXPERTS)),
                        pltpu.SemaphoreType.DMA((1,))],
    )
    return pl.pallas_call(
        _combine_body,
        grid_spec=grid_spec,
        out_shape=jax.ShapeDtypeStruct((n, D_MODEL), F32),
        compiler_params=_cparams(("arbitrary",)),
        name="combine",
    )(off, nch, h1, gsel, lpos, p2, wpp, pg, wpg, bpg, ye)


def _rope_tables(t):
    rows = t // GRID_W
    row_idx = jnp.repeat(jnp.arange(rows, dtype=F32), GRID_W)
    col_idx = jnp.tile(jnp.arange(GRID_W, dtype=F32), rows)
    freqs = 1.0 / (ROPE_THETA ** (jnp.arange(0, HALF_ROT, 2, dtype=F32) / HALF_ROT))
    ang_r = row_idx[:, None] * freqs[None, :]
    ang_c = col_idx[:, None] * freqs[None, :]
    cr, sr, cc, sc = jnp.cos(ang_r), jnp.sin(ang_r), jnp.cos(ang_c), jnp.sin(ang_c)
    cos_h = jnp.concatenate([cr, cr, cc, cc], axis=-1)
    sin_h = jnp.concatenate([-sr, sr, -sc, sc], axis=-1)
    return jnp.tile(cos_h, (1, LANES // HEAD_DIM)), jnp.tile(sin_h, (1, LANES // HEAD_DIM))


def _prepare_weights(norm1_g, w_in, conv_dw_w, conv_dw_b, conv_ln_g, conv_ln_b, q_norm_g, k_norm_g,
                     conv_out_g, attn_out_g, w_out, norm2_g, w_router, w_gate, w_up, w_down,
                     ple_proj, ple_norm_g, ple_gate_w, ple_gate_b):
    i = 0
    q_scale = (HEAD_DIM ** -0.5) * math.log2(math.e)
    qkg = jnp.concatenate([jnp.tile(q_norm_g[i] * q_scale, N_HEADS), jnp.tile(k_norm_g[i], N_KV_HEADS)])
    bound = HEAD_DIM * jnp.max(jnp.abs(q_norm_g[i] * q_scale)) * jnp.max(jnp.abs(k_norm_g[i]))
    use_shift = bound <= MAX_SCORE_SHIFT
    lane = jnp.arange(LANES)
    fill = jnp.stack([jnp.where(lane == HEAD_DIM, 1.0, 0.0),
                      jnp.where(lane == HEAD_DIM, -jnp.where(use_shift, bound, 0.0), 0.0)]).astype(F32)
    ones_bd = (lane[:, None] // HEAD_DIM == lane[None, :] // HEAD_DIM).astype(BF16)
    tok = jnp.arange(TR)
    wr_hi = w_router[i].astype(BF16)
    wr_lo = (w_router[i] - wr_hi.astype(F32)).astype(BF16)
    wr = jnp.concatenate([wr_hi, wr_lo, jnp.zeros((D_MODEL, LANES - 2 * N_EXPERTS), BF16)], axis=1)
    return dict(
        g1=norm1_g[i][None, :], w_in=w_in[i].astype(BF16), qkg=qkg[None, :], ones_bd=ones_bd,
        fill=fill, use_shift=use_shift,
        dw_w=conv_dw_w[i], dw_b=conv_dw_b[i][None, :], ln_g=conv_ln_g[i][None, :], ln_b=conv_ln_b[i][None, :],
        conv_out_g=conv_out_g[i][None, :], attn_out_g=attn_out_g[i][None, :],
        w_out_c=w_out[i, :D_CONV].astype(BF16), w_out_a=w_out[i, D_CONV:].astype(BF16),
        g2=norm2_g[i][None, :], w_router=wr,
        wg=w_gate[i].astype(BF16), wu=w_up[i].astype(BF16), wd=w_down[i].astype(BF16),
        tri=(tok[:, None] < tok[None, :]).astype(BF16),
        wpp=ple_proj[i].astype(BF16), pg=ple_norm_g[i][None, :], wpg=ple_gate_w[i].astype(BF16),
        bpg=ple_gate_b[i][None, :],
    )


def _trunk(x, p, w):
    b, t, _ = x.shape
    n = b * t
    assert t % TM_OUT == 0 and t % TK == 0 and t % TT == 0 and t % TQ == 0 and n % TR == 0
    cap = CAPACITY_FACTOR * n // N_EXPERTS
    nt = n // TR
    x2 = x.reshape(n, D_MODEL)
    cos_t, sin_t = _rope_tables(t)

    gl, q, k, v = _inproj(x2, w["g1"], w["w_in"], w["qkg"], cos_t, sin_t, w["ones_bd"], w["fill"], t)
    cn = _conv(gl.reshape(b, t, D_CONV), w["dw_w"], w["dw_b"], w["ln_g"], w["ln_b"], w["conv_out_g"])
    ao = lax.cond(w["use_shift"], _attention_shifted, _attention,
                  q.reshape(b, t, -1), k.reshape(b, t, -1), v)
    h1, xn2, aff_t = _outproj(x2, cn.reshape(n, D_CONV), ao.reshape(n, D_Q), w["attn_out_g"],
                              w["w_out_c"], w["w_out_a"], w["g2"], w["w_router"])

    thr, need = _threshold(aff_t, cap)
    gsel, lpos, cnt = _masks(aff_t, thr, need, w["tri"])
    cnt = cnt[:, :, 0].T
    seg = (cnt + BF16_SUBLANES - 1) // BF16_SUBLANES * BF16_SUBLANES
    ends = jnp.cumsum(seg, axis=1)
    off = (ends - seg).astype(jnp.int32).reshape(-1)
    nch = ((cnt + CH - 1) // CH).astype(jnp.int32).reshape(-1)
    cap_rows = (cap + nt * (BF16_SUBLANES - 1) + CH + TMF - 1) // TMF * TMF
    ntile = jnp.maximum((ends[:, -1] + TMF - 1) // TMF, 1).astype(jnp.int32)

    xe = _dispatch(off, nch, xn2, gsel, lpos, cap, cap_rows)
    ye = _ffn(ntile, xe, w["wg"], w["wu"], w["wd"])
    y = _combine(off, nch, h1, gsel, lpos, p.reshape(n, D_PLE), w["wpp"], w["pg"], w["wpg"],
                 w["bpg"], ye)
    return y.reshape(b, t, D_MODEL)


def kernel(x_prompt, x_sample, p_prompt, p_sample, norm1_g, w_in, conv_dw_w, conv_dw_b, conv_ln_g, conv_ln_b, q_norm_g, k_norm_g, conv_out_g, attn_out_g, w_out, norm2_g, w_router, w_gate, w_up, w_down, ple_proj, ple_norm_g, ple_gate_w, ple_gate_b):
    w = _prepare_weights(norm1_g, w_in, conv_dw_w, conv_dw_b, conv_ln_g, conv_ln_b, q_norm_g, k_norm_g,
                         conv_out_g, attn_out_g, w_out, norm2_g, w_router, w_gate, w_up, w_down,
                         ple_proj, ple_norm_g, ple_gate_w, ple_gate_b)
    y_prompt = _trunk(x_prompt, p_prompt[0], w)
    y_sample = _trunk(x_sample, p_sample[0], w)
    return (y_prompt, y_sample)
```

```python
import functools
import math

import jax
import jax.numpy as jnp
from jax import lax
from jax.experimental import pallas as pl
from jax.experimental.pallas import tpu as pltpu

D_MODEL = 1024
D_CONV = 512
CONV_WIDTH = 31
CONV_PAD = CONV_WIDTH // 2
N_HEADS = 8
N_KV_HEADS = 2
HEAD_DIM = 64
KV_GROUP = N_HEADS // N_KV_HEADS
D_Q = N_HEADS * HEAD_DIM
D_KV = N_KV_HEADS * HEAD_DIM
D_IN = 2 * D_CONV + D_Q + 2 * D_KV
HALF_ROT = HEAD_DIM // 2
ROPE_THETA = 10000.0
GRID_W = 64
N_EXPERTS = 16
CAPACITY_FACTOR = 2
D_FF_EXPERT = 2048
D_PLE = 256
EPS = 1e-6

LANES = 128
F32_SUBLANES = 8
BF16_SUBLANES = 16
VMEM_LIMIT = 56 * 1024 * 1024

TM = 512
TM_OUT = 1024
TT = 256
HALO = 16
TQ = 256
TK = 512
KV_CHUNKS = 8
MAX_SCORE_SHIFT = 40.0
TR = 512
CH = 128
TMF = 512

F32 = jnp.float32
BF16 = jnp.bfloat16


def _cparams(sem):
    return pltpu.CompilerParams(dimension_semantics=sem, vmem_limit_bytes=VMEM_LIMIT)


def _sigmoid(x):
    return 1.0 / (1.0 + jnp.exp(-x))


def _inproj_body(x_ref, g1_ref, w_ref, qkg_ref, cos_ref, sin_ref, ones_ref, fill_ref,
                 gl_ref, q_ref, k_ref, v_ref):
    for blk in range(TM_OUT // TM):
        _inproj_rows(slice(blk * TM, (blk + 1) * TM), x_ref, g1_ref, w_ref, qkg_ref, cos_ref, sin_ref,
                     ones_ref, fill_ref, gl_ref, q_ref, k_ref, v_ref)


def _inproj_rows(rows, x_ref, g1_ref, w_ref, qkg_ref, cos_ref, sin_ref, ones_ref, fill_ref,
                 gl_ref, q_ref, k_ref, v_ref):
    x = x_ref[rows, :]
    a = x * lax.rsqrt(jnp.mean(x * x, axis=-1, keepdims=True) + EPS) * g1_ref[...]
    z = jnp.dot(a.astype(BF16), w_ref[...], preferred_element_type=F32)
    val = z[:, :D_CONV]
    gate = z[:, D_CONV:2 * D_CONV]
    gl_ref[rows, :] = (val * _sigmoid(gate)).astype(BF16)

    lane = lax.broadcasted_iota(jnp.int32, (x.shape[0], LANES), 1)
    first_half = (lane % HALF_ROT) < (HALF_ROT // 2)
    low_head = lane < HEAD_DIM
    cos = cos_ref[rows, :]
    sin = sin_ref[rows, :]
    o0 = 2 * D_CONV
    n_chunks = (D_Q + D_KV) // LANES
    for c in range(n_chunks):
        qc = z[:, o0 + c * LANES:o0 + (c + 1) * LANES]
        ssum = jnp.dot((qc * qc).astype(BF16), ones_ref[...], preferred_element_type=F32)
        qn = qc * lax.rsqrt(ssum * (1.0 / HEAD_DIM) + EPS) * qkg_ref[:, c * LANES:(c + 1) * LANES]
        partner = jnp.where(first_half, pltpu.roll(qn, LANES - HALF_ROT // 2, 1),
                            pltpu.roll(qn, HALF_ROT // 2, 1))
        qr = qn * cos + partner * sin
        fill = fill_ref[0:1, :] if c < D_Q // LANES else fill_ref[1:2, :]
        even = jnp.where(low_head, qr, fill).astype(BF16)
        odd = jnp.where(low_head, pltpu.roll(qr, HEAD_DIM, 1), fill).astype(BF16)
        if c < D_Q // LANES:
            q_ref[rows, (2 * c) * LANES:(2 * c + 1) * LANES] = even
            q_ref[rows, (2 * c + 1) * LANES:(2 * c + 2) * LANES] = odd
        else:
            k_ref[rows, 0:LANES] = even
            k_ref[rows, LANES:2 * LANES] = odd
    vv = z[:, o0 + D_Q + D_KV:]
    v_ref[0:LANES, rows] = jnp.where(low_head, vv, 1.0).T.astype(BF16)
    v_ref[LANES:2 * LANES, rows] = jnp.where(low_head, pltpu.roll(vv, HEAD_DIM, 1), 1.0).T.astype(BF16)


def _inproj(x2, g1, w_in_bf, qkg, cos_t, sin_t, ones_bd, fill, seq_len):
    n = x2.shape[0]
    pos_tiles = seq_len // TM_OUT
    row = lambda i: (i, 0)
    const = lambda i: (0, 0)
    return pl.pallas_call(
        _inproj_body,
        grid=(n // TM_OUT,),
        in_specs=[
            pl.BlockSpec((TM_OUT, D_MODEL), row),
            pl.BlockSpec((1, D_MODEL), const),
            pl.BlockSpec((D_MODEL, D_IN), const),
            pl.BlockSpec((1, D_Q + D_KV), const),
            pl.BlockSpec((TM_OUT, LANES), lambda i: (i % pos_tiles, 0)),
            pl.BlockSpec((TM_OUT, LANES), lambda i: (i % pos_tiles, 0)),
            pl.BlockSpec((LANES, LANES), const),
            pl.BlockSpec((2, LANES), const),
        ],
        out_specs=[
            pl.BlockSpec((TM_OUT, D_CONV), row),
            pl.BlockSpec((TM_OUT, N_HEADS * LANES), row),
            pl.BlockSpec((TM_OUT, N_KV_HEADS * LANES), row),
            pl.BlockSpec((N_KV_HEADS * LANES, TM_OUT), lambda i: (0, i)),
        ],
        out_shape=[
            jax.ShapeDtypeStruct((n, D_CONV), BF16),
            jax.ShapeDtypeStruct((n, N_HEADS * LANES), BF16),
            jax.ShapeDtypeStruct((n, N_KV_HEADS * LANES), BF16),
            jax.ShapeDtypeStruct((N_KV_HEADS * LANES, n), BF16),
        ],
        compiler_params=_cparams(("parallel",)),
        name="inproj",
    )(x2, g1, w_in_bf, qkg, cos_t, sin_t, ones_bd, fill)


CONV_ROWS = 64
CONV_SPAN = TT + 2 * HALO - F32_SUBLANES


def _conv_body(left_ref, main_ref, right_ref, w_ref, b_ref, lng_ref, lnb_ref, og_ref,
               out_ref, win_ref, conv_ref):
    i = pl.program_id(1)
    last = pl.num_programs(1) - 1
    left = left_ref[0].astype(F32)
    right = right_ref[0].astype(F32)
    win_ref[0, 0:HALO, :] = jnp.where(i > 0, left, 0.0)
    win_ref[0, HALO:HALO + TT, :] = main_ref[0].astype(F32)
    win_ref[0, HALO + TT:HALO + TT + HALO, :] = jnp.where(i < last, right, 0.0)
    for r in range(1, F32_SUBLANES):
        win_ref[r, 0:CONV_SPAN, :] = win_ref[0, r:r + CONV_SPAN, :]
    base = HALO - CONV_PAD

    def row_block(rr, carry):
        r0 = pl.multiple_of(rr * CONV_ROWS, CONV_ROWS)
        groups = CONV_ROWS // F32_SUBLANES
        reach = (base + CONV_WIDTH - 1) // F32_SUBLANES + 1
        for c in range(D_CONV // LANES):
            ls = slice(c * LANES, (c + 1) * LANES)
            accs = [None] * groups
            for shift in range(F32_SUBLANES):
                for q in range(groups + reach - 1):
                    uses = [(q - a, a * F32_SUBLANES + shift - base) for a in range(reach)
                            if 0 <= a * F32_SUBLANES + shift - base < CONV_WIDTH and 0 <= q - a < groups]
                    if not uses:
                        continue
                    tile = win_ref[shift, pl.ds(r0 + q * F32_SUBLANES, F32_SUBLANES), ls]
                    for g, k in uses:
                        term = tile * w_ref[k:k + 1, ls]
                        accs[g] = term if accs[g] is None else accs[g] + term
            for g in range(groups):
                conv_ref[pl.ds(r0 + g * F32_SUBLANES, F32_SUBLANES), ls] = accs[g]
        return carry

    lax.fori_loop(0, TT // CONV_ROWS, row_block, 0)
    cv = conv_ref[...] + b_ref[...]
    mu = jnp.mean(cv, axis=-1, keepdims=True)
    d = cv - mu
    var = jnp.mean(d * d, axis=-1, keepdims=True)
    y = d * lax.rsqrt(var + EPS) * lng_ref[...] + lnb_ref[...]
    y = y * _sigmoid(y)
    y = y * lax.rsqrt(jnp.mean(y * y, axis=-1, keepdims=True) + EPS) * og_ref[...]
    out_ref[0] = y.astype(BF16)


def _conv(gl3, dw_w, dw_b, ln_g, ln_b, out_g):
    b, t, _ = gl3.shape
    hb = TT // HALO
    n_halo = t // HALO
    const = lambda bb, i: (0, 0)
    return pl.pallas_call(
        _conv_body,
        grid=(b, t // TT),
        in_specs=[
            pl.BlockSpec((1, HALO, D_CONV), lambda bb, i: (bb, jnp.maximum(i * hb - 1, 0), 0)),
            pl.BlockSpec((1, TT, D_CONV), lambda bb, i: (bb, i, 0)),
            pl.BlockSpec((1, HALO, D_CONV), lambda bb, i: (bb, jnp.minimum((i + 1) * hb, n_halo - 1), 0)),
            pl.BlockSpec((CONV_WIDTH, D_CONV), const),
            pl.BlockSpec((1, D_CONV), const),
            pl.BlockSpec((1, D_CONV), const),
            pl.BlockSpec((1, D_CONV), const),
            pl.BlockSpec((1, D_CONV), const),
        ],
        out_specs=pl.BlockSpec((1, TT, D_CONV), lambda bb, i: (bb, i, 0)),
        out_shape=jax.ShapeDtypeStruct((b, t, D_CONV), BF16),
        scratch_shapes=[pltpu.VMEM((F32_SUBLANES, TT + 2 * HALO, D_CONV), F32),
                        pltpu.VMEM((TT, D_CONV), F32)],
        compiler_params=_cparams(("parallel", "parallel")),
        name="conv",
    )(gl3, gl3, gl3, dw_w, dw_b, ln_g, ln_b, out_g)


def _attn_body(q_ref, k_ref, vt_ref, o_ref, qs_ref, m_ref, acc_ref):
    t = k_ref.shape[1]
    nt_dims = (((1,), (1,)), ((), ()))
    for h in range(KV_GROUP):
        qs_ref[h * TQ:(h + 1) * TQ, :] = q_ref[0, :, h * LANES:(h + 1) * LANES]
    m_ref[...] = jnp.full(m_ref.shape, -jnp.inf, F32)
    acc_ref[...] = jnp.zeros(acc_ref.shape, F32)

    def step(kt, carry):
        start = pl.multiple_of(kt * TK, TK)
        s = lax.dot_general(qs_ref[...], k_ref[0, pl.ds(start, TK), :], nt_dims,
                            preferred_element_type=F32)
        m_old = m_ref[...]
        m_new = jnp.maximum(m_old, jnp.max(s, axis=-1, keepdims=True))
        p = jnp.exp2(s - m_new[:, 0:1])
        alpha = jnp.exp2(m_old - m_new)
        acc_ref[...] = alpha * acc_ref[...] + lax.dot_general(
            p.astype(BF16), vt_ref[:, pl.ds(start, TK)], nt_dims, preferred_element_type=F32)
        m_ref[...] = m_new
        return carry

    lax.fori_loop(0, t // TK, step, 0)
    acc = acc_ref[...]
    res = acc / pltpu.roll(acc, HEAD_DIM, 1)
    _attn_store(o_ref, [res[h * TQ:(h + 1) * TQ, :] for h in range(KV_GROUP)])


def _attn_shifted_body(q_ref, k_ref, vt_ref, o_ref, qs_ref, acc_ref):
    t = k_ref.shape[1]
    for h in range(KV_GROUP):
        qs_ref[h * TQ:(h + 1) * TQ, :] = q_ref[0, :, h * LANES:(h + 1) * LANES]
    acc_ref[...] = jnp.zeros(acc_ref.shape, F32)

    chunks = math.gcd(KV_CHUNKS, t // TK)

    def step(kt, carry):
        qs = qs_ref[...]
        part = None
        for c in range(chunks):
            start = pl.multiple_of(kt * (chunks * TK) + c * TK, TK)
            st = lax.dot_general(k_ref[0, pl.ds(start, TK), :], qs, (((1,), (1,)), ((), ())),
                                 preferred_element_type=F32)
            pv = jnp.dot(vt_ref[:, pl.ds(start, TK)], jnp.exp2(st).astype(BF16),
                         preferred_element_type=F32)
            part = pv if part is None else part + pv
        acc_ref[...] += part
        return carry

    lax.fori_loop(0, t // (chunks * TK), step, 0)
    acc = acc_ref[...]
    res = acc * (1.0 / acc[HEAD_DIM:HEAD_DIM + 1, :])
    _attn_store(o_ref, [res[:, h * TQ:(h + 1) * TQ].T for h in range(KV_GROUP)])


def _attn_store(o_ref, heads):
    lane = lax.broadcasted_iota(jnp.int32, (TQ, LANES), 1)
    low = lane < HEAD_DIM
    for hp in range(KV_GROUP // 2):
        o_ref[0, :, hp * LANES:(hp + 1) * LANES] = jnp.where(
            low, heads[2 * hp], pltpu.roll(heads[2 * hp + 1], HEAD_DIM, 1)).astype(BF16)


def _attention_call(body, name, scratch, q3, k3, vt):
    b, t, _ = q3.shape
    gw = KV_GROUP * LANES
    return pl.pallas_call(
        body,
        grid=(b, N_KV_HEADS, t // TQ),
        in_specs=[
            pl.BlockSpec((1, TQ, gw), lambda bb, j, i: (bb, i, j)),
            pl.BlockSpec((1, t, LANES), lambda bb, j, i: (bb, 0, j)),
            pl.BlockSpec((LANES, t), lambda bb, j, i: (j, bb)),
        ],
        out_specs=pl.BlockSpec((1, TQ, KV_GROUP * HEAD_DIM), lambda bb, j, i: (bb, i, j)),
        out_shape=jax.ShapeDtypeStruct((b, t, D_Q), BF16),
        scratch_shapes=[pltpu.VMEM((KV_GROUP * TQ, LANES), BF16)] + scratch,
        compiler_params=_cparams(("parallel", "parallel", "parallel")),
        name=name,
    )(q3, k3, vt)


def _attention_shifted(q3, k3, vt):
    return _attention_call(_attn_shifted_body, "attention_shifted",
                           [pltpu.VMEM((LANES, KV_GROUP * TQ), F32)], q3, k3, vt)


def _attention(q3, k3, vt):
    return _attention_call(_attn_body, "attention",
                           [pltpu.VMEM((KV_GROUP * TQ, LANES), F32),
                            pltpu.VMEM((KV_GROUP * TQ, LANES), F32)], q3, k3, vt)


def _outproj_body(x_ref, cn_ref, ao_ref, ag_ref, wc_ref, wa_ref, g2_ref, wr_ref,
                  h_ref, xn_ref, aff_ref):
    for blk in range(TM_OUT // TM):
        rows = slice(blk * TM, (blk + 1) * TM)
        ao = ao_ref[rows, :].astype(F32)
        an = ao * lax.rsqrt(jnp.mean(ao * ao, axis=-1, keepdims=True) + EPS) * ag_ref[...]
        h = (x_ref[rows, :]
             + jnp.dot(cn_ref[rows, :], wc_ref[...], preferred_element_type=F32)
             + jnp.dot(an.astype(BF16), wa_ref[...], preferred_element_type=F32))
        h_ref[rows, :] = h
        xn = h * lax.rsqrt(jnp.mean(h * h, axis=-1, keepdims=True) + EPS) * g2_ref[...]
        xn_hi = xn.astype(BF16)
        xn_ref[rows, :] = xn_hi
        xn_lo = (xn - xn_hi.astype(F32)).astype(BF16)
        parts = (jnp.dot(xn_hi, wr_ref[...], preferred_element_type=F32)
                 + jnp.dot(xn_lo, wr_ref[...], preferred_element_type=F32))
        parts_t = parts.T
        logits = parts_t[0:N_EXPERTS, :] + parts_t[N_EXPERTS:2 * N_EXPERTS, :]
        mx = jnp.max(logits, axis=0, keepdims=True)
        ex = jnp.exp(logits - mx)
        aff_ref[:, rows] = ex / jnp.sum(ex, axis=0, keepdims=True)


def _outproj(x2, cn, ao, ag, wc, wa, g2, wr):
    n = x2.shape[0]
    row = lambda i: (i, 0)
    const = lambda i: (0, 0)
    return pl.pallas_call(
        _outproj_body,
        grid=(n // TM_OUT,),
        in_specs=[
            pl.BlockSpec((TM_OUT, D_MODEL), row),
            pl.BlockSpec((TM_OUT, D_CONV), row),
            pl.BlockSpec((TM_OUT, D_Q), row),
            pl.BlockSpec((1, D_Q), const),
            pl.BlockSpec((D_CONV, D_MODEL), const),
            pl.BlockSpec((D_Q, D_MODEL), const),
            pl.BlockSpec((1, D_MODEL), const),
            pl.BlockSpec((D_MODEL, LANES), const),
        ],
        out_specs=[
            pl.BlockSpec((TM_OUT, D_MODEL), row),
            pl.BlockSpec((TM_OUT, D_MODEL), row),
            pl.BlockSpec((N_EXPERTS, TM_OUT), lambda i: (0, i)),
        ],
        out_shape=[
            jax.ShapeDtypeStruct((n, D_MODEL), F32),
            jax.ShapeDtypeStruct((n, D_MODEL), BF16),
            jax.ShapeDtypeStruct((N_EXPERTS, n), F32),
        ],
        compiler_params=_cparams(("parallel",)),
        name="outproj",
    )(x2, cn, ao, ag, wc, wa, g2, wr)


def _threshold_body(cap, aff_ref, thr_ref, need_ref):
    def step(it, lo):
        cand = lo | (jnp.int32(1) << (30 - it))
        bits = pltpu.bitcast(aff_ref[...], jnp.int32)
        cnt = jnp.sum((bits >= cand).astype(jnp.int32), axis=1, keepdims=True)
        return jnp.where(cnt >= cap, cand, lo)

    thr = lax.fori_loop(0, 31, step, jnp.zeros((N_EXPERTS, 1), jnp.int32))
    bits = pltpu.bitcast(aff_ref[...], jnp.int32)
    n_gt = jnp.sum((bits > thr).astype(jnp.int32), axis=1, keepdims=True)
    thr_ref[...] = jnp.broadcast_to(thr, thr_ref.shape)
    need_ref[...] = jnp.broadcast_to(cap - n_gt, need_ref.shape)


def _threshold(aff_t, cap):
    n = aff_t.shape[1]
    full = lambda: (0, 0)
    return pl.pallas_call(
        functools.partial(_threshold_body, cap),
        in_specs=[pl.BlockSpec((N_EXPERTS, n), full)],
        out_specs=[pl.BlockSpec((N_EXPERTS, LANES), full), pl.BlockSpec((N_EXPERTS, LANES), full)],
        out_shape=[jax.ShapeDtypeStruct((N_EXPERTS, LANES), jnp.int32),
                   jax.ShapeDtypeStruct((N_EXPERTS, LANES), jnp.int32)],
        compiler_params=pltpu.CompilerParams(vmem_limit_bytes=VMEM_LIMIT),
        name="threshold",
    )(aff_t)


def _mask_body(aff_ref, thr_ref, need_ref, tri_ref, gsel_ref, lpos_ref, cnt_ref, eqc_ref):
    @pl.when(pl.program_id(0) == 0)
    def _():
        eqc_ref[...] = jnp.zeros(eqc_ref.shape, F32)

    aff = aff_ref[...]
    bits = pltpu.bitcast(aff, jnp.int32)
    thr = thr_ref[:, 0:1]
    need = need_ref[:, 0:1].astype(F32)
    gt = bits > thr
    eq = bits == thr
    eq_f = jnp.where(eq, 1.0, 0.0)
    eq_rank = eqc_ref[:, 0:1] + jnp.dot(eq_f.astype(BF16), tri_ref[...], preferred_element_type=F32)
    sel = gt | (eq & (eq_rank < need))
    sel_f = jnp.where(sel, 1.0, 0.0)
    gsel_ref[...] = jnp.where(sel, aff, -1.0)
    lpos_ref[...] = jnp.dot(sel_f.astype(BF16), tri_ref[...], preferred_element_type=F32)
    cnt = jnp.sum(sel_f, axis=1, keepdims=True)
    cnt_ref[0] = jnp.broadcast_to(cnt, (N_EXPERTS, LANES)).astype(jnp.int32)
    eqc_ref[...] = eqc_ref[...] + jnp.sum(eq_f, axis=1, keepdims=True)


def _masks(aff_t, thr, need, tri):
    n = aff_t.shape[1]
    nt = n // TR
    const = lambda i: (0, 0)
    tile = lambda i: (0, i)
    return pl.pallas_call(
        _mask_body,
        grid=(nt,),
        in_specs=[
            pl.BlockSpec((N_EXPERTS, TR), tile),
            pl.BlockSpec((N_EXPERTS, LANES), const),
            pl.BlockSpec((N_EXPERTS, LANES), const),
            pl.BlockSpec((TR, TR), const),
        ],
        out_specs=[
            pl.BlockSpec((N_EXPERTS, TR), tile),
            pl.BlockSpec((N_EXPERTS, TR), tile),
            pl.BlockSpec((1, N_EXPERTS, LANES), lambda i: (i, 0, 0)),
        ],
        out_shape=[
            jax.ShapeDtypeStruct((N_EXPERTS, n), F32),
            jax.ShapeDtypeStruct((N_EXPERTS, n), F32),
            jax.ShapeDtypeStruct((nt, N_EXPERTS, LANES), jnp.int32),
        ],
        scratch_shapes=[pltpu.VMEM((N_EXPERTS, LANES), F32)],
        compiler_params=_cparams(("arbitrary",)),
        name="masks",
    )(aff_t, thr, need, tri)


def _onehot_rows(gsel_row, lpos_row, first_slot, value):
    slot = lax.broadcasted_iota(jnp.int32, (CH, TR), 0).astype(F32) + first_slot.astype(F32)
    return jnp.where((gsel_row >= 0.0) & (lpos_row == slot), value, 0.0).astype(BF16)


def _dispatch_body(cap, base_ref, shift_ref, nch_ref, keepc_ref, keepr_ref, xn_ref, gsel_ref, lpos_ref,
                   xe_ref, stage_ref, carry_ref, sem_ref, xsem_ref):
    n = pl.program_id(0)
    nt = pl.num_programs(0)

    def copy(e, slot, sem, chunk):
        row0 = pl.multiple_of(base_ref[e * nt + n] + chunk * CH, BF16_SUBLANES)
        return pltpu.make_async_copy(stage_ref.at[slot], xe_ref.at[e, pl.ds(row0, CH)], sem)

    @pl.when(n == 0)
    def _():
        carry_ref[...] = jnp.zeros(carry_ref.shape, BF16)
        stage_ref[N_EXPERTS] = jnp.zeros((CH, D_MODEL), BF16)
        first = cap // CH * CH
        fills = [pltpu.make_async_copy(stage_ref.at[N_EXPERTS], xe_ref.at[e, pl.ds(row0, CH)], xsem_ref.at[0])
                 for e in range(N_EXPERTS) for row0 in range(first, xe_ref.shape[1], CH)]
        for cp in fills:
            cp.start()
        for cp in fills:
            cp.wait()

    def rows_of(e, chunk):
        hot = _onehot_rows(gsel_ref[pl.ds(e, 1), :], lpos_ref[pl.ds(e, 1), :],
                           chunk * CH - shift_ref[e * nt + n], 1.0)
        return jnp.dot(hot, xn_ref[...], preferred_element_type=F32)

    def keep_group(e, slot):
        row0 = pl.multiple_of(keepr_ref[e * nt + n], BF16_SUBLANES)
        carry_ref[e] = stage_ref[slot, pl.ds(row0, BF16_SUBLANES), :]

    for e in range(N_EXPERTS):
        rows = rows_of(e, jnp.int32(0))
        stage_ref[e] = rows.astype(BF16)
        stage_ref[e, 0:BF16_SUBLANES, :] = (rows[0:BF16_SUBLANES] + carry_ref[e].astype(F32)).astype(BF16)
        copy(e, e, sem_ref.at[e], 0).start()

        @pl.when(keepc_ref[e * nt + n] == 0)
        def _():
            keep_group(e, e)

        @pl.when(keepc_ref[e * nt + n] >= nch_ref[e * nt + n])
        def _():
            carry_ref[e] = jnp.zeros((BF16_SUBLANES, D_MODEL), BF16)
    for e in range(N_EXPERTS):
        copy(e, e, sem_ref.at[e], 0).wait()

    def per_expert(e, carry):
        def per_chunk(chunk, c):
            stage_ref[N_EXPERTS] = rows_of(e, chunk).astype(BF16)
            cp = copy(e, N_EXPERTS, xsem_ref.at[0], chunk)
            cp.start()

            @pl.when(keepc_ref[e * nt + n] == chunk)
            def _():
                keep_group(e, N_EXPERTS)
            cp.wait()
            return c
        return lax.fori_loop(1, nch_ref[e * nt + n], per_chunk, carry)

    lax.fori_loop(0, N_EXPERTS, per_expert, 0)


def _dispatch(base, shift, nch, keepc, keepr, xn2, gsel, lpos, cap, cap_rows):
    n = xn2.shape[0]
    nt = n // TR
    grid_spec = pltpu.PrefetchScalarGridSpec(
        num_scalar_prefetch=5,
        grid=(nt,),
        in_specs=[
            pl.BlockSpec((TR, D_MODEL), lambda i, *_: (i, 0)),
            pl.BlockSpec((N_EXPERTS, TR), lambda i, *_: (0, i)),
            pl.BlockSpec((N_EXPERTS, TR), lambda i, *_: (0, i)),
        ],
        out_specs=pl.BlockSpec(memory_space=pl.ANY),
        scratch_shapes=[pltpu.VMEM((N_EXPERTS + 1, CH, D_MODEL), BF16),
                        pltpu.VMEM((N_EXPERTS, BF16_SUBLANES, D_MODEL), BF16),
                        pltpu.SemaphoreType.DMA((N_EXPERTS,)),
                        pltpu.SemaphoreType.DMA((1,))],
    )
    return pl.pallas_call(
        functools.partial(_dispatch_body, cap),
        grid_spec=grid_spec,
        out_shape=jax.ShapeDtypeStruct((N_EXPERTS, cap_rows, D_MODEL), BF16),
        compiler_params=_cparams(("arbitrary",)),
        name="dispatch",
    )(base, shift, nch, keepc, keepr, xn2, gsel, lpos)


def _ffn_body(ntile_ref, x_ref, wg_ref, wu_ref, wd_ref, y_ref):
    e = pl.program_id(0)
    i = pl.program_id(1)

    @pl.when(i < ntile_ref[e])
    def _():
        x = x_ref[0]
        g = jnp.dot(x, wg_ref[0], preferred_element_type=F32)
        u = jnp.dot(x, wu_ref[0], preferred_element_type=F32)
        hid = (g * _sigmoid(g) * u).astype(BF16)
        y_ref[0] = jnp.dot(hid, wd_ref[0], preferred_element_type=F32).astype(BF16)

    @pl.when(i >= ntile_ref[e])
    def _():
        y_ref[...] = jnp.zeros(y_ref.shape, BF16)


def _ffn(ntile, xe, wg, wu, wd):
    cap_rows = xe.shape[1]
    rows = lambda e, i, nt_ref: (e, jnp.minimum(i, nt_ref[e] - 1), 0)
    wmap = lambda e, i, nt_ref: (e, 0, 0)
    grid_spec = pltpu.PrefetchScalarGridSpec(
        num_scalar_prefetch=1,
        grid=(N_EXPERTS, cap_rows // TMF),
        in_specs=[
            pl.BlockSpec((1, TMF, D_MODEL), rows),
            pl.BlockSpec((1, D_MODEL, D_FF_EXPERT), wmap),
            pl.BlockSpec((1, D_MODEL, D_FF_EXPERT), wmap),
            pl.BlockSpec((1, D_FF_EXPERT, D_MODEL), wmap),
        ],
        out_specs=pl.BlockSpec((1, TMF, D_MODEL), lambda e, i, nt_ref: (e, i, 0)),
    )
    return pl.pallas_call(
        _ffn_body,
        grid_spec=grid_spec,
        out_shape=jax.ShapeDtypeStruct(xe.shape, BF16),
        compiler_params=_cparams(("arbitrary", "arbitrary")),
        name="expert_ffn",
    )(ntile, xe, wg, wu, wd)


def _combine_body(base_ref, shift_ref, nch_ref, h_ref, gsel_ref, lpos_ref, p_ref, wpp_ref, pg_ref,
                  wpg_ref, bpg_ref, ye_ref, y_ref, ybuf_ref, xbuf_ref, hot_ref, acc_ref, sem_ref, xsem_ref):
    n = pl.program_id(0)
    nt = pl.num_programs(0)
    tn = (((0,), (0,)), ((), ()))
    cur = n % 2

    def first_chunk(e, step, buf):
        row0 = pl.multiple_of(base_ref[e * nt + step], BF16_SUBLANES)
        return pltpu.make_async_copy(ye_ref.at[e, pl.ds(row0, CH)], ybuf_ref.at[buf, pl.ds(e * CH, CH)],
                                     sem_ref.at[buf, e])

    def later_chunk(e, chunk):
        row0 = pl.multiple_of(base_ref[e * nt + n] + chunk * CH, BF16_SUBLANES)
        return pltpu.make_async_copy(ye_ref.at[e, pl.ds(row0, CH)], xbuf_ref, xsem_ref.at[0])

    def gated_hot(e, chunk):
        g = gsel_ref[pl.ds(e, 1), :]
        return _onehot_rows(g, lpos_ref[pl.ds(e, 1), :], chunk * CH - shift_ref[e * nt + n], g)

    @pl.when(n == 0)
    def _():
        for e in range(N_EXPERTS):
            first_chunk(e, 0, 0).start()

    @pl.when(n + 1 < nt)
    def _():
        for e in range(N_EXPERTS):
            first_chunk(e, n + 1, 1 - cur).start()

    for e in range(N_EXPERTS):
        hot_ref[e * CH:(e + 1) * CH, :] = gated_hot(e, jnp.int32(0))
    for e in range(N_EXPERTS):
        first_chunk(e, n, cur).wait()
    acc_ref[...] = h_ref[...] + lax.dot_general(hot_ref[...], ybuf_ref[cur], tn, preferred_element_type=F32)

    def per_expert(e, carry):
        def per_chunk(chunk, c):
            cp = later_chunk(e, chunk)
            cp.start()
            cp.wait()
            acc_ref[...] += lax.dot_general(gated_hot(e, chunk), xbuf_ref[...], tn, preferred_element_type=F32)
            return c
        return lax.fori_loop(1, nch_ref[e * nt + n], per_chunk, carry)

    lax.fori_loop(0, N_EXPERTS, per_expert, 0)

    h2 = acc_ref[...]
    emb = jnp.dot(p_ref[...].astype(BF16), wpp_ref[...], preferred_element_type=F32)
    hn = h2 * lax.rsqrt(jnp.mean(h2 * h2, axis=-1, keepdims=True) + EPS) * pg_ref[...]
    gate = _sigmoid(jnp.dot(hn.astype(BF16), wpg_ref[...], preferred_element_type=F32) + bpg_ref[...])
    y_ref[...] = h2 + gate * emb


def _combine(base, shift, nch, h1, gsel, lpos, p2, wpp, pg, wpg, bpg, ye):
    n = h1.shape[0]
    nt = n // TR
    row = lambda i, *_: (i, 0)
    tile = lambda i, *_: (0, i)
    const = lambda i, *_: (0, 0)
    grid_spec = pltpu.PrefetchScalarGridSpec(
        num_scalar_prefetch=3,
        grid=(nt,),
        in_specs=[
            pl.BlockSpec((TR, D_MODEL), row),
            pl.BlockSpec((N_EXPERTS, TR), tile),
            pl.BlockSpec((N_EXPERTS, TR), tile),
            pl.BlockSpec((TR, D_PLE), row),
            pl.BlockSpec((D_PLE, D_MODEL), const),
            pl.BlockSpec((1, D_MODEL), const),
            pl.BlockSpec((D_MODEL, D_MODEL), const),
            pl.BlockSpec((1, D_MODEL), const),
            pl.BlockSpec(memory_space=pl.ANY),
        ],
        out_specs=pl.BlockSpec((TR, D_MODEL), row),
        scratch_shapes=[pltpu.VMEM((2, N_EXPERTS * CH, D_MODEL), BF16),
                        pltpu.VMEM((CH, D_MODEL), BF16),
                        pltpu.VMEM((N_EXPERTS * CH, TR), BF16),
                        pltpu.VMEM((TR, D_MODEL), F32),
                        pltpu.SemaphoreType.DMA((2, N_EXPERTS)),
                        pltpu.SemaphoreType.DMA((1,))],
    )
    return pl.pallas_call(
        _combine_body,
        grid_spec=grid_spec,
        out_shape=jax.ShapeDtypeStruct((n, D_MODEL), F32),
        compiler_params=_cparams(("arbitrary",)),
        name="combine",
    )(base, shift, nch, h1, gsel, lpos, p2, wpp, pg, wpg, bpg, ye)


def _rope_tables(t):
    rows = t // GRID_W
    row_idx = jnp.repeat(jnp.arange(rows, dtype=F32), GRID_W)
    col_idx = jnp.tile(jnp.arange(GRID_W, dtype=F32), rows)
    freqs = 1.0 / (ROPE_THETA ** (jnp.arange(0, HALF_ROT, 2, dtype=F32) / HALF_ROT))
    ang_r = row_idx[:, None] * freqs[None, :]
    ang_c = col_idx[:, None] * freqs[None, :]
    cr, sr, cc, sc = jnp.cos(ang_r), jnp.sin(ang_r), jnp.cos(ang_c), jnp.sin(ang_c)
    cos_h = jnp.concatenate([cr, cr, cc, cc], axis=-1)
    sin_h = jnp.concatenate([-sr, sr, -sc, sc], axis=-1)
    return jnp.tile(cos_h, (1, LANES // HEAD_DIM)), jnp.tile(sin_h, (1, LANES // HEAD_DIM))


def _prepare_weights(norm1_g, w_in, conv_dw_w, conv_dw_b, conv_ln_g, conv_ln_b, q_norm_g, k_norm_g,
                     conv_out_g, attn_out_g, w_out, norm2_g, w_router, w_gate, w_up, w_down,
                     ple_proj, ple_norm_g, ple_gate_w, ple_gate_b):
    i = 0
    q_scale = (HEAD_DIM ** -0.5) * math.log2(math.e)
    qkg = jnp.concatenate([jnp.tile(q_norm_g[i] * q_scale, N_HEADS), jnp.tile(k_norm_g[i], N_KV_HEADS)])
    bound = HEAD_DIM * jnp.max(jnp.abs(q_norm_g[i] * q_scale)) * jnp.max(jnp.abs(k_norm_g[i]))
    use_shift = bound <= MAX_SCORE_SHIFT
    lane = jnp.arange(LANES)
    fill = jnp.stack([jnp.where(lane == HEAD_DIM, 1.0, 0.0),
                      jnp.where(lane == HEAD_DIM, -jnp.where(use_shift, bound, 0.0), 0.0)]).astype(F32)
    ones_bd = (lane[:, None] // HEAD_DIM == lane[None, :] // HEAD_DIM).astype(BF16)
    tok = jnp.arange(TR)
    wr_hi = w_router[i].astype(BF16)
    wr_lo = (w_router[i] - wr_hi.astype(F32)).astype(BF16)
    wr = jnp.concatenate([wr_hi, wr_lo, jnp.zeros((D_MODEL, LANES - 2 * N_EXPERTS), BF16)], axis=1)
    return dict(
        g1=norm1_g[i][None, :], w_in=w_in[i].astype(BF16), qkg=qkg[None, :], ones_bd=ones_bd,
        fill=fill, use_shift=use_shift,
        dw_w=conv_dw_w[i], dw_b=conv_dw_b[i][None, :], ln_g=conv_ln_g[i][None, :], ln_b=conv_ln_b[i][None, :],
        conv_out_g=conv_out_g[i][None, :], attn_out_g=attn_out_g[i][None, :],
        w_out_c=w_out[i, :D_CONV].astype(BF16), w_out_a=w_out[i, D_CONV:].astype(BF16),
        g2=norm2_g[i][None, :], w_router=wr,
        wg=w_gate[i].astype(BF16), wu=w_up[i].astype(BF16), wd=w_down[i].astype(BF16),
        tri=(tok[:, None] < tok[None, :]).astype(BF16),
        wpp=ple_proj[i].astype(BF16), pg=ple_norm_g[i][None, :], wpg=ple_gate_w[i].astype(BF16),
        bpg=ple_gate_b[i][None, :],
    )


def _trunk(x, p, w):
    b, t, _ = x.shape
    n = b * t
    assert t % TM_OUT == 0 and t % TK == 0 and t % TT == 0 and t % TQ == 0 and n % TR == 0
    cap = CAPACITY_FACTOR * n // N_EXPERTS
    nt = n // TR
    x2 = x.reshape(n, D_MODEL)
    cos_t, sin_t = _rope_tables(t)

    gl, q, k, v = _inproj(x2, w["g1"], w["w_in"], w["qkg"], cos_t, sin_t, w["ones_bd"], w["fill"], t)
    cn = _conv(gl.reshape(b, t, D_CONV), w["dw_w"], w["dw_b"], w["ln_g"], w["ln_b"], w["conv_out_g"])
    ao = lax.cond(w["use_shift"], _attention_shifted, _attention,
                  q.reshape(b, t, -1), k.reshape(b, t, -1), v)
    h1, xn2, aff_t = _outproj(x2, cn.reshape(n, D_CONV), ao.reshape(n, D_Q), w["attn_out_g"],
                              w["w_out_c"], w["w_out_a"], w["g2"], w["w_router"])

    thr, need = _threshold(aff_t, cap)
    gsel, lpos, cnt = _masks(aff_t, thr, need, w["tri"])
    cnt = cnt[:, :, 0].T
    ends = jnp.cumsum(cnt, axis=1)
    start = ends - cnt
    base = start // BF16_SUBLANES * BF16_SUBLANES
    shift = start - base
    nch = jnp.maximum((shift + cnt + CH - 1) // CH, 1)
    keep = ends // BF16_SUBLANES * BF16_SUBLANES - base
    flat = lambda a: a.astype(jnp.int32).reshape(-1)
    base, shift, nch, keepc, keepr = flat(base), flat(shift), flat(nch), flat(keep // CH), flat(keep % CH)
    full_tiles = (cap + TMF - 1) // TMF
    cap_rows = (full_tiles + 1) * TMF
    ntile = jnp.full((N_EXPERTS,), full_tiles, jnp.int32)

    xe = _dispatch(base, shift, nch, keepc, keepr, xn2, gsel, lpos, cap, cap_rows)
    ye = _ffn(ntile, xe, w["wg"], w["wu"], w["wd"])
    y = _combine(base, shift, nch, h1, gsel, lpos, p.reshape(n, D_PLE), w["wpp"], w["pg"], w["wpg"],
                 w["bpg"], ye)
    return y.reshape(b, t, D_MODEL)


def kernel(x_prompt, x_sample, p_prompt, p_sample, norm1_g, w_in, conv_dw_w, conv_dw_b, conv_ln_g, conv_ln_b, q_norm_g, k_norm_g, conv_out_g, attn_out_g, w_out, norm2_g, w_router, w_gate, w_up, w_down, ple_proj, ple_norm_g, ple_gate_w, ple_gate_b):
    w = _prepare_weights(norm1_g, w_in, conv_dw_w, conv_dw_b, conv_ln_g, conv_ln_b, q_norm_g, k_norm_g,
                         conv_out_g, attn_out_g, w_out, norm2_g, w_router, w_gate, w_up, w_down,
                         ple_proj, ple_norm_g, ple_gate_w, ple_gate_b)
    y_prompt = _trunk(x_prompt, p_prompt[0], w)
    y_sample = _trunk(x_sample, p_sample[0], w)
    return (y_prompt, y_sample)
```

```python
import functools
import math

import jax
import jax.numpy as jnp
from jax import lax
from jax.experimental import pallas as pl
from jax.experimental.pallas import tpu as pltpu

D_MODEL = 1024
D_CONV = 512
CONV_WIDTH = 31
CONV_PAD = CONV_WIDTH // 2
N_HEADS = 8
N_KV_HEADS = 2
HEAD_DIM = 64
KV_GROUP = N_HEADS // N_KV_HEADS
D_Q = N_HEADS * HEAD_DIM
D_KV = N_KV_HEADS * HEAD_DIM
D_IN = 2 * D_CONV + D_Q + 2 * D_KV
HALF_ROT = HEAD_DIM // 2
ROPE_THETA = 10000.0
GRID_W = 64
N_EXPERTS = 16
CAPACITY_FACTOR = 2
D_FF_EXPERT = 2048
D_PLE = 256
EPS = 1e-6

LANES = 128
F32_SUBLANES = 8
BF16_SUBLANES = 16
VMEM_LIMIT = 56 * 1024 * 1024

TM = 512
TM_OUT = 1024
TT = 256
HALO = 16
TQ = 256
TK = 512
KV_CHUNKS = 8
MAX_SCORE_SHIFT = 40.0
TR = 512
CH = 96
TMF = 512

F32 = jnp.float32
BF16 = jnp.bfloat16


def _cparams(sem):
    return pltpu.CompilerParams(dimension_semantics=sem, vmem_limit_bytes=VMEM_LIMIT)


def _sigmoid(x):
    return 1.0 / (1.0 + jnp.exp(-x))


def _inproj_body(x_ref, g1_ref, w_ref, qkg_ref, cos_ref, sin_ref, ones_ref, fill_ref,
                 gl_ref, q_ref, k_ref, v_ref):
    for blk in range(TM_OUT // TM):
        _inproj_rows(slice(blk * TM, (blk + 1) * TM), x_ref, g1_ref, w_ref, qkg_ref, cos_ref, sin_ref,
                     ones_ref, fill_ref, gl_ref, q_ref, k_ref, v_ref)


def _inproj_rows(rows, x_ref, g1_ref, w_ref, qkg_ref, cos_ref, sin_ref, ones_ref, fill_ref,
                 gl_ref, q_ref, k_ref, v_ref):
    x = x_ref[rows, :]
    a = x * lax.rsqrt(jnp.mean(x * x, axis=-1, keepdims=True) + EPS) * g1_ref[...]
    z = jnp.dot(a.astype(BF16), w_ref[...], preferred_element_type=F32)
    val = z[:, :D_CONV]
    gate = z[:, D_CONV:2 * D_CONV]
    gl_ref[rows, :] = (val * _sigmoid(gate)).astype(BF16)

    lane = lax.broadcasted_iota(jnp.int32, (x.shape[0], LANES), 1)
    first_half = (lane % HALF_ROT) < (HALF_ROT // 2)
    low_head = lane < HEAD_DIM
    cos = cos_ref[rows, :]
    sin = sin_ref[rows, :]
    o0 = 2 * D_CONV
    n_chunks = (D_Q + D_KV) // LANES
    for c in range(n_chunks):
        qc = z[:, o0 + c * LANES:o0 + (c + 1) * LANES]
        ssum = jnp.dot((qc * qc).astype(BF16), ones_ref[...], preferred_element_type=F32)
        qn = qc * lax.rsqrt(ssum * (1.0 / HEAD_DIM) + EPS) * qkg_ref[:, c * LANES:(c + 1) * LANES]
        partner = jnp.where(first_half, pltpu.roll(qn, LANES - HALF_ROT // 2, 1),
                            pltpu.roll(qn, HALF_ROT // 2, 1))
        qr = qn * cos + partner * sin
        fill = fill_ref[0:1, :] if c < D_Q // LANES else fill_ref[1:2, :]
        even = jnp.where(low_head, qr, fill).astype(BF16)
        odd = jnp.where(low_head, pltpu.roll(qr, HEAD_DIM, 1), fill).astype(BF16)
        if c < D_Q // LANES:
            q_ref[rows, (2 * c) * LANES:(2 * c + 1) * LANES] = even
            q_ref[rows, (2 * c + 1) * LANES:(2 * c + 2) * LANES] = odd
        else:
            k_ref[rows, 0:LANES] = even
            k_ref[rows, LANES:2 * LANES] = odd
    vv = z[:, o0 + D_Q + D_KV:]
    v_ref[0:LANES, rows] = jnp.where(low_head, vv, 1.0).T.astype(BF16)
    v_ref[LANES:2 * LANES, rows] = jnp.where(low_head, pltpu.roll(vv, HEAD_DIM, 1), 1.0).T.astype(BF16)


def _inproj(x2, g1, w_in_bf, qkg, cos_t, sin_t, ones_bd, fill, seq_len):
    n = x2.shape[0]
    pos_tiles = seq_len // TM_OUT
    row = lambda i: (i, 0)
    const = lambda i: (0, 0)
    return pl.pallas_call(
        _inproj_body,
        grid=(n // TM_OUT,),
        in_specs=[
            pl.BlockSpec((TM_OUT, D_MODEL), row),
            pl.BlockSpec((1, D_MODEL), const),
            pl.BlockSpec((D_MODEL, D_IN), const),
            pl.BlockSpec((1, D_Q + D_KV), const),
            pl.BlockSpec((TM_OUT, LANES), lambda i: (i % pos_tiles, 0)),
            pl.BlockSpec((TM_OUT, LANES), lambda i: (i % pos_tiles, 0)),
            pl.BlockSpec((LANES, LANES), const),
            pl.BlockSpec((2, LANES), const),
        ],
        out_specs=[
            pl.BlockSpec((TM_OUT, D_CONV), row),
            pl.BlockSpec((TM_OUT, N_HEADS * LANES), row),
            pl.BlockSpec((TM_OUT, N_KV_HEADS * LANES), row),
            pl.BlockSpec((N_KV_HEADS * LANES, TM_OUT), lambda i: (0, i)),
        ],
        out_shape=[
            jax.ShapeDtypeStruct((n, D_CONV), BF16),
            jax.ShapeDtypeStruct((n, N_HEADS * LANES), BF16),
            jax.ShapeDtypeStruct((n, N_KV_HEADS * LANES), BF16),
            jax.ShapeDtypeStruct((N_KV_HEADS * LANES, n), BF16),
        ],
        compiler_params=_cparams(("parallel",)),
        name="inproj",
    )(x2, g1, w_in_bf, qkg, cos_t, sin_t, ones_bd, fill)


CONV_ROWS = 64
CONV_SPAN = TT + 2 * HALO - F32_SUBLANES


def _conv_body(left_ref, main_ref, right_ref, w_ref, b_ref, lng_ref, lnb_ref, og_ref,
               out_ref, win_ref, conv_ref):
    i = pl.program_id(1)
    last = pl.num_programs(1) - 1
    left = left_ref[0].astype(F32)
    right = right_ref[0].astype(F32)
    win_ref[0, 0:HALO, :] = jnp.where(i > 0, left, 0.0)
    win_ref[0, HALO:HALO + TT, :] = main_ref[0].astype(F32)
    win_ref[0, HALO + TT:HALO + TT + HALO, :] = jnp.where(i < last, right, 0.0)
    for r in range(1, F32_SUBLANES):
        win_ref[r, 0:CONV_SPAN, :] = win_ref[0, r:r + CONV_SPAN, :]
    base = HALO - CONV_PAD

    def row_block(rr, carry):
        r0 = pl.multiple_of(rr * CONV_ROWS, CONV_ROWS)
        groups = CONV_ROWS // F32_SUBLANES
        reach = (base + CONV_WIDTH - 1) // F32_SUBLANES + 1
        for c in range(D_CONV // LANES):
            ls = slice(c * LANES, (c + 1) * LANES)
            accs = [None] * groups
            for shift in range(F32_SUBLANES):
                for q in range(groups + reach - 1):
                    uses = [(q - a, a * F32_SUBLANES + shift - base) for a in range(reach)
                            if 0 <= a * F32_SUBLANES + shift - base < CONV_WIDTH and 0 <= q - a < groups]
                    if not uses:
                        continue
                    tile = win_ref[shift, pl.ds(r0 + q * F32_SUBLANES, F32_SUBLANES), ls]
                    for g, k in uses:
                        term = tile * w_ref[k:k + 1, ls]
                        accs[g] = term if accs[g] is None else accs[g] + term
            for g in range(groups):
                conv_ref[pl.ds(r0 + g * F32_SUBLANES, F32_SUBLANES), ls] = accs[g]
        return carry

    lax.fori_loop(0, TT // CONV_ROWS, row_block, 0)
    cv = conv_ref[...] + b_ref[...]
    mu = jnp.mean(cv, axis=-1, keepdims=True)
    d = cv - mu
    var = jnp.mean(d * d, axis=-1, keepdims=True)
    y = d * lax.rsqrt(var + EPS) * lng_ref[...] + lnb_ref[...]
    y = y * _sigmoid(y)
    y = y * lax.rsqrt(jnp.mean(y * y, axis=-1, keepdims=True) + EPS) * og_ref[...]
    out_ref[0] = y.astype(BF16)


def _conv(gl3, dw_w, dw_b, ln_g, ln_b, out_g):
    b, t, _ = gl3.shape
    hb = TT // HALO
    n_halo = t // HALO
    const = lambda bb, i: (0, 0)
    return pl.pallas_call(
        _conv_body,
        grid=(b, t // TT),
        in_specs=[
            pl.BlockSpec((1, HALO, D_CONV), lambda bb, i: (bb, jnp.maximum(i * hb - 1, 0), 0)),
            pl.BlockSpec((1, TT, D_CONV), lambda bb, i: (bb, i, 0)),
            pl.BlockSpec((1, HALO, D_CONV), lambda bb, i: (bb, jnp.minimum((i + 1) * hb, n_halo - 1), 0)),
            pl.BlockSpec((CONV_WIDTH, D_CONV), const),
            pl.BlockSpec((1, D_CONV), const),
            pl.BlockSpec((1, D_CONV), const),
            pl.BlockSpec((1, D_CONV), const),
            pl.BlockSpec((1, D_CONV), const),
        ],
        out_specs=pl.BlockSpec((1, TT, D_CONV), lambda bb, i: (bb, i, 0)),
        out_shape=jax.ShapeDtypeStruct((b, t, D_CONV), BF16),
        scratch_shapes=[pltpu.VMEM((F32_SUBLANES, TT + 2 * HALO, D_CONV), F32),
                        pltpu.VMEM((TT, D_CONV), F32)],
        compiler_params=_cparams(("parallel", "parallel")),
        name="conv",
    )(gl3, gl3, gl3, dw_w, dw_b, ln_g, ln_b, out_g)


def _attn_body(q_ref, k_ref, vt_ref, o_ref, qs_ref, m_ref, acc_ref):
    t = k_ref.shape[1]
    nt_dims = (((1,), (1,)), ((), ()))
    for h in range(KV_GROUP):
        qs_ref[h * TQ:(h + 1) * TQ, :] = q_ref[0, :, h * LANES:(h + 1) * LANES]
    m_ref[...] = jnp.full(m_ref.shape, -jnp.inf, F32)
    acc_ref[...] = jnp.zeros(acc_ref.shape, F32)

    def step(kt, carry):
        start = pl.multiple_of(kt * TK, TK)
        s = lax.dot_general(qs_ref[...], k_ref[0, pl.ds(start, TK), :], nt_dims,
                            preferred_element_type=F32)
        m_old = m_ref[...]
        m_new = jnp.maximum(m_old, jnp.max(s, axis=-1, keepdims=True))
        p = jnp.exp2(s - m_new[:, 0:1])
        alpha = jnp.exp2(m_old - m_new)
        acc_ref[...] = alpha * acc_ref[...] + lax.dot_general(
            p.astype(BF16), vt_ref[:, pl.ds(start, TK)], nt_dims, preferred_element_type=F32)
        m_ref[...] = m_new
        return carry

    lax.fori_loop(0, t // TK, step, 0)
    acc = acc_ref[...]
    res = acc / pltpu.roll(acc, HEAD_DIM, 1)
    _attn_store(o_ref, [res[h * TQ:(h + 1) * TQ, :] for h in range(KV_GROUP)])


def _attn_shifted_body(q_ref, k_ref, vt_ref, o_ref, qs_ref, acc_ref):
    t = k_ref.shape[1]
    for h in range(KV_GROUP):
        qs_ref[h * TQ:(h + 1) * TQ, :] = q_ref[0, :, h * LANES:(h + 1) * LANES]
    acc_ref[...] = jnp.zeros(acc_ref.shape, F32)

    chunks = math.gcd(KV_CHUNKS, t // TK)

    def step(kt, carry):
        qs = qs_ref[...]
        part = None
        for c in range(chunks):
            start = pl.multiple_of(kt * (chunks * TK) + c * TK, TK)
            st = lax.dot_general(k_ref[0, pl.ds(start, TK), :], qs, (((1,), (1,)), ((), ())),
                                 preferred_element_type=F32)
            pv = jnp.dot(vt_ref[:, pl.ds(start, TK)], jnp.exp2(st).astype(BF16),
                         preferred_element_type=F32)
            part = pv if part is None else part + pv
        acc_ref[...] += part
        return carry

    lax.fori_loop(0, t // (chunks * TK), step, 0)
    acc = acc_ref[...]
    res = acc * (1.0 / acc[HEAD_DIM:HEAD_DIM + 1, :])
    _attn_store(o_ref, [res[:, h * TQ:(h + 1) * TQ].T for h in range(KV_GROUP)])


def _attn_store(o_ref, heads):
    lane = lax.broadcasted_iota(jnp.int32, (TQ, LANES), 1)
    low = lane < HEAD_DIM
    for hp in range(KV_GROUP // 2):
        o_ref[0, :, hp * LANES:(hp + 1) * LANES] = jnp.where(
            low, heads[2 * hp], pltpu.roll(heads[2 * hp + 1], HEAD_DIM, 1)).astype(BF16)


def _attention_call(body, name, scratch, q3, k3, vt):
    b, t, _ = q3.shape
    gw = KV_GROUP * LANES
    return pl.pallas_call(
        body,
        grid=(b, N_KV_HEADS, t // TQ),
        in_specs=[
            pl.BlockSpec((1, TQ, gw), lambda bb, j, i: (bb, i, j)),
            pl.BlockSpec((1, t, LANES), lambda bb, j, i: (bb, 0, j)),
            pl.BlockSpec((LANES, t), lambda bb, j, i: (j, bb)),
        ],
        out_specs=pl.BlockSpec((1, TQ, KV_GROUP * HEAD_DIM), lambda bb, j, i: (bb, i, j)),
        out_shape=jax.ShapeDtypeStruct((b, t, D_Q), BF16),
        scratch_shapes=[pltpu.VMEM((KV_GROUP * TQ, LANES), BF16)] + scratch,
        compiler_params=_cparams(("parallel", "parallel", "parallel")),
        name=name,
    )(q3, k3, vt)


def _attention_shifted(q3, k3, vt):
    return _attention_call(_attn_shifted_body, "attention_shifted",
                           [pltpu.VMEM((LANES, KV_GROUP * TQ), F32)], q3, k3, vt)


def _attention(q3, k3, vt):
    return _attention_call(_attn_body, "attention",
                           [pltpu.VMEM((KV_GROUP * TQ, LANES), F32),
                            pltpu.VMEM((KV_GROUP * TQ, LANES), F32)], q3, k3, vt)


def _outproj_body(x_ref, cn_ref, ao_ref, ag_ref, wc_ref, wa_ref, g2_ref, wr_ref,
                  h_ref, xn_ref, aff_ref):
    for blk in range(TM_OUT // TM):
        rows = slice(blk * TM, (blk + 1) * TM)
        ao = ao_ref[rows, :].astype(F32)
        an = ao * lax.rsqrt(jnp.mean(ao * ao, axis=-1, keepdims=True) + EPS) * ag_ref[...]
        h = (x_ref[rows, :]
             + jnp.dot(cn_ref[rows, :], wc_ref[...], preferred_element_type=F32)
             + jnp.dot(an.astype(BF16), wa_ref[...], preferred_element_type=F32))
        h_ref[rows, :] = h
        xn = h * lax.rsqrt(jnp.mean(h * h, axis=-1, keepdims=True) + EPS) * g2_ref[...]
        xn_hi = xn.astype(BF16)
        xn_ref[rows, :] = xn_hi
        xn_lo = (xn - xn_hi.astype(F32)).astype(BF16)
        parts = (jnp.dot(xn_hi, wr_ref[...], preferred_element_type=F32)
                 + jnp.dot(xn_lo, wr_ref[...], preferred_element_type=F32))
        parts_t = parts.T
        logits = parts_t[0:N_EXPERTS, :] + parts_t[N_EXPERTS:2 * N_EXPERTS, :]
        mx = jnp.max(logits, axis=0, keepdims=True)
        ex = jnp.exp(logits - mx)
        aff_ref[:, rows] = ex / jnp.sum(ex, axis=0, keepdims=True)


def _outproj(x2, cn, ao, ag, wc, wa, g2, wr):
    n = x2.shape[0]
    row = lambda i: (i, 0)
    const = lambda i: (0, 0)
    return pl.pallas_call(
        _outproj_body,
        grid=(n // TM_OUT,),
        in_specs=[
            pl.BlockSpec((TM_OUT, D_MODEL), row),
            pl.BlockSpec((TM_OUT, D_CONV), row),
            pl.BlockSpec((TM_OUT, D_Q), row),
            pl.BlockSpec((1, D_Q), const),
            pl.BlockSpec((D_CONV, D_MODEL), const),
            pl.BlockSpec((D_Q, D_MODEL), const),
            pl.BlockSpec((1, D_MODEL), const),
            pl.BlockSpec((D_MODEL, LANES), const),
        ],
        out_specs=[
            pl.BlockSpec((TM_OUT, D_MODEL), row),
            pl.BlockSpec((TM_OUT, D_MODEL), row),
            pl.BlockSpec((N_EXPERTS, TM_OUT), lambda i: (0, i)),
        ],
        out_shape=[
            jax.ShapeDtypeStruct((n, D_MODEL), F32),
            jax.ShapeDtypeStruct((n, D_MODEL), BF16),
            jax.ShapeDtypeStruct((N_EXPERTS, n), F32),
        ],
        compiler_params=_cparams(("parallel",)),
        name="outproj",
    )(x2, cn, ao, ag, wc, wa, g2, wr)


def _threshold_body(cap, aff_ref, thr_ref, need_ref):
    def step(it, lo):
        cand = lo | (jnp.int32(1) << (30 - it))
        bits = pltpu.bitcast(aff_ref[...], jnp.int32)
        cnt = jnp.sum((bits >= cand).astype(jnp.int32), axis=1, keepdims=True)
        return jnp.where(cnt >= cap, cand, lo)

    thr = lax.fori_loop(0, 31, step, jnp.zeros((N_EXPERTS, 1), jnp.int32))
    bits = pltpu.bitcast(aff_ref[...], jnp.int32)
    n_gt = jnp.sum((bits > thr).astype(jnp.int32), axis=1, keepdims=True)
    thr_ref[...] = jnp.broadcast_to(thr, thr_ref.shape)
    need_ref[...] = jnp.broadcast_to(cap - n_gt, need_ref.shape)


def _threshold(aff_t, cap):
    n = aff_t.shape[1]
    full = lambda: (0, 0)
    return pl.pallas_call(
        functools.partial(_threshold_body, cap),
        in_specs=[pl.BlockSpec((N_EXPERTS, n), full)],
        out_specs=[pl.BlockSpec((N_EXPERTS, LANES), full), pl.BlockSpec((N_EXPERTS, LANES), full)],
        out_shape=[jax.ShapeDtypeStruct((N_EXPERTS, LANES), jnp.int32),
                   jax.ShapeDtypeStruct((N_EXPERTS, LANES), jnp.int32)],
        compiler_params=pltpu.CompilerParams(vmem_limit_bytes=VMEM_LIMIT),
        name="threshold",
    )(aff_t)


def _mask_body(aff_ref, thr_ref, need_ref, tri_ref, gsel_ref, lpos_ref, cnt_ref, eqc_ref):
    @pl.when(pl.program_id(0) == 0)
    def _():
        eqc_ref[...] = jnp.zeros(eqc_ref.shape, F32)

    aff = aff_ref[...]
    bits = pltpu.bitcast(aff, jnp.int32)
    thr = thr_ref[:, 0:1]
    need = need_ref[:, 0:1].astype(F32)
    gt = bits > thr
    eq = bits == thr
    eq_f = jnp.where(eq, 1.0, 0.0)
    eq_rank = eqc_ref[:, 0:1] + jnp.dot(eq_f.astype(BF16), tri_ref[...], preferred_element_type=F32)
    sel = gt | (eq & (eq_rank < need))
    sel_f = jnp.where(sel, 1.0, 0.0)
    gsel_ref[...] = jnp.where(sel, aff, -1.0)
    lpos_ref[...] = jnp.dot(sel_f.astype(BF16), tri_ref[...], preferred_element_type=F32)
    cnt = jnp.sum(sel_f, axis=1, keepdims=True)
    cnt_ref[0] = jnp.broadcast_to(cnt, (N_EXPERTS, LANES)).astype(jnp.int32)
    eqc_ref[...] = eqc_ref[...] + jnp.sum(eq_f, axis=1, keepdims=True)


def _masks(aff_t, thr, need, tri):
    n = aff_t.shape[1]
    nt = n // TR
    const = lambda i: (0, 0)
    tile = lambda i: (0, i)
    return pl.pallas_call(
        _mask_body,
        grid=(nt,),
        in_specs=[
            pl.BlockSpec((N_EXPERTS, TR), tile),
            pl.BlockSpec((N_EXPERTS, LANES), const),
            pl.BlockSpec((N_EXPERTS, LANES), const),
            pl.BlockSpec((TR, TR), const),
        ],
        out_specs=[
            pl.BlockSpec((N_EXPERTS, TR), tile),
            pl.BlockSpec((N_EXPERTS, TR), tile),
            pl.BlockSpec((1, N_EXPERTS, LANES), lambda i: (i, 0, 0)),
        ],
        out_shape=[
            jax.ShapeDtypeStruct((N_EXPERTS, n), F32),
            jax.ShapeDtypeStruct((N_EXPERTS, n), F32),
            jax.ShapeDtypeStruct((nt, N_EXPERTS, LANES), jnp.int32),
        ],
        scratch_shapes=[pltpu.VMEM((N_EXPERTS, LANES), F32)],
        compiler_params=_cparams(("arbitrary",)),
        name="masks",
    )(aff_t, thr, need, tri)


def _onehot_rows(gsel_row, lpos_row, first_slot, value):
    slot = lax.broadcasted_iota(jnp.int32, (CH, TR), 0).astype(F32) + first_slot.astype(F32)
    return jnp.where((gsel_row >= 0.0) & (lpos_row == slot), value, 0.0).astype(BF16)


def _dispatch_body(cap, base_ref, shift_ref, nch_ref, keepc_ref, keepr_ref, xn_ref, gsel_ref, lpos_ref,
                   xe_ref, stage_ref, carry_ref, hot_ref, sem_ref, xsem_ref):
    n = pl.program_id(0)
    nt = pl.num_programs(0)

    def copy(e, slot, sem, chunk):
        row0 = pl.multiple_of(base_ref[e * nt + n] + chunk * CH, BF16_SUBLANES)
        return pltpu.make_async_copy(stage_ref.at[slot], xe_ref.at[e, pl.ds(row0, CH)], sem)

    @pl.when(n == 0)
    def _():
        carry_ref[...] = jnp.zeros(carry_ref.shape, BF16)
        stage_ref[N_EXPERTS] = jnp.zeros((CH, D_MODEL), BF16)
        rows_total = xe_ref.shape[1]
        n_fill = -(-(rows_total - cap) // CH)
        starts = [rows_total - (j + 1) * CH for j in range(n_fill)]
        fills = [pltpu.make_async_copy(stage_ref.at[N_EXPERTS], xe_ref.at[e, pl.ds(row0, CH)], xsem_ref.at[0])
                 for e in range(N_EXPERTS) for row0 in starts]
        for cp in fills:
            cp.start()
        for cp in fills:
            cp.wait()

    def rows_of(e, chunk):
        hot = _onehot_rows(gsel_ref[pl.ds(e, 1), :], lpos_ref[pl.ds(e, 1), :],
                           chunk * CH - shift_ref[e * nt + n], 1.0)
        return jnp.dot(hot, xn_ref[...], preferred_element_type=F32)

    def kept_group(e, slot):
        row0 = pl.multiple_of(keepr_ref[e * nt + n], BF16_SUBLANES)
        return stage_ref[slot, pl.ds(row0, BF16_SUBLANES), :]

    for e in range(N_EXPERTS):
        hot_ref[e * CH:(e + 1) * CH, :] = _onehot_rows(gsel_ref[e:e + 1, :], lpos_ref[e:e + 1, :],
                                                       -shift_ref[e * nt + n], 1.0)
    rows_all = jnp.dot(hot_ref[...], xn_ref[...], preferred_element_type=F32)
    for e in range(N_EXPERTS):
        rows = rows_all[e * CH:(e + 1) * CH]
        stage_ref[e, 0:BF16_SUBLANES, :] = (rows[0:BF16_SUBLANES] + carry_ref[e].astype(F32)).astype(BF16)
        stage_ref[e, BF16_SUBLANES:CH, :] = rows[BF16_SUBLANES:CH].astype(BF16)
        copy(e, e, sem_ref.at[e], 0).start()
        carry_ref[e] = jnp.where(keepc_ref[e * nt + n] == 0, kept_group(e, e), jnp.zeros((), BF16))
    for e in range(N_EXPERTS):
        copy(e, e, sem_ref.at[e], 0).wait()

    def per_expert(e, carry):
        def per_chunk(chunk, c):
            stage_ref[N_EXPERTS] = rows_of(e, chunk).astype(BF16)
            cp = copy(e, N_EXPERTS, xsem_ref.at[0], chunk)
            cp.start()

            @pl.when(keepc_ref[e * nt + n] == chunk)
            def _():
                carry_ref[e] = kept_group(e, N_EXPERTS)
            cp.wait()
            return c
        return lax.fori_loop(1, nch_ref[e * nt + n], per_chunk, carry)

    lax.fori_loop(0, N_EXPERTS, per_expert, 0)


def _dispatch(base, shift, nch, keepc, keepr, xn2, gsel, lpos, cap, cap_rows):
    n = xn2.shape[0]
    nt = n // TR
    grid_spec = pltpu.PrefetchScalarGridSpec(
        num_scalar_prefetch=5,
        grid=(nt,),
        in_specs=[
            pl.BlockSpec((TR, D_MODEL), lambda i, *_: (i, 0)),
            pl.BlockSpec((N_EXPERTS, TR), lambda i, *_: (0, i)),
            pl.BlockSpec((N_EXPERTS, TR), lambda i, *_: (0, i)),
        ],
        out_specs=pl.BlockSpec(memory_space=pl.ANY),
        scratch_shapes=[pltpu.VMEM((N_EXPERTS + 1, CH, D_MODEL), BF16),
                        pltpu.VMEM((N_EXPERTS, BF16_SUBLANES, D_MODEL), BF16),
                        pltpu.VMEM((N_EXPERTS * CH, TR), BF16),
                        pltpu.SemaphoreType.DMA((N_EXPERTS,)),
                        pltpu.SemaphoreType.DMA((1,))],
    )
    return pl.pallas_call(
        functools.partial(_dispatch_body, cap),
        grid_spec=grid_spec,
        out_shape=jax.ShapeDtypeStruct((N_EXPERTS, cap_rows, D_MODEL), BF16),
        compiler_params=_cparams(("arbitrary",)),
        name="dispatch",
    )(base, shift, nch, keepc, keepr, xn2, gsel, lpos)


def _ffn_body(ntile_ref, x_ref, wg_ref, wu_ref, wd_ref, y_ref):
    e = pl.program_id(0)
    i = pl.program_id(1)

    @pl.when(i < ntile_ref[e])
    def _():
        x = x_ref[0]
        g = jnp.dot(x, wg_ref[0], preferred_element_type=F32)
        u = jnp.dot(x, wu_ref[0], preferred_element_type=F32)
        hid = (g * _sigmoid(g) * u).astype(BF16)
        y_ref[0] = jnp.dot(hid, wd_ref[0], preferred_element_type=F32).astype(BF16)

    @pl.when(i >= ntile_ref[e])
    def _():
        y_ref[...] = jnp.zeros(y_ref.shape, BF16)


def _ffn(ntile, xe, wg, wu, wd):
    cap_rows = xe.shape[1]
    rows = lambda e, i, nt_ref: (e, jnp.minimum(i, nt_ref[e] - 1), 0)
    wmap = lambda e, i, nt_ref: (e, 0, 0)
    grid_spec = pltpu.PrefetchScalarGridSpec(
        num_scalar_prefetch=1,
        grid=(N_EXPERTS, cap_rows // TMF),
        in_specs=[
            pl.BlockSpec((1, TMF, D_MODEL), rows),
            pl.BlockSpec((1, D_MODEL, D_FF_EXPERT), wmap),
            pl.BlockSpec((1, D_MODEL, D_FF_EXPERT), wmap),
            pl.BlockSpec((1, D_FF_EXPERT, D_MODEL), wmap),
        ],
        out_specs=pl.BlockSpec((1, TMF, D_MODEL), lambda e, i, nt_ref: (e, i, 0)),
    )
    return pl.pallas_call(
        _ffn_body,
        grid_spec=grid_spec,
        out_shape=jax.ShapeDtypeStruct(xe.shape, BF16),
        compiler_params=_cparams(("arbitrary", "arbitrary")),
        name="expert_ffn",
    )(ntile, xe, wg, wu, wd)


def _combine_body(base_ref, shift_ref, nch_ref, h_ref, gsel_ref, lpos_ref, p_ref, wpp_ref, pg_ref,
                  wpg_ref, bpg_ref, ye_ref, y_ref, ybuf_ref, xbuf_ref, hot_ref, acc_ref, sem_ref, xsem_ref):
    n = pl.program_id(0)
    nt = pl.num_programs(0)
    tn = (((0,), (0,)), ((), ()))
    cur = n % 2

    def first_chunk(e, step, buf):
        row0 = pl.multiple_of(base_ref[e * nt + step], BF16_SUBLANES)
        return pltpu.make_async_copy(ye_ref.at[e, pl.ds(row0, CH)], ybuf_ref.at[buf, pl.ds(e * CH, CH)],
                                     sem_ref.at[buf, e])

    def later_chunk(e, chunk):
        row0 = pl.multiple_of(base_ref[e * nt + n] + chunk * CH, BF16_SUBLANES)
        return pltpu.make_async_copy(ye_ref.at[e, pl.ds(row0, CH)], xbuf_ref, xsem_ref.at[0])

    def gated_hot(e, chunk):
        g = gsel_ref[pl.ds(e, 1), :]
        return _onehot_rows(g, lpos_ref[pl.ds(e, 1), :], chunk * CH - shift_ref[e * nt + n], g)

    @pl.when(n == 0)
    def _():
        for e in range(N_EXPERTS):
            first_chunk(e, 0, 0).start()

    @pl.when(n + 1 < nt)
    def _():
        for e in range(N_EXPERTS):
            first_chunk(e, n + 1, 1 - cur).start()

    for e in range(N_EXPERTS):
        hot_ref[e * CH:(e + 1) * CH, :] = gated_hot(e, jnp.int32(0))
    for e in range(N_EXPERTS):
        first_chunk(e, n, cur).wait()
    acc_ref[...] = h_ref[...] + lax.dot_general(hot_ref[...], ybuf_ref[cur], tn, preferred_element_type=F32)

    def per_expert(e, carry):
        def per_chunk(chunk, c):
            cp = later_chunk(e, chunk)
            cp.start()
            cp.wait()
            acc_ref[...] += lax.dot_general(gated_hot(e, chunk), xbuf_ref[...], tn, preferred_element_type=F32)
            return c
        return lax.fori_loop(1, nch_ref[e * nt + n], per_chunk, carry)

    lax.fori_loop(0, N_EXPERTS, per_expert, 0)

    h2 = acc_ref[...]
    emb = jnp.dot(p_ref[...].astype(BF16), wpp_ref[...], preferred_element_type=F32)
    hn = h2 * lax.rsqrt(jnp.mean(h2 * h2, axis=-1, keepdims=True) + EPS) * pg_ref[...]
    gate = _sigmoid(jnp.dot(hn.astype(BF16), wpg_ref[...], preferred_element_type=F32) + bpg_ref[...])
    y_ref[...] = h2 + gate * emb


def _combine(base, shift, nch, h1, gsel, lpos, p2, wpp, pg, wpg, bpg, ye):
    n = h1.shape[0]
    nt = n // TR
    row = lambda i, *_: (i, 0)
    tile = lambda i, *_: (0, i)
    const = lambda i, *_: (0, 0)
    grid_spec = pltpu.PrefetchScalarGridSpec(
        num_scalar_prefetch=3,
        grid=(nt,),
        in_specs=[
            pl.BlockSpec((TR, D_MODEL), row),
            pl.BlockSpec((N_EXPERTS, TR), tile),
            pl.BlockSpec((N_EXPERTS, TR), tile),
            pl.BlockSpec((TR, D_PLE), row),
            pl.BlockSpec((D_PLE, D_MODEL), const),
            pl.BlockSpec((1, D_MODEL), const),
            pl.BlockSpec((D_MODEL, D_MODEL), const),
            pl.BlockSpec((1, D_MODEL), const),
            pl.BlockSpec(memory_space=pl.ANY),
        ],
        out_specs=pl.BlockSpec((TR, D_MODEL), row),
        scratch_shapes=[pltpu.VMEM((2, N_EXPERTS * CH, D_MODEL), BF16),
                        pltpu.VMEM((CH, D_MODEL), BF16),
                        pltpu.VMEM((N_EXPERTS * CH, TR), BF16),
                        pltpu.VMEM((TR, D_MODEL), F32),
                        pltpu.SemaphoreType.DMA((2, N_EXPERTS)),
                        pltpu.SemaphoreType.DMA((1,))],
    )
    return pl.pallas_call(
        _combine_body,
        grid_spec=grid_spec,
        out_shape=jax.ShapeDtypeStruct((n, D_MODEL), F32),
        compiler_params=_cparams(("arbitrary",)),
        name="combine",
    )(base, shift, nch, h1, gsel, lpos, p2, wpp, pg, wpg, bpg, ye)


def _rope_tables(t):
    rows = t // GRID_W
    row_idx = jnp.repeat(jnp.arange(rows, dtype=F32), GRID_W)
    col_idx = jnp.tile(jnp.arange(GRID_W, dtype=F32), rows)
    freqs = 1.0 / (ROPE_THETA ** (jnp.arange(0, HALF_ROT, 2, dtype=F32) / HALF_ROT))
    ang_r = row_idx[:, None] * freqs[None, :]
    ang_c = col_idx[:, None] * freqs[None, :]
    cr, sr, cc, sc = jnp.cos(ang_r), jnp.sin(ang_r), jnp.cos(ang_c), jnp.sin(ang_c)
    cos_h = jnp.concatenate([cr, cr, cc, cc], axis=-1)
    sin_h = jnp.concatenate([-sr, sr, -sc, sc], axis=-1)
    return jnp.tile(cos_h, (1, LANES // HEAD_DIM)), jnp.tile(sin_h, (1, LANES // HEAD_DIM))


def _prepare_weights(norm1_g, w_in, conv_dw_w, conv_dw_b, conv_ln_g, conv_ln_b, q_norm_g, k_norm_g,
                     conv_out_g, attn_out_g, w_out, norm2_g, w_router, w_gate, w_up, w_down,
                     ple_proj, ple_norm_g, ple_gate_w, ple_gate_b):
    i = 0
    q_scale = (HEAD_DIM ** -0.5) * math.log2(math.e)
    qkg = jnp.concatenate([jnp.tile(q_norm_g[i] * q_scale, N_HEADS), jnp.tile(k_norm_g[i], N_KV_HEADS)])
    bound = HEAD_DIM * jnp.max(jnp.abs(q_norm_g[i] * q_scale)) * jnp.max(jnp.abs(k_norm_g[i]))
    use_shift = bound <= MAX_SCORE_SHIFT
    lane = jnp.arange(LANES)
    fill = jnp.stack([jnp.where(lane == HEAD_DIM, 1.0, 0.0),
                      jnp.where(lane == HEAD_DIM, -jnp.where(use_shift, bound, 0.0), 0.0)]).astype(F32)
    ones_bd = (lane[:, None] // HEAD_DIM == lane[None, :] // HEAD_DIM).astype(BF16)
    tok = jnp.arange(TR)
    wr_hi = w_router[i].astype(BF16)
    wr_lo = (w_router[i] - wr_hi.astype(F32)).astype(BF16)
    wr = jnp.concatenate([wr_hi, wr_lo, jnp.zeros((D_MODEL, LANES - 2 * N_EXPERTS), BF16)], axis=1)
    return dict(
        g1=norm1_g[i][None, :], w_in=w_in[i].astype(BF16), qkg=qkg[None, :], ones_bd=ones_bd,
        fill=fill, use_shift=use_shift,
        dw_w=conv_dw_w[i], dw_b=conv_dw_b[i][None, :], ln_g=conv_ln_g[i][None, :], ln_b=conv_ln_b[i][None, :],
        conv_out_g=conv_out_g[i][None, :], attn_out_g=attn_out_g[i][None, :],
        w_out_c=w_out[i, :D_CONV].astype(BF16), w_out_a=w_out[i, D_CONV:].astype(BF16),
        g2=norm2_g[i][None, :], w_router=wr,
        wg=w_gate[i].astype(BF16), wu=w_up[i].astype(BF16), wd=w_down[i].astype(BF16),
        tri=(tok[:, None] < tok[None, :]).astype(BF16),
        wpp=ple_proj[i].astype(BF16), pg=ple_norm_g[i][None, :], wpg=ple_gate_w[i].astype(BF16),
        bpg=ple_gate_b[i][None, :],
    )


def _trunk(x, p, w):
    b, t, _ = x.shape
    n = b * t
    assert t % TM_OUT == 0 and t % TK == 0 and t % TT == 0 and t % TQ == 0 and n % TR == 0
    cap = CAPACITY_FACTOR * n // N_EXPERTS
    nt = n // TR
    x2 = x.reshape(n, D_MODEL)
    cos_t, sin_t = _rope_tables(t)

    gl, q, k, v = _inproj(x2, w["g1"], w["w_in"], w["qkg"], cos_t, sin_t, w["ones_bd"], w["fill"], t)
    cn = _conv(gl.reshape(b, t, D_CONV), w["dw_w"], w["dw_b"], w["ln_g"], w["ln_b"], w["conv_out_g"])
    ao = lax.cond(w["use_shift"], _attention_shifted, _attention,
                  q.reshape(b, t, -1), k.reshape(b, t, -1), v)
    h1, xn2, aff_t = _outproj(x2, cn.reshape(n, D_CONV), ao.reshape(n, D_Q), w["attn_out_g"],
                              w["w_out_c"], w["w_out_a"], w["g2"], w["w_router"])

    thr, need = _threshold(aff_t, cap)
    gsel, lpos, cnt = _masks(aff_t, thr, need, w["tri"])
    cnt = cnt[:, :, 0].T
    ends = jnp.cumsum(cnt, axis=1)
    start = ends - cnt
    base = start // BF16_SUBLANES * BF16_SUBLANES
    shift = start - base
    nch = jnp.maximum((shift + cnt + CH - 1) // CH, 1)
    keep = ends // BF16_SUBLANES * BF16_SUBLANES - base
    flat = lambda a: a.astype(jnp.int32).reshape(-1)
    base, shift, nch, keepc, keepr = flat(base), flat(shift), flat(nch), flat(keep // CH), flat(keep % CH)
    full_tiles = (cap + TMF - 1) // TMF
    cap_rows = (full_tiles + 1) * TMF
    ntile = jnp.full((N_EXPERTS,), full_tiles, jnp.int32)

    xe = _dispatch(base, shift, nch, keepc, keepr, xn2, gsel, lpos, cap, cap_rows)
    ye = _ffn(ntile, xe, w["wg"], w["wu"], w["wd"])
    y = _combine(base, shift, nch, h1, gsel, lpos, p.reshape(n, D_PLE), w["wpp"], w["pg"], w["wpg"],
                 w["bpg"], ye)
    return y.reshape(b, t, D_MODEL)


def kernel(x_prompt, x_sample, p_prompt, p_sample, norm1_g, w_in, conv_dw_w, conv_dw_b, conv_ln_g, conv_ln_b, q_norm_g, k_norm_g, conv_out_g, attn_out_g, w_out, norm2_g, w_router, w_gate, w_up, w_down, ple_proj, ple_norm_g, ple_gate_w, ple_gate_b):
    w = _prepare_weights(norm1_g, w_in, conv_dw_w, conv_dw_b, conv_ln_g, conv_ln_b, q_norm_g, k_norm_g,
                         conv_out_g, attn_out_g, w_out, norm2_g, w_router, w_gate, w_up, w_down,
                         ple_proj, ple_norm_g, ple_gate_w, ple_gate_b)
    y_prompt = _trunk(x_prompt, p_prompt[0], w)
    y_sample = _trunk(x_sample, p_sample[0], w)
    return (y_prompt, y_sample)
```

```python
import functools
import math

import jax
import jax.numpy as jnp
from jax import lax
from jax.experimental import pallas as pl
from jax.experimental.pallas import tpu as pltpu

D_MODEL = 1024
D_CONV = 512
CONV_WIDTH = 31
CONV_PAD = CONV_WIDTH // 2
N_HEADS = 8
N_KV_HEADS = 2
HEAD_DIM = 64
KV_GROUP = N_HEADS // N_KV_HEADS
D_Q = N_HEADS * HEAD_DIM
D_KV = N_KV_HEADS * HEAD_DIM
D_IN = 2 * D_CONV + D_Q + 2 * D_KV
HALF_ROT = HEAD_DIM // 2
ROPE_THETA = 10000.0
GRID_W = 64
N_EXPERTS = 16
CAPACITY_FACTOR = 2
D_FF_EXPERT = 2048
D_PLE = 256
EPS = 1e-6

LANES = 128
F32_SUBLANES = 8
BF16_SUBLANES = 16
VMEM_LIMIT = 56 * 1024 * 1024

TM = 512
TM_OUT = 1024
TT = 256
HALO = 16
TQ = 256
TK = 512
KV_CHUNKS = 8
MAX_SCORE_SHIFT = 40.0
TR = 512
CH = 128
TMF = 512

F32 = jnp.float32
BF16 = jnp.bfloat16


def _cparams(sem):
    return pltpu.CompilerParams(dimension_semantics=sem, vmem_limit_bytes=VMEM_LIMIT)


def _sigmoid(x):
    return 1.0 / (1.0 + jnp.exp(-x))


def _inproj_body(x_ref, g1_ref, w_ref, qkg_ref, cos_ref, sin_ref, ones_ref, fill_ref,
                 gl_ref, q_ref, k_ref, v_ref):
    for blk in range(TM_OUT // TM):
        _inproj_rows(slice(blk * TM, (blk + 1) * TM), x_ref, g1_ref, w_ref, qkg_ref, cos_ref, sin_ref,
                     ones_ref, fill_ref, gl_ref, q_ref, k_ref, v_ref)


def _inproj_rows(rows, x_ref, g1_ref, w_ref, qkg_ref, cos_ref, sin_ref, ones_ref, fill_ref,
                 gl_ref, q_ref, k_ref, v_ref):
    x = x_ref[rows, :]
    a = x * lax.rsqrt(jnp.mean(x * x, axis=-1, keepdims=True) + EPS) * g1_ref[...]
    z = jnp.dot(a.astype(BF16), w_ref[...], preferred_element_type=F32)
    val = z[:, :D_CONV]
    gate = z[:, D_CONV:2 * D_CONV]
    gl_ref[rows, :] = (val * _sigmoid(gate)).astype(BF16)

    lane = lax.broadcasted_iota(jnp.int32, (x.shape[0], LANES), 1)
    first_half = (lane % HALF_ROT) < (HALF_ROT // 2)
    low_head = lane < HEAD_DIM
    cos = cos_ref[rows, :]
    sin = sin_ref[rows, :]
    o0 = 2 * D_CONV
    n_chunks = (D_Q + D_KV) // LANES
    for c in range(n_chunks):
        qc = z[:, o0 + c * LANES:o0 + (c + 1) * LANES]
        ssum = jnp.dot((qc * qc).astype(BF16), ones_ref[...], preferred_element_type=F32)
        qn = qc * lax.rsqrt(ssum * (1.0 / HEAD_DIM) + EPS) * qkg_ref[:, c * LANES:(c + 1) * LANES]
        partner = jnp.where(first_half, pltpu.roll(qn, LANES - HALF_ROT // 2, 1),
                            pltpu.roll(qn, HALF_ROT // 2, 1))
        qr = qn * cos + partner * sin
        fill = fill_ref[0:1, :] if c < D_Q // LANES else fill_ref[1:2, :]
        even = jnp.where(low_head, qr, fill).astype(BF16)
        odd = jnp.where(low_head, pltpu.roll(qr, HEAD_DIM, 1), fill).astype(BF16)
        if c < D_Q // LANES:
            q_ref[rows, (2 * c) * LANES:(2 * c + 1) * LANES] = even
            q_ref[rows, (2 * c + 1) * LANES:(2 * c + 2) * LANES] = odd
        else:
            k_ref[rows, 0:LANES] = even
            k_ref[rows, LANES:2 * LANES] = odd
    vv = z[:, o0 + D_Q + D_KV:]
    v_ref[0:LANES, rows] = jnp.where(low_head, vv, 1.0).T.astype(BF16)
    v_ref[LANES:2 * LANES, rows] = jnp.where(low_head, pltpu.roll(vv, HEAD_DIM, 1), 1.0).T.astype(BF16)


def _inproj(x2, g1, w_in_bf, qkg, cos_t, sin_t, ones_bd, fill, seq_len):
    n = x2.shape[0]
    pos_tiles = seq_len // TM_OUT
    row = lambda i: (i, 0)
    const = lambda i: (0, 0)
    return pl.pallas_call(
        _inproj_body,
        grid=(n // TM_OUT,),
        in_specs=[
            pl.BlockSpec((TM_OUT, D_MODEL), row),
            pl.BlockSpec((1, D_MODEL), const),
            pl.BlockSpec((D_MODEL, D_IN), const),
            pl.BlockSpec((1, D_Q + D_KV), const),
            pl.BlockSpec((TM_OUT, LANES), lambda i: (i % pos_tiles, 0)),
            pl.BlockSpec((TM_OUT, LANES), lambda i: (i % pos_tiles, 0)),
            pl.BlockSpec((LANES, LANES), const),
            pl.BlockSpec((2, LANES), const),
        ],
        out_specs=[
            pl.BlockSpec((TM_OUT, D_CONV), row),
            pl.BlockSpec((TM_OUT, N_HEADS * LANES), row),
            pl.BlockSpec((TM_OUT, N_KV_HEADS * LANES), row),
            pl.BlockSpec((N_KV_HEADS * LANES, TM_OUT), lambda i: (0, i)),
        ],
        out_shape=[
            jax.ShapeDtypeStruct((n, D_CONV), BF16),
            jax.ShapeDtypeStruct((n, N_HEADS * LANES), BF16),
            jax.ShapeDtypeStruct((n, N_KV_HEADS * LANES), BF16),
            jax.ShapeDtypeStruct((N_KV_HEADS * LANES, n), BF16),
        ],
        compiler_params=_cparams(("parallel",)),
        name="inproj",
    )(x2, g1, w_in_bf, qkg, cos_t, sin_t, ones_bd, fill)


CONV_ROWS = 64
CONV_SPAN = TT + 2 * HALO - F32_SUBLANES


def _conv_body(left_ref, main_ref, right_ref, w_ref, b_ref, lng_ref, lnb_ref, og_ref,
               out_ref, win_ref, conv_ref):
    i = pl.program_id(1)
    last = pl.num_programs(1) - 1
    left = left_ref[0].astype(F32)
    right = right_ref[0].astype(F32)
    win_ref[0, 0:HALO, :] = jnp.where(i > 0, left, 0.0)
    win_ref[0, HALO:HALO + TT, :] = main_ref[0].astype(F32)
    win_ref[0, HALO + TT:HALO + TT + HALO, :] = jnp.where(i < last, right, 0.0)
    for r in range(1, F32_SUBLANES):
        win_ref[r, 0:CONV_SPAN, :] = win_ref[0, r:r + CONV_SPAN, :]
    base = HALO - CONV_PAD

    def row_block(rr, carry):
        r0 = pl.multiple_of(rr * CONV_ROWS, CONV_ROWS)
        groups = CONV_ROWS // F32_SUBLANES
        reach = (base + CONV_WIDTH - 1) // F32_SUBLANES + 1
        for c in range(D_CONV // LANES):
            ls = slice(c * LANES, (c + 1) * LANES)
            accs = [None] * groups
            for shift in range(F32_SUBLANES):
                for q in range(groups + reach - 1):
                    uses = [(q - a, a * F32_SUBLANES + shift - base) for a in range(reach)
                            if 0 <= a * F32_SUBLANES + shift - base < CONV_WIDTH and 0 <= q - a < groups]
                    if not uses:
                        continue
                    tile = win_ref[shift, pl.ds(r0 + q * F32_SUBLANES, F32_SUBLANES), ls]
                    for g, k in uses:
                        term = tile * w_ref[k:k + 1, ls]
                        accs[g] = term if accs[g] is None else accs[g] + term
            for g in range(groups):
                conv_ref[pl.ds(r0 + g * F32_SUBLANES, F32_SUBLANES), ls] = accs[g]
        return carry

    lax.fori_loop(0, TT // CONV_ROWS, row_block, 0)
    cv = conv_ref[...] + b_ref[...]
    mu = jnp.mean(cv, axis=-1, keepdims=True)
    d = cv - mu
    var = jnp.mean(d * d, axis=-1, keepdims=True)
    y = d * lax.rsqrt(var + EPS) * lng_ref[...] + lnb_ref[...]
    y = y * _sigmoid(y)
    y = y * lax.rsqrt(jnp.mean(y * y, axis=-1, keepdims=True) + EPS) * og_ref[...]
    out_ref[0] = y.astype(BF16)


def _conv(gl3, dw_w, dw_b, ln_g, ln_b, out_g):
    b, t, _ = gl3.shape
    hb = TT // HALO
    n_halo = t // HALO
    const = lambda bb, i: (0, 0)
    return pl.pallas_call(
        _conv_body,
        grid=(b, t // TT),
        in_specs=[
            pl.BlockSpec((1, HALO, D_CONV), lambda bb, i: (bb, jnp.maximum(i * hb - 1, 0), 0)),
            pl.BlockSpec((1, TT, D_CONV), lambda bb, i: (bb, i, 0)),
            pl.BlockSpec((1, HALO, D_CONV), lambda bb, i: (bb, jnp.minimum((i + 1) * hb, n_halo - 1), 0)),
            pl.BlockSpec((CONV_WIDTH, D_CONV), const),
            pl.BlockSpec((1, D_CONV), const),
            pl.BlockSpec((1, D_CONV), const),
            pl.BlockSpec((1, D_CONV), const),
            pl.BlockSpec((1, D_CONV), const),
        ],
        out_specs=pl.BlockSpec((1, TT, D_CONV), lambda bb, i: (bb, i, 0)),
        out_shape=jax.ShapeDtypeStruct((b, t, D_CONV), BF16),
        scratch_shapes=[pltpu.VMEM((F32_SUBLANES, TT + 2 * HALO, D_CONV), F32),
                        pltpu.VMEM((TT, D_CONV), F32)],
        compiler_params=_cparams(("parallel", "parallel")),
        name="conv",
    )(gl3, gl3, gl3, dw_w, dw_b, ln_g, ln_b, out_g)


def _attn_body(q_ref, k_ref, vt_ref, o_ref, qs_ref, m_ref, acc_ref):
    t = k_ref.shape[1]
    nt_dims = (((1,), (1,)), ((), ()))
    for h in range(KV_GROUP):
        qs_ref[h * TQ:(h + 1) * TQ, :] = q_ref[0, :, h * LANES:(h + 1) * LANES]
    m_ref[...] = jnp.full(m_ref.shape, -jnp.inf, F32)
    acc_ref[...] = jnp.zeros(acc_ref.shape, F32)

    def step(kt, carry):
        start = pl.multiple_of(kt * TK, TK)
        s = lax.dot_general(qs_ref[...], k_ref[0, pl.ds(start, TK), :], nt_dims,
                            preferred_element_type=F32)
        m_old = m_ref[...]
        m_new = jnp.maximum(m_old, jnp.max(s, axis=-1, keepdims=True))
        p = jnp.exp2(s - m_new[:, 0:1])
        alpha = jnp.exp2(m_old - m_new)
        acc_ref[...] = alpha * acc_ref[...] + lax.dot_general(
            p.astype(BF16), vt_ref[:, pl.ds(start, TK)], nt_dims, preferred_element_type=F32)
        m_ref[...] = m_new
        return carry

    lax.fori_loop(0, t // TK, step, 0)
    acc = acc_ref[...]
    res = acc / pltpu.roll(acc, HEAD_DIM, 1)
    _attn_store(o_ref, [res[h * TQ:(h + 1) * TQ, :] for h in range(KV_GROUP)])


def _attn_shifted_body(q_ref, k_ref, vt_ref, o_ref, qs_ref, acc_ref):
    t = k_ref.shape[1]
    for h in range(KV_GROUP):
        qs_ref[h * TQ:(h + 1) * TQ, :] = q_ref[0, :, h * LANES:(h + 1) * LANES]
    acc_ref[...] = jnp.zeros(acc_ref.shape, F32)

    chunks = math.gcd(KV_CHUNKS, t // TK)

    def step(kt, carry):
        qs = qs_ref[...]
        part = None
        for c in range(chunks):
            start = pl.multiple_of(kt * (chunks * TK) + c * TK, TK)
            st = lax.dot_general(k_ref[0, pl.ds(start, TK), :], qs, (((1,), (1,)), ((), ())),
                                 preferred_element_type=F32)
            pv = jnp.dot(vt_ref[:, pl.ds(start, TK)], jnp.exp2(st).astype(BF16),
                         preferred_element_type=F32)
            part = pv if part is None else part + pv
        acc_ref[...] += part
        return carry

    lax.fori_loop(0, t // (chunks * TK), step, 0)
    acc = acc_ref[...]
    res = acc * (1.0 / acc[HEAD_DIM:HEAD_DIM + 1, :])
    _attn_store(o_ref, [res[:, h * TQ:(h + 1) * TQ].T for h in range(KV_GROUP)])


def _attn_store(o_ref, heads):
    lane = lax.broadcasted_iota(jnp.int32, (TQ, LANES), 1)
    low = lane < HEAD_DIM
    for hp in range(KV_GROUP // 2):
        o_ref[0, :, hp * LANES:(hp + 1) * LANES] = jnp.where(
            low, heads[2 * hp], pltpu.roll(heads[2 * hp + 1], HEAD_DIM, 1)).astype(BF16)


def _attention_call(body, name, scratch, q3, k3, vt):
    b, t, _ = q3.shape
    gw = KV_GROUP * LANES
    return pl.pallas_call(
        body,
        grid=(b, N_KV_HEADS, t // TQ),
        in_specs=[
            pl.BlockSpec((1, TQ, gw), lambda bb, j, i: (bb, i, j)),
            pl.BlockSpec((1, t, LANES), lambda bb, j, i: (bb, 0, j)),
            pl.BlockSpec((LANES, t), lambda bb, j, i: (j, bb)),
        ],
        out_specs=pl.BlockSpec((1, TQ, KV_GROUP * HEAD_DIM), lambda bb, j, i: (bb, i, j)),
        out_shape=jax.ShapeDtypeStruct((b, t, D_Q), BF16),
        scratch_shapes=[pltpu.VMEM((KV_GROUP * TQ, LANES), BF16)] + scratch,
        compiler_params=_cparams(("parallel", "parallel", "parallel")),
        name=name,
    )(q3, k3, vt)


def _attention_shifted(q3, k3, vt):
    return _attention_call(_attn_shifted_body, "attention_shifted",
                           [pltpu.VMEM((LANES, KV_GROUP * TQ), F32)], q3, k3, vt)


def _attention(q3, k3, vt):
    return _attention_call(_attn_body, "attention",
                           [pltpu.VMEM((KV_GROUP * TQ, LANES), F32),
                            pltpu.VMEM((KV_GROUP * TQ, LANES), F32)], q3, k3, vt)


def _outproj_body(x_ref, cn_ref, ao_ref, ag_ref, wc_ref, wa_ref, g2_ref, wr_ref,
                  h_ref, xn_ref, aff_ref):
    for blk in range(TM_OUT // TM):
        rows = slice(blk * TM, (blk + 1) * TM)
        ao = ao_ref[rows, :].astype(F32)
        an = ao * lax.rsqrt(jnp.mean(ao * ao, axis=-1, keepdims=True) + EPS) * ag_ref[...]
        h = (x_ref[rows, :]
             + jnp.dot(cn_ref[rows, :], wc_ref[...], preferred_element_type=F32)
             + jnp.dot(an.astype(BF16), wa_ref[...], preferred_element_type=F32))
        h_ref[rows, :] = h
        xn = h * lax.rsqrt(jnp.mean(h * h, axis=-1, keepdims=True) + EPS) * g2_ref[...]
        xn_hi = xn.astype(BF16)
        xn_ref[rows, :] = xn_hi
        xn_lo = (xn - xn_hi.astype(F32)).astype(BF16)
        parts = (jnp.dot(xn_hi, wr_ref[...], preferred_element_type=F32)
                 + jnp.dot(xn_lo, wr_ref[...], preferred_element_type=F32))
        parts_t = parts.T
        logits = parts_t[0:N_EXPERTS, :] + parts_t[N_EXPERTS:2 * N_EXPERTS, :]
        mx = jnp.max(logits, axis=0, keepdims=True)
        ex = jnp.exp(logits - mx)
        aff_ref[:, rows] = ex / jnp.sum(ex, axis=0, keepdims=True)


def _outproj(x2, cn, ao, ag, wc, wa, g2, wr):
    n = x2.shape[0]
    row = lambda i: (i, 0)
    const = lambda i: (0, 0)
    return pl.pallas_call(
        _outproj_body,
        grid=(n // TM_OUT,),
        in_specs=[
            pl.BlockSpec((TM_OUT, D_MODEL), row),
            pl.BlockSpec((TM_OUT, D_CONV), row),
            pl.BlockSpec((TM_OUT, D_Q), row),
            pl.BlockSpec((1, D_Q), const),
            pl.BlockSpec((D_CONV, D_MODEL), const),
            pl.BlockSpec((D_Q, D_MODEL), const),
            pl.BlockSpec((1, D_MODEL), const),
            pl.BlockSpec((D_MODEL, LANES), const),
        ],
        out_specs=[
            pl.BlockSpec((TM_OUT, D_MODEL), row),
            pl.BlockSpec((TM_OUT, D_MODEL), row),
            pl.BlockSpec((N_EXPERTS, TM_OUT), lambda i: (0, i)),
        ],
        out_shape=[
            jax.ShapeDtypeStruct((n, D_MODEL), F32),
            jax.ShapeDtypeStruct((n, D_MODEL), BF16),
            jax.ShapeDtypeStruct((N_EXPERTS, n), F32),
        ],
        compiler_params=_cparams(("parallel",)),
        name="outproj",
    )(x2, cn, ao, ag, wc, wa, g2, wr)


def _threshold_body(cap, aff_ref, thr_ref, need_ref):
    def step(it, lo):
        cand = lo | (jnp.int32(1) << (30 - it))
        bits = pltpu.bitcast(aff_ref[...], jnp.int32)
        cnt = jnp.sum((bits >= cand).astype(jnp.int32), axis=1, keepdims=True)
        return jnp.where(cnt >= cap, cand, lo)

    thr = lax.fori_loop(0, 31, step, jnp.zeros((N_EXPERTS, 1), jnp.int32))
    bits = pltpu.bitcast(aff_ref[...], jnp.int32)
    n_gt = jnp.sum((bits > thr).astype(jnp.int32), axis=1, keepdims=True)
    thr_ref[...] = jnp.broadcast_to(thr, thr_ref.shape)
    need_ref[...] = jnp.broadcast_to(cap - n_gt, need_ref.shape)


def _threshold(aff_t, cap):
    n = aff_t.shape[1]
    full = lambda: (0, 0)
    return pl.pallas_call(
        functools.partial(_threshold_body, cap),
        in_specs=[pl.BlockSpec((N_EXPERTS, n), full)],
        out_specs=[pl.BlockSpec((N_EXPERTS, LANES), full), pl.BlockSpec((N_EXPERTS, LANES), full)],
        out_shape=[jax.ShapeDtypeStruct((N_EXPERTS, LANES), jnp.int32),
                   jax.ShapeDtypeStruct((N_EXPERTS, LANES), jnp.int32)],
        compiler_params=pltpu.CompilerParams(vmem_limit_bytes=VMEM_LIMIT),
        name="threshold",
    )(aff_t)


def _mask_body(aff_ref, thr_ref, need_ref, tri_ref, gsel_ref, lpos_ref, cnt_ref, eqc_ref):
    @pl.when(pl.program_id(0) == 0)
    def _():
        eqc_ref[...] = jnp.zeros(eqc_ref.shape, F32)

    aff = aff_ref[...]
    bits = pltpu.bitcast(aff, jnp.int32)
    thr = thr_ref[:, 0:1]
    need = need_ref[:, 0:1].astype(F32)
    gt = bits > thr
    eq = bits == thr
    eq_f = jnp.where(eq, 1.0, 0.0)
    eq_rank = eqc_ref[:, 0:1] + jnp.dot(eq_f.astype(BF16), tri_ref[...], preferred_element_type=F32)
    sel = gt | (eq & (eq_rank < need))
    sel_f = jnp.where(sel, 1.0, 0.0)
    gsel_ref[...] = jnp.where(sel, aff, -1.0)
    lpos_ref[...] = jnp.dot(sel_f.astype(BF16), tri_ref[...], preferred_element_type=F32)
    cnt = jnp.sum(sel_f, axis=1, keepdims=True)
    cnt_ref[0] = jnp.broadcast_to(cnt, (N_EXPERTS, LANES)).astype(jnp.int32)
    eqc_ref[...] = eqc_ref[...] + jnp.sum(eq_f, axis=1, keepdims=True)


def _masks(aff_t, thr, need, tri):
    n = aff_t.shape[1]
    nt = n // TR
    const = lambda i: (0, 0)
    tile = lambda i: (0, i)
    return pl.pallas_call(
        _mask_body,
        grid=(nt,),
        in_specs=[
            pl.BlockSpec((N_EXPERTS, TR), tile),
            pl.BlockSpec((N_EXPERTS, LANES), const),
            pl.BlockSpec((N_EXPERTS, LANES), const),
            pl.BlockSpec((TR, TR), const),
        ],
        out_specs=[
            pl.BlockSpec((N_EXPERTS, TR), tile),
            pl.BlockSpec((N_EXPERTS, TR), tile),
            pl.BlockSpec((1, N_EXPERTS, LANES), lambda i: (i, 0, 0)),
        ],
        out_shape=[
            jax.ShapeDtypeStruct((N_EXPERTS, n), F32),
            jax.ShapeDtypeStruct((N_EXPERTS, n), F32),
            jax.ShapeDtypeStruct((nt, N_EXPERTS, LANES), jnp.int32),
        ],
        scratch_shapes=[pltpu.VMEM((N_EXPERTS, LANES), F32)],
        compiler_params=_cparams(("arbitrary",)),
        name="masks",
    )(aff_t, thr, need, tri)


def _onehot_rows(gsel_row, lpos_row, first_slot, value):
    slot = lax.broadcasted_iota(jnp.int32, (CH, TR), 0).astype(F32) + first_slot.astype(F32)
    return jnp.where((gsel_row >= 0.0) & (lpos_row == slot), value, 0.0).astype(BF16)


def _dispatch_body(cap, base_ref, shift_ref, nch_ref, keepc_ref, keepr_ref, xn_ref, gsel_ref, lpos_ref,
                   xe_ref, stage_ref, carry_ref, hot_ref, sem_ref, xsem_ref):
    n = pl.program_id(0)
    nt = pl.num_programs(0)

    def copy(e, slot, sem, chunk):
        row0 = pl.multiple_of(base_ref[e * nt + n] + chunk * CH, BF16_SUBLANES)
        return pltpu.make_async_copy(stage_ref.at[slot], xe_ref.at[e, pl.ds(row0, CH)], sem)

    @pl.when(n == 0)
    def _():
        carry_ref[...] = jnp.zeros(carry_ref.shape, BF16)
        stage_ref[N_EXPERTS] = jnp.zeros((CH, D_MODEL), BF16)
        rows_total = xe_ref.shape[1]
        n_fill = -(-(rows_total - cap) // CH)
        starts = [rows_total - (j + 1) * CH for j in range(n_fill)]
        fills = [pltpu.make_async_copy(stage_ref.at[N_EXPERTS], xe_ref.at[e, pl.ds(row0, CH)], xsem_ref.at[0])
                 for e in range(N_EXPERTS) for row0 in starts]
        for cp in fills:
            cp.start()
        for cp in fills:
            cp.wait()

    def rows_of(e, chunk):
        hot = _onehot_rows(gsel_ref[pl.ds(e, 1), :], lpos_ref[pl.ds(e, 1), :],
                           chunk * CH - shift_ref[e * nt + n], 1.0)
        return jnp.dot(hot, xn_ref[...], preferred_element_type=F32)

    def kept_group(e, slot):
        row0 = pl.multiple_of(keepr_ref[e * nt + n], BF16_SUBLANES)
        return stage_ref[slot, pl.ds(row0, BF16_SUBLANES), :]

    for e in range(N_EXPERTS):
        hot_ref[e * CH:(e + 1) * CH, :] = _onehot_rows(gsel_ref[e:e + 1, :], lpos_ref[e:e + 1, :],
                                                       -shift_ref[e * nt + n], 1.0)
    rows_all = jnp.dot(hot_ref[...], xn_ref[...], preferred_element_type=F32)

    @pl.when(n > 0)
    def _():
        for e in range(N_EXPERTS):
            copy(e, e, sem_ref.at[e], 0).wait()

    for e in range(N_EXPERTS):
        rows = rows_all[e * CH:(e + 1) * CH]
        stage_ref[e, 0:BF16_SUBLANES, :] = (rows[0:BF16_SUBLANES] + carry_ref[e].astype(F32)).astype(BF16)
        stage_ref[e, BF16_SUBLANES:CH, :] = rows[BF16_SUBLANES:CH].astype(BF16)
        copy(e, e, sem_ref.at[e], 0).start()
        carry_ref[e] = jnp.where(keepc_ref[e * nt + n] == 0, kept_group(e, e), jnp.zeros((), BF16))

    @pl.when(n == nt - 1)
    def _():
        for e in range(N_EXPERTS):
            copy(e, e, sem_ref.at[e], 0).wait()

    def per_expert(e, carry):
        def per_chunk(chunk, c):
            stage_ref[N_EXPERTS] = rows_of(e, chunk).astype(BF16)
            cp = copy(e, N_EXPERTS, xsem_ref.at[0], chunk)
            cp.start()

            @pl.when(keepc_ref[e * nt + n] == chunk)
            def _():
                carry_ref[e] = kept_group(e, N_EXPERTS)
            cp.wait()
            return c
        return lax.fori_loop(1, nch_ref[e * nt + n], per_chunk, carry)

    lax.fori_loop(0, N_EXPERTS, per_expert, 0)


def _dispatch(base, shift, nch, keepc, keepr, xn2, gsel, lpos, cap, cap_rows):
    n = xn2.shape[0]
    nt = n // TR
    grid_spec = pltpu.PrefetchScalarGridSpec(
        num_scalar_prefetch=5,
        grid=(nt,),
        in_specs=[
            pl.BlockSpec((TR, D_MODEL), lambda i, *_: (i, 0)),
            pl.BlockSpec((N_EXPERTS, TR), lambda i, *_: (0, i)),
            pl.BlockSpec((N_EXPERTS, TR), lambda i, *_: (0, i)),
        ],
        out_specs=pl.BlockSpec(memory_space=pl.ANY),
        scratch_shapes=[pltpu.VMEM((N_EXPERTS + 1, CH, D_MODEL), BF16),
                        pltpu.VMEM((N_EXPERTS, BF16_SUBLANES, D_MODEL), BF16),
                        pltpu.VMEM((N_EXPERTS * CH, TR), BF16),
                        pltpu.SemaphoreType.DMA((N_EXPERTS,)),
                        pltpu.SemaphoreType.DMA((1,))],
    )
    return pl.pallas_call(
        functools.partial(_dispatch_body, cap),
        grid_spec=grid_spec,
        out_shape=jax.ShapeDtypeStruct((N_EXPERTS, cap_rows, D_MODEL), BF16),
        compiler_params=_cparams(("arbitrary",)),
        name="dispatch",
    )(base, shift, nch, keepc, keepr, xn2, gsel, lpos)


def _ffn_body(ntile_ref, x_ref, wg_ref, wu_ref, wd_ref, y_ref):
    e = pl.program_id(0)
    i = pl.program_id(1)

    @pl.when(i < ntile_ref[e])
    def _():
        x = x_ref[0]
        g = jnp.dot(x, wg_ref[0], preferred_element_type=F32)
        u = jnp.dot(x, wu_ref[0], preferred_element_type=F32)
        hid = (g * _sigmoid(g) * u).astype(BF16)
        y_ref[0] = jnp.dot(hid, wd_ref[0], preferred_element_type=F32).astype(BF16)

    @pl.when(i >= ntile_ref[e])
    def _():
        y_ref[...] = jnp.zeros(y_ref.shape, BF16)


def _ffn(ntile, xe, wg, wu, wd):
    cap_rows = xe.shape[1]
    rows = lambda e, i, nt_ref: (e, jnp.minimum(i, nt_ref[e] - 1), 0)
    wmap = lambda e, i, nt_ref: (e, 0, 0)
    grid_spec = pltpu.PrefetchScalarGridSpec(
        num_scalar_prefetch=1,
        grid=(N_EXPERTS, cap_rows // TMF),
        in_specs=[
            pl.BlockSpec((1, TMF, D_MODEL), rows),
            pl.BlockSpec((1, D_MODEL, D_FF_EXPERT), wmap),
            pl.BlockSpec((1, D_MODEL, D_FF_EXPERT), wmap),
            pl.BlockSpec((1, D_FF_EXPERT, D_MODEL), wmap),
        ],
        out_specs=pl.BlockSpec((1, TMF, D_MODEL), lambda e, i, nt_ref: (e, i, 0)),
    )
    return pl.pallas_call(
        _ffn_body,
        grid_spec=grid_spec,
        out_shape=jax.ShapeDtypeStruct(xe.shape, BF16),
        compiler_params=_cparams(("arbitrary", "arbitrary")),
        name="expert_ffn",
    )(ntile, xe, wg, wu, wd)


def _combine_body(base_ref, shift_ref, nch_ref, h_ref, gsel_ref, lpos_ref, p_ref, wpp_ref, pg_ref,
                  wpg_ref, bpg_ref, ye_ref, y_ref, ybuf_ref, xbuf_ref, hot_ref, acc_ref, sem_ref, xsem_ref):
    n = pl.program_id(0)
    nt = pl.num_programs(0)
    tn = (((0,), (0,)), ((), ()))
    cur = n % 2

    def first_chunk(e, step, buf):
        row0 = pl.multiple_of(base_ref[e * nt + step], BF16_SUBLANES)
        return pltpu.make_async_copy(ye_ref.at[e, pl.ds(row0, CH)], ybuf_ref.at[buf, pl.ds(e * CH, CH)],
                                     sem_ref.at[buf, e])

    def later_chunk(e, chunk):
        row0 = pl.multiple_of(base_ref[e * nt + n] + chunk * CH, BF16_SUBLANES)
        return pltpu.make_async_copy(ye_ref.at[e, pl.ds(row0, CH)], xbuf_ref, xsem_ref.at[0])

    def gated_hot(e, chunk):
        g = gsel_ref[pl.ds(e, 1), :]
        return _onehot_rows(g, lpos_ref[pl.ds(e, 1), :], chunk * CH - shift_ref[e * nt + n], g)

    @pl.when(n == 0)
    def _():
        for e in range(N_EXPERTS):
            first_chunk(e, 0, 0).start()

    @pl.when(n + 1 < nt)
    def _():
        for e in range(N_EXPERTS):
            first_chunk(e, n + 1, 1 - cur).start()

    for e in range(N_EXPERTS):
        hot_ref[e * CH:(e + 1) * CH, :] = gated_hot(e, jnp.int32(0))
    for e in range(N_EXPERTS):
        first_chunk(e, n, cur).wait()
    acc_ref[...] = h_ref[...] + lax.dot_general(hot_ref[...], ybuf_ref[cur], tn, preferred_element_type=F32)

    def per_expert(e, carry):
        def per_chunk(chunk, c):
            cp = later_chunk(e, chunk)
            cp.start()
            cp.wait()
            acc_ref[...] += lax.dot_general(gated_hot(e, chunk), xbuf_ref[...], tn, preferred_element_type=F32)
            return c
        return lax.fori_loop(1, nch_ref[e * nt + n], per_chunk, carry)

    lax.fori_loop(0, N_EXPERTS, per_expert, 0)

    h2 = acc_ref[...]
    emb = jnp.dot(p_ref[...].astype(BF16), wpp_ref[...], preferred_element_type=F32)
    hn = h2 * lax.rsqrt(jnp.mean(h2 * h2, axis=-1, keepdims=True) + EPS) * pg_ref[...]
    gate = _sigmoid(jnp.dot(hn.astype(BF16), wpg_ref[...], preferred_element_type=F32) + bpg_ref[...])
    y_ref[...] = h2 + gate * emb


def _combine(base, shift, nch, h1, gsel, lpos, p2, wpp, pg, wpg, bpg, ye):
    n = h1.shape[0]
    nt = n // TR
    row = lambda i, *_: (i, 0)
    tile = lambda i, *_: (0, i)
    const = lambda i, *_: (0, 0)
    grid_spec = pltpu.PrefetchScalarGridSpec(
        num_scalar_prefetch=3,
        grid=(nt,),
        in_specs=[
            pl.BlockSpec((TR, D_MODEL), row),
            pl.BlockSpec((N_EXPERTS, TR), tile),
            pl.BlockSpec((N_EXPERTS, TR), tile),
            pl.BlockSpec((TR, D_PLE), row),
            pl.BlockSpec((D_PLE, D_MODEL), const),
            pl.BlockSpec((1, D_MODEL), const),
            pl.BlockSpec((D_MODEL, D_MODEL), const),
            pl.BlockSpec((1, D_MODEL), const),
            pl.BlockSpec(memory_space=pl.ANY),
        ],
        out_specs=pl.BlockSpec((TR, D_MODEL), row),
        scratch_shapes=[pltpu.VMEM((2, N_EXPERTS * CH, D_MODEL), BF16),
                        pltpu.VMEM((CH, D_MODEL), BF16),
                        pltpu.VMEM((N_EXPERTS * CH, TR), BF16),
                        pltpu.VMEM((TR, D_MODEL), F32),
                        pltpu.SemaphoreType.DMA((2, N_EXPERTS)),
                        pltpu.SemaphoreType.DMA((1,))],
    )
    return pl.pallas_call(
        _combine_body,
        grid_spec=grid_spec,
        out_shape=jax.ShapeDtypeStruct((n, D_MODEL), F32),
        compiler_params=_cparams(("arbitrary",)),
        name="combine",
    )(base, shift, nch, h1, gsel, lpos, p2, wpp, pg, wpg, bpg, ye)


def _rope_tables(t):
    rows = t // GRID_W
    row_idx = jnp.repeat(jnp.arange(rows, dtype=F32), GRID_W)
    col_idx = jnp.tile(jnp.arange(GRID_W, dtype=F32), rows)
    freqs = 1.0 / (ROPE_THETA ** (jnp.arange(0, HALF_ROT, 2, dtype=F32) / HALF_ROT))
    ang_r = row_idx[:, None] * freqs[None, :]
    ang_c = col_idx[:, None] * freqs[None, :]
    cr, sr, cc, sc = jnp.cos(ang_r), jnp.sin(ang_r), jnp.cos(ang_c), jnp.sin(ang_c)
    cos_h = jnp.concatenate([cr, cr, cc, cc], axis=-1)
    sin_h = jnp.concatenate([-sr, sr, -sc, sc], axis=-1)
    return jnp.tile(cos_h, (1, LANES // HEAD_DIM)), jnp.tile(sin_h, (1, LANES // HEAD_DIM))


def _prepare_weights(norm1_g, w_in, conv_dw_w, conv_dw_b, conv_ln_g, conv_ln_b, q_norm_g, k_norm_g,
                     conv_out_g, attn_out_g, w_out, norm2_g, w_router, w_gate, w_up, w_down,
                     ple_proj, ple_norm_g, ple_gate_w, ple_gate_b):
    i = 0
    q_scale = (HEAD_DIM ** -0.5) * math.log2(math.e)
    qkg = jnp.concatenate([jnp.tile(q_norm_g[i] * q_scale, N_HEADS), jnp.tile(k_norm_g[i], N_KV_HEADS)])
    bound = HEAD_DIM * jnp.max(jnp.abs(q_norm_g[i] * q_scale)) * jnp.max(jnp.abs(k_norm_g[i]))
    use_shift = bound <= MAX_SCORE_SHIFT
    lane = jnp.arange(LANES)
    fill = jnp.stack([jnp.where(lane == HEAD_DIM, 1.0, 0.0),
                      jnp.where(lane == HEAD_DIM, -jnp.where(use_shift, bound, 0.0), 0.0)]).astype(F32)
    ones_bd = (lane[:, None] // HEAD_DIM == lane[None, :] // HEAD_DIM).astype(BF16)
    tok = jnp.arange(TR)
    wr_hi = w_router[i].astype(BF16)
    wr_lo = (w_router[i] - wr_hi.astype(F32)).astype(BF16)
    wr = jnp.concatenate([wr_hi, wr_lo, jnp.zeros((D_MODEL, LANES - 2 * N_EXPERTS), BF16)], axis=1)
    return dict(
        g1=norm1_g[i][None, :], w_in=w_in[i].astype(BF16), qkg=qkg[None, :], ones_bd=ones_bd,
        fill=fill, use_shift=use_shift,
        dw_w=conv_dw_w[i], dw_b=conv_dw_b[i][None, :], ln_g=conv_ln_g[i][None, :], ln_b=conv_ln_b[i][None, :],
        conv_out_g=conv_out_g[i][None, :], attn_out_g=attn_out_g[i][None, :],
        w_out_c=w_out[i, :D_CONV].astype(BF16), w_out_a=w_out[i, D_CONV:].astype(BF16),
        g2=norm2_g[i][None, :], w_router=wr,
        wg=w_gate[i].astype(BF16), wu=w_up[i].astype(BF16), wd=w_down[i].astype(BF16),
        tri=(tok[:, None] < tok[None, :]).astype(BF16),
        wpp=ple_proj[i].astype(BF16), pg=ple_norm_g[i][None, :], wpg=ple_gate_w[i].astype(BF16),
        bpg=ple_gate_b[i][None, :],
    )


def _trunk(x, p, w):
    b, t, _ = x.shape
    n = b * t
    assert t % TM_OUT == 0 and t % TK == 0 and t % TT == 0 and t % TQ == 0 and n % TR == 0
    cap = CAPACITY_FACTOR * n // N_EXPERTS
    nt = n // TR
    x2 = x.reshape(n, D_MODEL)
    cos_t, sin_t = _rope_tables(t)

    gl, q, k, v = _inproj(x2, w["g1"], w["w_in"], w["qkg"], cos_t, sin_t, w["ones_bd"], w["fill"], t)
    cn = _conv(gl.reshape(b, t, D_CONV), w["dw_w"], w["dw_b"], w["ln_g"], w["ln_b"], w["conv_out_g"])
    ao = lax.cond(w["use_shift"], _attention_shifted, _attention,
                  q.reshape(b, t, -1), k.reshape(b, t, -1), v)
    h1, xn2, aff_t = _outproj(x2, cn.reshape(n, D_CONV), ao.reshape(n, D_Q), w["attn_out_g"],
                              w["w_out_c"], w["w_out_a"], w["g2"], w["w_router"])

    thr, need = _threshold(aff_t, cap)
    gsel, lpos, cnt = _masks(aff_t, thr, need, w["tri"])
    cnt = cnt[:, :, 0].T
    ends = jnp.cumsum(cnt, axis=1)
    start = ends - cnt
    base = start // BF16_SUBLANES * BF16_SUBLANES
    shift = start - base
    nch = jnp.maximum((shift + cnt + CH - 1) // CH, 1)
    keep = ends // BF16_SUBLANES * BF16_SUBLANES - base
    flat = lambda a: a.astype(jnp.int32).reshape(-1)
    base, shift, nch, keepc, keepr = flat(base), flat(shift), flat(nch), flat(keep // CH), flat(keep % CH)
    full_tiles = (cap + TMF - 1) // TMF
    cap_rows = (full_tiles + 1) * TMF
    ntile = jnp.full((N_EXPERTS,), full_tiles, jnp.int32)

    xe = _dispatch(base, shift, nch, keepc, keepr, xn2, gsel, lpos, cap, cap_rows)
    ye = _ffn(ntile, xe, w["wg"], w["wu"], w["wd"])
    y = _combine(base, shift, nch, h1, gsel, lpos, p.reshape(n, D_PLE), w["wpp"], w["pg"], w["wpg"],
                 w["bpg"], ye)
    return y.reshape(b, t, D_MODEL)


def kernel(x_prompt, x_sample, p_prompt, p_sample, norm1_g, w_in, conv_dw_w, conv_dw_b, conv_ln_g, conv_ln_b, q_norm_g, k_norm_g, conv_out_g, attn_out_g, w_out, norm2_g, w_router, w_gate, w_up, w_down, ple_proj, ple_norm_g, ple_gate_w, ple_gate_b):
    w = _prepare_weights(norm1_g, w_in, conv_dw_w, conv_dw_b, conv_ln_g, conv_ln_b, q_norm_g, k_norm_g,
                         conv_out_g, attn_out_g, w_out, norm2_g, w_router, w_gate, w_up, w_down,
                         ple_proj, ple_norm_g, ple_gate_w, ple_gate_b)
    y_prompt = _trunk(x_prompt, p_prompt[0], w)
    y_sample = _trunk(x_sample, p_sample[0], w)
    return (y_prompt, y_sample)
```

```python
import functools
import math

import jax
import jax.numpy as jnp
from jax import lax
from jax.experimental import pallas as pl
from jax.experimental.pallas import tpu as pltpu

D_MODEL = 1024
D_CONV = 512
CONV_WIDTH = 31
CONV_PAD = CONV_WIDTH // 2
N_HEADS = 8
N_KV_HEADS = 2
HEAD_DIM = 64
KV_GROUP = N_HEADS // N_KV_HEADS
D_Q = N_HEADS * HEAD_DIM
D_KV = N_KV_HEADS * HEAD_DIM
D_IN = 2 * D_CONV + D_Q + 2 * D_KV
HALF_ROT = HEAD_DIM // 2
ROPE_THETA = 10000.0
GRID_W = 64
N_EXPERTS = 16
CAPACITY_FACTOR = 2
D_FF_EXPERT = 2048
D_PLE = 256
EPS = 1e-6

LANES = 128
F32_SUBLANES = 8
BF16_SUBLANES = 16
VMEM_LIMIT = 56 * 1024 * 1024

TM = 512
TM_OUT = 1024
TT = 256
HALO = 16
TQ = 512
TK = 512
KV_CHUNKS = 8
MAX_SCORE_SHIFT = 40.0
TR = 512
CH = 128
TMF = 512

F32 = jnp.float32
BF16 = jnp.bfloat16


def _cparams(sem):
    return pltpu.CompilerParams(dimension_semantics=sem, vmem_limit_bytes=VMEM_LIMIT)


def _sigmoid(x):
    return 1.0 / (1.0 + jnp.exp(-x))


def _inproj_body(x_ref, g1_ref, w_ref, qkg_ref, cos_ref, sin_ref, ones_ref, fill_ref,
                 gl_ref, q_ref, k_ref, v_ref):
    for blk in range(TM_OUT // TM):
        _inproj_rows(slice(blk * TM, (blk + 1) * TM), x_ref, g1_ref, w_ref, qkg_ref, cos_ref, sin_ref,
                     ones_ref, fill_ref, gl_ref, q_ref, k_ref, v_ref)


def _inproj_rows(rows, x_ref, g1_ref, w_ref, qkg_ref, cos_ref, sin_ref, ones_ref, fill_ref,
                 gl_ref, q_ref, k_ref, v_ref):
    x = x_ref[rows, :]
    a = x * lax.rsqrt(jnp.mean(x * x, axis=-1, keepdims=True) + EPS) * g1_ref[...]
    z = jnp.dot(a.astype(BF16), w_ref[...], preferred_element_type=F32)
    val = z[:, :D_CONV]
    gate = z[:, D_CONV:2 * D_CONV]
    gl_ref[rows, :] = (val * _sigmoid(gate)).astype(BF16)

    lane = lax.broadcasted_iota(jnp.int32, (x.shape[0], LANES), 1)
    first_half = (lane % HALF_ROT) < (HALF_ROT // 2)
    low_head = lane < HEAD_DIM
    cos = cos_ref[rows, :]
    sin = sin_ref[rows, :]
    o0 = 2 * D_CONV
    n_chunks = (D_Q + D_KV) // LANES
    for c in range(n_chunks):
        qc = z[:, o0 + c * LANES:o0 + (c + 1) * LANES]
        ssum = jnp.dot((qc * qc).astype(BF16), ones_ref[...], preferred_element_type=F32)
        qn = qc * lax.rsqrt(ssum * (1.0 / HEAD_DIM) + EPS) * qkg_ref[:, c * LANES:(c + 1) * LANES]
        partner = jnp.where(first_half, pltpu.roll(qn, LANES - HALF_ROT // 2, 1),
                            pltpu.roll(qn, HALF_ROT // 2, 1))
        qr = qn * cos + partner * sin
        fill = fill_ref[0:1, :] if c < D_Q // LANES else fill_ref[1:2, :]
        even = jnp.where(low_head, qr, fill).astype(BF16)
        odd = jnp.where(low_head, pltpu.roll(qr, HEAD_DIM, 1), fill).astype(BF16)
        if c < D_Q // LANES:
            q_ref[rows, (2 * c) * LANES:(2 * c + 1) * LANES] = even
            q_ref[rows, (2 * c + 1) * LANES:(2 * c + 2) * LANES] = odd
        else:
            k_ref[rows, 0:LANES] = even
            k_ref[rows, LANES:2 * LANES] = odd
    vv = z[:, o0 + D_Q + D_KV:]
    v_ref[0:LANES, rows] = jnp.where(low_head, vv, 1.0).T.astype(BF16)
    v_ref[LANES:2 * LANES, rows] = jnp.where(low_head, pltpu.roll(vv, HEAD_DIM, 1), 1.0).T.astype(BF16)


def _inproj(x2, g1, w_in_bf, qkg, cos_t, sin_t, ones_bd, fill, seq_len):
    n = x2.shape[0]
    pos_tiles = seq_len // TM_OUT
    row = lambda i: (i, 0)
    const = lambda i: (0, 0)
    return pl.pallas_call(
        _inproj_body,
        grid=(n // TM_OUT,),
        in_specs=[
            pl.BlockSpec((TM_OUT, D_MODEL), row),
            pl.BlockSpec((1, D_MODEL), const),
            pl.BlockSpec((D_MODEL, D_IN), const),
            pl.BlockSpec((1, D_Q + D_KV), const),
            pl.BlockSpec((TM_OUT, LANES), lambda i: (i % pos_tiles, 0)),
            pl.BlockSpec((TM_OUT, LANES), lambda i: (i % pos_tiles, 0)),
            pl.BlockSpec((LANES, LANES), const),
            pl.BlockSpec((2, LANES), const),
        ],
        out_specs=[
            pl.BlockSpec((TM_OUT, D_CONV), row),
            pl.BlockSpec((TM_OUT, N_HEADS * LANES), row),
            pl.BlockSpec((TM_OUT, N_KV_HEADS * LANES), row),
            pl.BlockSpec((N_KV_HEADS * LANES, TM_OUT), lambda i: (0, i)),
        ],
        out_shape=[
            jax.ShapeDtypeStruct((n, D_CONV), BF16),
            jax.ShapeDtypeStruct((n, N_HEADS * LANES), BF16),
            jax.ShapeDtypeStruct((n, N_KV_HEADS * LANES), BF16),
            jax.ShapeDtypeStruct((N_KV_HEADS * LANES, n), BF16),
        ],
        compiler_params=_cparams(("parallel",)),
        name="inproj",
    )(x2, g1, w_in_bf, qkg, cos_t, sin_t, ones_bd, fill)


CONV_ROWS = 64
CONV_SPAN = TT + 2 * HALO - F32_SUBLANES


def _conv_body(left_ref, main_ref, right_ref, w_ref, b_ref, lng_ref, lnb_ref, og_ref,
               out_ref, win_ref, conv_ref):
    i = pl.program_id(1)
    last = pl.num_programs(1) - 1
    left = left_ref[0].astype(F32)
    right = right_ref[0].astype(F32)
    win_ref[0, 0:HALO, :] = jnp.where(i > 0, left, 0.0)
    win_ref[0, HALO:HALO + TT, :] = main_ref[0].astype(F32)
    win_ref[0, HALO + TT:HALO + TT + HALO, :] = jnp.where(i < last, right, 0.0)
    for r in range(1, F32_SUBLANES):
        win_ref[r, 0:CONV_SPAN, :] = win_ref[0, r:r + CONV_SPAN, :]
    base = HALO - CONV_PAD

    def row_block(rr, carry):
        r0 = pl.multiple_of(rr * CONV_ROWS, CONV_ROWS)
        groups = CONV_ROWS // F32_SUBLANES
        reach = (base + CONV_WIDTH - 1) // F32_SUBLANES + 1
        for c in range(D_CONV // LANES):
            ls = slice(c * LANES, (c + 1) * LANES)
            accs = [None] * groups
            for shift in range(F32_SUBLANES):
                for q in range(groups + reach - 1):
                    uses = [(q - a, a * F32_SUBLANES + shift - base) for a in range(reach)
                            if 0 <= a * F32_SUBLANES + shift - base < CONV_WIDTH and 0 <= q - a < groups]
                    if not uses:
                        continue
                    tile = win_ref[shift, pl.ds(r0 + q * F32_SUBLANES, F32_SUBLANES), ls]
                    for g, k in uses:
                        term = tile * w_ref[k:k + 1, ls]
                        accs[g] = term if accs[g] is None else accs[g] + term
            for g in range(groups):
                conv_ref[pl.ds(r0 + g * F32_SUBLANES, F32_SUBLANES), ls] = accs[g]
        return carry

    lax.fori_loop(0, TT // CONV_ROWS, row_block, 0)
    cv = conv_ref[...] + b_ref[...]
    mu = jnp.mean(cv, axis=-1, keepdims=True)
    d = cv - mu
    var = jnp.mean(d * d, axis=-1, keepdims=True)
    y = d * lax.rsqrt(var + EPS) * lng_ref[...] + lnb_ref[...]
    y = y * _sigmoid(y)
    y = y * lax.rsqrt(jnp.mean(y * y, axis=-1, keepdims=True) + EPS) * og_ref[...]
    out_ref[0] = y.astype(BF16)


def _conv(gl3, dw_w, dw_b, ln_g, ln_b, out_g):
    b, t, _ = gl3.shape
    hb = TT // HALO
    n_halo = t // HALO
    const = lambda bb, i: (0, 0)
    return pl.pallas_call(
        _conv_body,
        grid=(b, t // TT),
        in_specs=[
            pl.BlockSpec((1, HALO, D_CONV), lambda bb, i: (bb, jnp.maximum(i * hb - 1, 0), 0)),
            pl.BlockSpec((1, TT, D_CONV), lambda bb, i: (bb, i, 0)),
            pl.BlockSpec((1, HALO, D_CONV), lambda bb, i: (bb, jnp.minimum((i + 1) * hb, n_halo - 1), 0)),
            pl.BlockSpec((CONV_WIDTH, D_CONV), const),
            pl.BlockSpec((1, D_CONV), const),
            pl.BlockSpec((1, D_CONV), const),
            pl.BlockSpec((1, D_CONV), const),
            pl.BlockSpec((1, D_CONV), const),
        ],
        out_specs=pl.BlockSpec((1, TT, D_CONV), lambda bb, i: (bb, i, 0)),
        out_shape=jax.ShapeDtypeStruct((b, t, D_CONV), BF16),
        scratch_shapes=[pltpu.VMEM((F32_SUBLANES, TT + 2 * HALO, D_CONV), F32),
                        pltpu.VMEM((TT, D_CONV), F32)],
        compiler_params=_cparams(("parallel", "parallel")),
        name="conv",
    )(gl3, gl3, gl3, dw_w, dw_b, ln_g, ln_b, out_g)


def _attn_body(q_ref, k_ref, vt_ref, o_ref, qs_ref, m_ref, acc_ref):
    t = k_ref.shape[1]
    nt_dims = (((1,), (1,)), ((), ()))
    for h in range(KV_GROUP):
        qs_ref[h * TQ:(h + 1) * TQ, :] = q_ref[0, :, h * LANES:(h + 1) * LANES]
    m_ref[...] = jnp.full(m_ref.shape, -jnp.inf, F32)
    acc_ref[...] = jnp.zeros(acc_ref.shape, F32)

    def step(kt, carry):
        start = pl.multiple_of(kt * TK, TK)
        s = lax.dot_general(qs_ref[...], k_ref[0, pl.ds(start, TK), :], nt_dims,
                            preferred_element_type=F32)
        m_old = m_ref[...]
        m_new = jnp.maximum(m_old, jnp.max(s, axis=-1, keepdims=True))
        p = jnp.exp2(s - m_new[:, 0:1])
        alpha = jnp.exp2(m_old - m_new)
        acc_ref[...] = alpha * acc_ref[...] + lax.dot_general(
            p.astype(BF16), vt_ref[:, pl.ds(start, TK)], nt_dims, preferred_element_type=F32)
        m_ref[...] = m_new
        return carry

    lax.fori_loop(0, t // TK, step, 0)
    acc = acc_ref[...]
    res = acc / pltpu.roll(acc, HEAD_DIM, 1)
    _attn_store(o_ref, [res[h * TQ:(h + 1) * TQ, :] for h in range(KV_GROUP)])


def _attn_shifted_body(q_ref, k_ref, vt_ref, o_ref, qs_ref, acc_ref):
    t = k_ref.shape[1]
    for h in range(KV_GROUP):
        qs_ref[h * TQ:(h + 1) * TQ, :] = q_ref[0, :, h * LANES:(h + 1) * LANES]
    acc_ref[...] = jnp.zeros(acc_ref.shape, F32)

    chunks = math.gcd(KV_CHUNKS, t // TK)

    def step(kt, carry):
        qs = qs_ref[...]
        part = None
        for c in range(chunks):
            start = pl.multiple_of(kt * (chunks * TK) + c * TK, TK)
            st = lax.dot_general(k_ref[0, pl.ds(start, TK), :], qs, (((1,), (1,)), ((), ())),
                                 preferred_element_type=F32)
            pv = jnp.dot(vt_ref[:, pl.ds(start, TK)], jnp.exp2(st).astype(BF16),
                         preferred_element_type=F32)
            part = pv if part is None else part + pv
        acc_ref[...] += part
        return carry

    lax.fori_loop(0, t // (chunks * TK), step, 0)
    acc = acc_ref[...]
    res = acc * (1.0 / acc[HEAD_DIM:HEAD_DIM + 1, :])
    _attn_store(o_ref, [res[:, h * TQ:(h + 1) * TQ].T for h in range(KV_GROUP)])


def _attn_store(o_ref, heads):
    lane = lax.broadcasted_iota(jnp.int32, (TQ, LANES), 1)
    low = lane < HEAD_DIM
    for hp in range(KV_GROUP // 2):
        o_ref[0, :, hp * LANES:(hp + 1) * LANES] = jnp.where(
            low, heads[2 * hp], pltpu.roll(heads[2 * hp + 1], HEAD_DIM, 1)).astype(BF16)


def _attention_call(body, name, scratch, q3, k3, vt):
    b, t, _ = q3.shape
    gw = KV_GROUP * LANES
    return pl.pallas_call(
        body,
        grid=(b, N_KV_HEADS, t // TQ),
        in_specs=[
            pl.BlockSpec((1, TQ, gw), lambda bb, j, i: (bb, i, j)),
            pl.BlockSpec((1, t, LANES), lambda bb, j, i: (bb, 0, j)),
            pl.BlockSpec((LANES, t), lambda bb, j, i: (j, bb)),
        ],
        out_specs=pl.BlockSpec((1, TQ, KV_GROUP * HEAD_DIM), lambda bb, j, i: (bb, i, j)),
        out_shape=jax.ShapeDtypeStruct((b, t, D_Q), BF16),
        scratch_shapes=[pltpu.VMEM((KV_GROUP * TQ, LANES), BF16)] + scratch,
        compiler_params=_cparams(("parallel", "parallel", "parallel")),
        name=name,
    )(q3, k3, vt)


def _attention_shifted(q3, k3, vt):
    return _attention_call(_attn_shifted_body, "attention_shifted",
                           [pltpu.VMEM((LANES, KV_GROUP * TQ), F32)], q3, k3, vt)


def _attention(q3, k3, vt):
    return _attention_call(_attn_body, "attention",
                           [pltpu.VMEM((KV_GROUP * TQ, LANES), F32),
                            pltpu.VMEM((KV_GROUP * TQ, LANES), F32)], q3, k3, vt)


def _outproj_body(x_ref, cn_ref, ao_ref, ag_ref, wc_ref, wa_ref, g2_ref, wr_ref,
                  h_ref, xn_ref, aff_ref):
    for blk in range(TM_OUT // TM):
        rows = slice(blk * TM, (blk + 1) * TM)
        ao = ao_ref[rows, :].astype(F32)
        an = ao * lax.rsqrt(jnp.mean(ao * ao, axis=-1, keepdims=True) + EPS) * ag_ref[...]
        h = (x_ref[rows, :]
             + jnp.dot(cn_ref[rows, :], wc_ref[...], preferred_element_type=F32)
             + jnp.dot(an.astype(BF16), wa_ref[...], preferred_element_type=F32))
        h_ref[rows, :] = h
        xn = h * lax.rsqrt(jnp.mean(h * h, axis=-1, keepdims=True) + EPS) * g2_ref[...]
        xn_hi = xn.astype(BF16)
        xn_ref[rows, :] = xn_hi
        xn_lo = (xn - xn_hi.astype(F32)).astype(BF16)
        parts = (jnp.dot(xn_hi, wr_ref[...], preferred_element_type=F32)
                 + jnp.dot(xn_lo, wr_ref[...], preferred_element_type=F32))
        parts_t = parts.T
        logits = parts_t[0:N_EXPERTS, :] + parts_t[N_EXPERTS:2 * N_EXPERTS, :]
        mx = jnp.max(logits, axis=0, keepdims=True)
        ex = jnp.exp(logits - mx)
        aff_ref[:, rows] = ex / jnp.sum(ex, axis=0, keepdims=True)


def _outproj(x2, cn, ao, ag, wc, wa, g2, wr):
    n = x2.shape[0]
    row = lambda i: (i, 0)
    const = lambda i: (0, 0)
    return pl.pallas_call(
        _outproj_body,
        grid=(n // TM_OUT,),
        in_specs=[
            pl.BlockSpec((TM_OUT, D_MODEL), row),
            pl.BlockSpec((TM_OUT, D_CONV), row),
            pl.BlockSpec((TM_OUT, D_Q), row),
            pl.BlockSpec((1, D_Q), const),
            pl.BlockSpec((D_CONV, D_MODEL), const),
            pl.BlockSpec((D_Q, D_MODEL), const),
            pl.BlockSpec((1, D_MODEL), const),
            pl.BlockSpec((D_MODEL, LANES), const),
        ],
        out_specs=[
            pl.BlockSpec((TM_OUT, D_MODEL), row),
            pl.BlockSpec((TM_OUT, D_MODEL), row),
            pl.BlockSpec((N_EXPERTS, TM_OUT), lambda i: (0, i)),
        ],
        out_shape=[
            jax.ShapeDtypeStruct((n, D_MODEL), F32),
            jax.ShapeDtypeStruct((n, D_MODEL), BF16),
            jax.ShapeDtypeStruct((N_EXPERTS, n), F32),
        ],
        compiler_params=_cparams(("parallel",)),
        name="outproj",
    )(x2, cn, ao, ag, wc, wa, g2, wr)


def _threshold_body(cap, aff_ref, thr_ref, need_ref):
    def step(it, lo):
        cand = lo | (jnp.int32(1) << (30 - it))
        bits = pltpu.bitcast(aff_ref[...], jnp.int32)
        cnt = jnp.sum((bits >= cand).astype(jnp.int32), axis=1, keepdims=True)
        return jnp.where(cnt >= cap, cand, lo)

    thr = lax.fori_loop(0, 31, step, jnp.zeros((N_EXPERTS, 1), jnp.int32))
    bits = pltpu.bitcast(aff_ref[...], jnp.int32)
    n_gt = jnp.sum((bits > thr).astype(jnp.int32), axis=1, keepdims=True)
    thr_ref[...] = jnp.broadcast_to(thr, thr_ref.shape)
    need_ref[...] = jnp.broadcast_to(cap - n_gt, need_ref.shape)


def _threshold(aff_t, cap):
    n = aff_t.shape[1]
    full = lambda: (0, 0)
    return pl.pallas_call(
        functools.partial(_threshold_body, cap),
        in_specs=[pl.BlockSpec((N_EXPERTS, n), full)],
        out_specs=[pl.BlockSpec((N_EXPERTS, LANES), full), pl.BlockSpec((N_EXPERTS, LANES), full)],
        out_shape=[jax.ShapeDtypeStruct((N_EXPERTS, LANES), jnp.int32),
                   jax.ShapeDtypeStruct((N_EXPERTS, LANES), jnp.int32)],
        compiler_params=pltpu.CompilerParams(vmem_limit_bytes=VMEM_LIMIT),
        name="threshold",
    )(aff_t)


def _mask_body(aff_ref, thr_ref, need_ref, tri_ref, gsel_ref, lpos_ref, cnt_ref, eqc_ref):
    @pl.when(pl.program_id(0) == 0)
    def _():
        eqc_ref[...] = jnp.zeros(eqc_ref.shape, F32)

    aff = aff_ref[...]
    bits = pltpu.bitcast(aff, jnp.int32)
    thr = thr_ref[:, 0:1]
    need = need_ref[:, 0:1].astype(F32)
    gt = bits > thr
    eq = bits == thr
    eq_f = jnp.where(eq, 1.0, 0.0)
    eq_rank = eqc_ref[:, 0:1] + jnp.dot(eq_f.astype(BF16), tri_ref[...], preferred_element_type=F32)
    sel = gt | (eq & (eq_rank < need))
    sel_f = jnp.where(sel, 1.0, 0.0)
    gsel_ref[...] = jnp.where(sel, aff, -1.0)
    lpos_ref[...] = jnp.dot(sel_f.astype(BF16), tri_ref[...], preferred_element_type=F32)
    cnt = jnp.sum(sel_f, axis=1, keepdims=True)
    cnt_ref[0] = jnp.broadcast_to(cnt, (N_EXPERTS, LANES)).astype(jnp.int32)
    eqc_ref[...] = eqc_ref[...] + jnp.sum(eq_f, axis=1, keepdims=True)


def _masks(aff_t, thr, need, tri):
    n = aff_t.shape[1]
    nt = n // TR
    const = lambda i: (0, 0)
    tile = lambda i: (0, i)
    return pl.pallas_call(
        _mask_body,
        grid=(nt,),
        in_specs=[
            pl.BlockSpec((N_EXPERTS, TR), tile),
            pl.BlockSpec((N_EXPERTS, LANES), const),
            pl.BlockSpec((N_EXPERTS, LANES), const),
            pl.BlockSpec((TR, TR), const),
        ],
        out_specs=[
            pl.BlockSpec((N_EXPERTS, TR), tile),
            pl.BlockSpec((N_EXPERTS, TR), tile),
            pl.BlockSpec((1, N_EXPERTS, LANES), lambda i: (i, 0, 0)),
        ],
        out_shape=[
            jax.ShapeDtypeStruct((N_EXPERTS, n), F32),
            jax.ShapeDtypeStruct((N_EXPERTS, n), F32),
            jax.ShapeDtypeStruct((nt, N_EXPERTS, LANES), jnp.int32),
        ],
        scratch_shapes=[pltpu.VMEM((N_EXPERTS, LANES), F32)],
        compiler_params=_cparams(("arbitrary",)),
        name="masks",
    )(aff_t, thr, need, tri)


def _onehot_rows(gsel_row, lpos_row, first_slot, value):
    slot = lax.broadcasted_iota(jnp.int32, (CH, TR), 0).astype(F32) + first_slot.astype(F32)
    return jnp.where((gsel_row >= 0.0) & (lpos_row == slot), value, 0.0).astype(BF16)


def _dispatch_body(cap, base_ref, shift_ref, nch_ref, keepc_ref, keepr_ref, xn_ref, gsel_ref, lpos_ref,
                   xe_ref, stage_ref, carry_ref, hot_ref, sem_ref, xsem_ref):
    n = pl.program_id(0)
    nt = pl.num_programs(0)

    def copy(e, slot, sem, chunk):
        row0 = pl.multiple_of(base_ref[e * nt + n] + chunk * CH, BF16_SUBLANES)
        return pltpu.make_async_copy(stage_ref.at[slot], xe_ref.at[e, pl.ds(row0, CH)], sem)

    @pl.when(n == 0)
    def _():
        carry_ref[...] = jnp.zeros(carry_ref.shape, BF16)
        stage_ref[N_EXPERTS] = jnp.zeros((CH, D_MODEL), BF16)
        rows_total = xe_ref.shape[1]
        n_fill = -(-(rows_total - cap) // CH)
        starts = [rows_total - (j + 1) * CH for j in range(n_fill)]
        fills = [pltpu.make_async_copy(stage_ref.at[N_EXPERTS], xe_ref.at[e, pl.ds(row0, CH)], xsem_ref.at[0])
                 for e in range(N_EXPERTS) for row0 in starts]
        for cp in fills:
            cp.start()
        for cp in fills:
            cp.wait()

    def rows_of(e, chunk):
        hot = _onehot_rows(gsel_ref[pl.ds(e, 1), :], lpos_ref[pl.ds(e, 1), :],
                           chunk * CH - shift_ref[e * nt + n], 1.0)
        return jnp.dot(hot, xn_ref[...], preferred_element_type=F32)

    def kept_group(e, slot):
        row0 = pl.multiple_of(keepr_ref[e * nt + n], BF16_SUBLANES)
        return stage_ref[slot, pl.ds(row0, BF16_SUBLANES), :]

    for e in range(N_EXPERTS):
        hot_ref[e * CH:(e + 1) * CH, :] = _onehot_rows(gsel_ref[e:e + 1, :], lpos_ref[e:e + 1, :],
                                                       -shift_ref[e * nt + n], 1.0)
    rows_all = jnp.dot(hot_ref[...], xn_ref[...], preferred_element_type=F32)

    @pl.when(n > 0)
    def _():
        for e in range(N_EXPERTS):
            copy(e, e, sem_ref.at[e], 0).wait()

    for e in range(N_EXPERTS):
        rows = rows_all[e * CH:(e + 1) * CH]
        stage_ref[e, 0:BF16_SUBLANES, :] = (rows[0:BF16_SUBLANES] + carry_ref[e].astype(F32)).astype(BF16)
        stage_ref[e, BF16_SUBLANES:CH, :] = rows[BF16_SUBLANES:CH].astype(BF16)
        copy(e, e, sem_ref.at[e], 0).start()
        carry_ref[e] = jnp.where(keepc_ref[e * nt + n] == 0, kept_group(e, e), jnp.zeros((), BF16))

    @pl.when(n == nt - 1)
    def _():
        for e in range(N_EXPERTS):
            copy(e, e, sem_ref.at[e], 0).wait()

    def per_expert(e, carry):
        def per_chunk(chunk, c):
            stage_ref[N_EXPERTS] = rows_of(e, chunk).astype(BF16)
            cp = copy(e, N_EXPERTS, xsem_ref.at[0], chunk)
            cp.start()

            @pl.when(keepc_ref[e * nt + n] == chunk)
            def _():
                carry_ref[e] = kept_group(e, N_EXPERTS)
            cp.wait()
            return c
        return lax.fori_loop(1, nch_ref[e * nt + n], per_chunk, carry)

    lax.fori_loop(0, N_EXPERTS, per_expert, 0)


def _dispatch(base, shift, nch, keepc, keepr, xn2, gsel, lpos, cap, cap_rows):
    n = xn2.shape[0]
    nt = n // TR
    grid_spec = pltpu.PrefetchScalarGridSpec(
        num_scalar_prefetch=5,
        grid=(nt,),
        in_specs=[
            pl.BlockSpec((TR, D_MODEL), lambda i, *_: (i, 0)),
            pl.BlockSpec((N_EXPERTS, TR), lambda i, *_: (0, i)),
            pl.BlockSpec((N_EXPERTS, TR), lambda i, *_: (0, i)),
        ],
        out_specs=pl.BlockSpec(memory_space=pl.ANY),
        scratch_shapes=[pltpu.VMEM((N_EXPERTS + 1, CH, D_MODEL), BF16),
                        pltpu.VMEM((N_EXPERTS, BF16_SUBLANES, D_MODEL), BF16),
                        pltpu.VMEM((N_EXPERTS * CH, TR), BF16),
                        pltpu.SemaphoreType.DMA((N_EXPERTS,)),
                        pltpu.SemaphoreType.DMA((1,))],
    )
    return pl.pallas_call(
        functools.partial(_dispatch_body, cap),
        grid_spec=grid_spec,
        out_shape=jax.ShapeDtypeStruct((N_EXPERTS, cap_rows, D_MODEL), BF16),
        compiler_params=_cparams(("arbitrary",)),
        name="dispatch",
    )(base, shift, nch, keepc, keepr, xn2, gsel, lpos)


def _ffn_body(ntile_ref, x_ref, wg_ref, wu_ref, wd_ref, y_ref):
    e = pl.program_id(0)
    i = pl.program_id(1)

    @pl.when(i < ntile_ref[e])
    def _():
        x = x_ref[0]
        g = jnp.dot(x, wg_ref[0], preferred_element_type=F32)
        u = jnp.dot(x, wu_ref[0], preferred_element_type=F32)
        hid = (g * _sigmoid(g) * u).astype(BF16)
        y_ref[0] = jnp.dot(hid, wd_ref[0], preferred_element_type=F32).astype(BF16)

    @pl.when(i >= ntile_ref[e])
    def _():
        y_ref[...] = jnp.zeros(y_ref.shape, BF16)


def _ffn(ntile, xe, wg, wu, wd):
    cap_rows = xe.shape[1]
    rows = lambda e, i, nt_ref: (e, jnp.minimum(i, nt_ref[e] - 1), 0)
    wmap = lambda e, i, nt_ref: (e, 0, 0)
    grid_spec = pltpu.PrefetchScalarGridSpec(
        num_scalar_prefetch=1,
        grid=(N_EXPERTS, cap_rows // TMF),
        in_specs=[
            pl.BlockSpec((1, TMF, D_MODEL), rows),
            pl.BlockSpec((1, D_MODEL, D_FF_EXPERT), wmap),
            pl.BlockSpec((1, D_MODEL, D_FF_EXPERT), wmap),
            pl.BlockSpec((1, D_FF_EXPERT, D_MODEL), wmap),
        ],
        out_specs=pl.BlockSpec((1, TMF, D_MODEL), lambda e, i, nt_ref: (e, i, 0)),
    )
    return pl.pallas_call(
        _ffn_body,
        grid_spec=grid_spec,
        out_shape=jax.ShapeDtypeStruct(xe.shape, BF16),
        compiler_params=_cparams(("arbitrary", "arbitrary")),
        name="expert_ffn",
    )(ntile, xe, wg, wu, wd)


def _combine_body(base_ref, shift_ref, nch_ref, h_ref, gsel_ref, lpos_ref, p_ref, wpp_ref, pg_ref,
                  wpg_ref, bpg_ref, ye_ref, y_ref, ybuf_ref, xbuf_ref, hot_ref, acc_ref, sem_ref, xsem_ref):
    n = pl.program_id(0)
    nt = pl.num_programs(0)
    tn = (((0,), (0,)), ((), ()))
    cur = n % 2

    def first_chunk(e, step, buf):
        row0 = pl.multiple_of(base_ref[e * nt + step], BF16_SUBLANES)
        return pltpu.make_async_copy(ye_ref.at[e, pl.ds(row0, CH)], ybuf_ref.at[buf, pl.ds(e * CH, CH)],
                                     sem_ref.at[buf, e])

    def later_chunk(e, chunk):
        row0 = pl.multiple_of(base_ref[e * nt + n] + chunk * CH, BF16_SUBLANES)
        return pltpu.make_async_copy(ye_ref.at[e, pl.ds(row0, CH)], xbuf_ref, xsem_ref.at[0])

    def gated_hot(e, chunk):
        g = gsel_ref[pl.ds(e, 1), :]
        return _onehot_rows(g, lpos_ref[pl.ds(e, 1), :], chunk * CH - shift_ref[e * nt + n], g)

    @pl.when(n == 0)
    def _():
        for e in range(N_EXPERTS):
            first_chunk(e, 0, 0).start()

    @pl.when(n + 1 < nt)
    def _():
        for e in range(N_EXPERTS):
            first_chunk(e, n + 1, 1 - cur).start()

    for e in range(N_EXPERTS):
        hot_ref[e * CH:(e + 1) * CH, :] = gated_hot(e, jnp.int32(0))
    for e in range(N_EXPERTS):
        first_chunk(e, n, cur).wait()
    acc_ref[...] = h_ref[...] + lax.dot_general(hot_ref[...], ybuf_ref[cur], tn, preferred_element_type=F32)

    def per_expert(e, carry):
        def per_chunk(chunk, c):
            cp = later_chunk(e, chunk)
            cp.start()
            cp.wait()
            acc_ref[...] += lax.dot_general(gated_hot(e, chunk), xbuf_ref[...], tn, preferred_element_type=F32)
            return c
        return lax.fori_loop(1, nch_ref[e * nt + n], per_chunk, carry)

    lax.fori_loop(0, N_EXPERTS, per_expert, 0)

    h2 = acc_ref[...]
    emb = jnp.dot(p_ref[...].astype(BF16), wpp_ref[...], preferred_element_type=F32)
    hn = h2 * lax.rsqrt(jnp.mean(h2 * h2, axis=-1, keepdims=True) + EPS) * pg_ref[...]
    gate = _sigmoid(jnp.dot(hn.astype(BF16), wpg_ref[...], preferred_element_type=F32) + bpg_ref[...])
    y_ref[...] = h2 + gate * emb


def _combine(base, shift, nch, h1, gsel, lpos, p2, wpp, pg, wpg, bpg, ye):
    n = h1.shape[0]
    nt = n // TR
    row = lambda i, *_: (i, 0)
    tile = lambda i, *_: (0, i)
    const = lambda i, *_: (0, 0)
    grid_spec = pltpu.PrefetchScalarGridSpec(
        num_scalar_prefetch=3,
        grid=(nt,),
        in_specs=[
            pl.BlockSpec((TR, D_MODEL), row),
            pl.BlockSpec((N_EXPERTS, TR), tile),
            pl.BlockSpec((N_EXPERTS, TR), tile),
            pl.BlockSpec((TR, D_PLE), row),
            pl.BlockSpec((D_PLE, D_MODEL), const),
            pl.BlockSpec((1, D_MODEL), const),
            pl.BlockSpec((D_MODEL, D_MODEL), const),
            pl.BlockSpec((1, D_MODEL), const),
            pl.BlockSpec(memory_space=pl.ANY),
        ],
        out_specs=pl.BlockSpec((TR, D_MODEL), row),
        scratch_shapes=[pltpu.VMEM((2, N_EXPERTS * CH, D_MODEL), BF16),
                        pltpu.VMEM((CH, D_MODEL), BF16),
                        pltpu.VMEM((N_EXPERTS * CH, TR), BF16),
                        pltpu.VMEM((TR, D_MODEL), F32),
                        pltpu.SemaphoreType.DMA((2, N_EXPERTS)),
                        pltpu.SemaphoreType.DMA((1,))],
    )
    return pl.pallas_call(
        _combine_body,
        grid_spec=grid_spec,
        out_shape=jax.ShapeDtypeStruct((n, D_MODEL), F32),
        compiler_params=_cparams(("arbitrary",)),
        name="combine",
    )(base, shift, nch, h1, gsel, lpos, p2, wpp, pg, wpg, bpg, ye)


def _rope_tables(t):
    rows = t // GRID_W
    row_idx = jnp.repeat(jnp.arange(rows, dtype=F32), GRID_W)
    col_idx = jnp.tile(jnp.arange(GRID_W, dtype=F32), rows)
    freqs = 1.0 / (ROPE_THETA ** (jnp.arange(0, HALF_ROT, 2, dtype=F32) / HALF_ROT))
    ang_r = row_idx[:, None] * freqs[None, :]
    ang_c = col_idx[:, None] * freqs[None, :]
    cr, sr, cc, sc = jnp.cos(ang_r), jnp.sin(ang_r), jnp.cos(ang_c), jnp.sin(ang_c)
    cos_h = jnp.concatenate([cr, cr, cc, cc], axis=-1)
    sin_h = jnp.concatenate([-sr, sr, -sc, sc], axis=-1)
    return jnp.tile(cos_h, (1, LANES // HEAD_DIM)), jnp.tile(sin_h, (1, LANES // HEAD_DIM))


def _prepare_weights(norm1_g, w_in, conv_dw_w, conv_dw_b, conv_ln_g, conv_ln_b, q_norm_g, k_norm_g,
                     conv_out_g, attn_out_g, w_out, norm2_g, w_router, w_gate, w_up, w_down,
                     ple_proj, ple_norm_g, ple_gate_w, ple_gate_b):
    i = 0
    q_scale = (HEAD_DIM ** -0.5) * math.log2(math.e)
    qkg = jnp.concatenate([jnp.tile(q_norm_g[i] * q_scale, N_HEADS), jnp.tile(k_norm_g[i], N_KV_HEADS)])
    bound = HEAD_DIM * jnp.max(jnp.abs(q_norm_g[i] * q_scale)) * jnp.max(jnp.abs(k_norm_g[i]))
    use_shift = bound <= MAX_SCORE_SHIFT
    lane = jnp.arange(LANES)
    fill = jnp.stack([jnp.where(lane == HEAD_DIM, 1.0, 0.0),
                      jnp.where(lane == HEAD_DIM, -jnp.where(use_shift, bound, 0.0), 0.0)]).astype(F32)
    ones_bd = (lane[:, None] // HEAD_DIM == lane[None, :] // HEAD_DIM).astype(BF16)
    tok = jnp.arange(TR)
    wr_hi = w_router[i].astype(BF16)
    wr_lo = (w_router[i] - wr_hi.astype(F32)).astype(BF16)
    wr = jnp.concatenate([wr_hi, wr_lo, jnp.zeros((D_MODEL, LANES - 2 * N_EXPERTS), BF16)], axis=1)
    return dict(
        g1=norm1_g[i][None, :], w_in=w_in[i].astype(BF16), qkg=qkg[None, :], ones_bd=ones_bd,
        fill=fill, use_shift=use_shift,
        dw_w=conv_dw_w[i], dw_b=conv_dw_b[i][None, :], ln_g=conv_ln_g[i][None, :], ln_b=conv_ln_b[i][None, :],
        conv_out_g=conv_out_g[i][None, :], attn_out_g=attn_out_g[i][None, :],
        w_out_c=w_out[i, :D_CONV].astype(BF16), w_out_a=w_out[i, D_CONV:].astype(BF16),
        g2=norm2_g[i][None, :], w_router=wr,
        wg=w_gate[i].astype(BF16), wu=w_up[i].astype(BF16), wd=w_down[i].astype(BF16),
        tri=(tok[:, None] < tok[None, :]).astype(BF16),
        wpp=ple_proj[i].astype(BF16), pg=ple_norm_g[i][None, :], wpg=ple_gate_w[i].astype(BF16),
        bpg=ple_gate_b[i][None, :],
    )


def _trunk(x, p, w):
    b, t, _ = x.shape
    n = b * t
    assert t % TM_OUT == 0 and t % TK == 0 and t % TT == 0 and t % TQ == 0 and n % TR == 0
    cap = CAPACITY_FACTOR * n // N_EXPERTS
    nt = n // TR
    x2 = x.reshape(n, D_MODEL)
    cos_t, sin_t = _rope_tables(t)

    gl, q, k, v = _inproj(x2, w["g1"], w["w_in"], w["qkg"], cos_t, sin_t, w["ones_bd"], w["fill"], t)
    cn = _conv(gl.reshape(b, t, D_CONV), w["dw_w"], w["dw_b"], w["ln_g"], w["ln_b"], w["conv_out_g"])
    ao = lax.cond(w["use_shift"], _attention_shifted, _attention,
                  q.reshape(b, t, -1), k.reshape(b, t, -1), v)
    h1, xn2, aff_t = _outproj(x2, cn.reshape(n, D_CONV), ao.reshape(n, D_Q), w["attn_out_g"],
                              w["w_out_c"], w["w_out_a"], w["g2"], w["w_router"])

    thr, need = _threshold(aff_t, cap)
    gsel, lpos, cnt = _masks(aff_t, thr, need, w["tri"])
    cnt = cnt[:, :, 0].T
    ends = jnp.cumsum(cnt, axis=1)
    start = ends - cnt
    base = start // BF16_SUBLANES * BF16_SUBLANES
    shift = start - base
    nch = jnp.maximum((shift + cnt + CH - 1) // CH, 1)
    keep = ends // BF16_SUBLANES * BF16_SUBLANES - base
    flat = lambda a: a.astype(jnp.int32).reshape(-1)
    base, shift, nch, keepc, keepr = flat(base), flat(shift), flat(nch), flat(keep // CH), flat(keep % CH)
    full_tiles = (cap + TMF - 1) // TMF
    cap_rows = (full_tiles + 1) * TMF
    ntile = jnp.full((N_EXPERTS,), full_tiles, jnp.int32)

    xe = _dispatch(base, shift, nch, keepc, keepr, xn2, gsel, lpos, cap, cap_rows)
    ye = _ffn(ntile, xe, w["wg"], w["wu"], w["wd"])
    y = _combine(base, shift, nch, h1, gsel, lpos, p.reshape(n, D_PLE), w["wpp"], w["pg"], w["wpg"],
                 w["bpg"], ye)
    return y.reshape(b, t, D_MODEL)


def kernel(x_prompt, x_sample, p_prompt, p_sample, norm1_g, w_in, conv_dw_w, conv_dw_b, conv_ln_g, conv_ln_b, q_norm_g, k_norm_g, conv_out_g, attn_out_g, w_out, norm2_g, w_router, w_gate, w_up, w_down, ple_proj, ple_norm_g, ple_gate_w, ple_gate_b):
    w = _prepare_weights(norm1_g, w_in, conv_dw_w, conv_dw_b, conv_ln_g, conv_ln_b, q_norm_g, k_norm_g,
                         conv_out_g, attn_out_g, w_out, norm2_g, w_router, w_gate, w_up, w_down,
                         ple_proj, ple_norm_g, ple_gate_w, ple_gate_b)
    y_prompt = _trunk(x_prompt, p_prompt[0], w)
    y_sample = _trunk(x_sample, p_sample[0], w)
    return (y_prompt, y_sample)
```

```python
import functools
import math

import jax
import jax.numpy as jnp
from jax import lax
from jax.experimental import pallas as pl
from jax.experimental.pallas import tpu as pltpu

D_MODEL = 1024
D_CONV = 512
CONV_WIDTH = 31
CONV_PAD = CONV_WIDTH // 2
N_HEADS = 8
N_KV_HEADS = 2
HEAD_DIM = 64
KV_GROUP = N_HEADS // N_KV_HEADS
D_Q = N_HEADS * HEAD_DIM
D_KV = N_KV_HEADS * HEAD_DIM
D_IN = 2 * D_CONV + D_Q + 2 * D_KV
HALF_ROT = HEAD_DIM // 2
ROPE_THETA = 10000.0
GRID_W = 64
N_EXPERTS = 16
CAPACITY_FACTOR = 2
D_FF_EXPERT = 2048
D_PLE = 256
EPS = 1e-6

LANES = 128
F32_SUBLANES = 8
BF16_SUBLANES = 16
VMEM_LIMIT = 56 * 1024 * 1024

TM = 512
TM_OUT = 1024
TT = 256
HALO = 16
TQ = 1024
TK = 512
KV_CHUNKS = 8
MAX_SCORE_SHIFT = 40.0
TR = 512
CH = 128
TMF = 512

F32 = jnp.float32
BF16 = jnp.bfloat16


def _cparams(sem):
    return pltpu.CompilerParams(dimension_semantics=sem, vmem_limit_bytes=VMEM_LIMIT)


def _sigmoid(x):
    return 1.0 / (1.0 + jnp.exp(-x))


def _inproj_body(x_ref, g1_ref, w_ref, qkg_ref, cos_ref, sin_ref, ones_ref, fill_ref,
                 gl_ref, q_ref, k_ref, v_ref):
    for blk in range(TM_OUT // TM):
        _inproj_rows(slice(blk * TM, (blk + 1) * TM), x_ref, g1_ref, w_ref, qkg_ref, cos_ref, sin_ref,
                     ones_ref, fill_ref, gl_ref, q_ref, k_ref, v_ref)


def _inproj_rows(rows, x_ref, g1_ref, w_ref, qkg_ref, cos_ref, sin_ref, ones_ref, fill_ref,
                 gl_ref, q_ref, k_ref, v_ref):
    x = x_ref[rows, :]
    a = x * lax.rsqrt(jnp.mean(x * x, axis=-1, keepdims=True) + EPS) * g1_ref[...]
    z = jnp.dot(a.astype(BF16), w_ref[...], preferred_element_type=F32)
    val = z[:, :D_CONV]
    gate = z[:, D_CONV:2 * D_CONV]
    gl_ref[rows, :] = (val * _sigmoid(gate)).astype(BF16)

    lane = lax.broadcasted_iota(jnp.int32, (x.shape[0], LANES), 1)
    first_half = (lane % HALF_ROT) < (HALF_ROT // 2)
    low_head = lane < HEAD_DIM
    cos = cos_ref[rows, :]
    sin = sin_ref[rows, :]
    o0 = 2 * D_CONV
    n_chunks = (D_Q + D_KV) // LANES
    for c in range(n_chunks):
        qc = z[:, o0 + c * LANES:o0 + (c + 1) * LANES]
        ssum = jnp.dot((qc * qc).astype(BF16), ones_ref[...], preferred_element_type=F32)
        qn = qc * lax.rsqrt(ssum * (1.0 / HEAD_DIM) + EPS) * qkg_ref[:, c * LANES:(c + 1) * LANES]
        partner = jnp.where(first_half, pltpu.roll(qn, LANES - HALF_ROT // 2, 1),
                            pltpu.roll(qn, HALF_ROT // 2, 1))
        qr = qn * cos + partner * sin
        fill = fill_ref[0:1, :] if c < D_Q // LANES else fill_ref[1:2, :]
        even = jnp.where(low_head, qr, fill).astype(BF16)
        odd = jnp.where(low_head, pltpu.roll(qr, HEAD_DIM, 1), fill).astype(BF16)
        if c < D_Q // LANES:
            q_ref[rows, (2 * c) * LANES:(2 * c + 1) * LANES] = even
            q_ref[rows, (2 * c + 1) * LANES:(2 * c + 2) * LANES] = odd
        else:
            k_ref[rows, 0:LANES] = even
            k_ref[rows, LANES:2 * LANES] = odd
    vv = z[:, o0 + D_Q + D_KV:]
    v_ref[0:LANES, rows] = jnp.where(low_head, vv, 1.0).T.astype(BF16)
    v_ref[LANES:2 * LANES, rows] = jnp.where(low_head, pltpu.roll(vv, HEAD_DIM, 1), 1.0).T.astype(BF16)


def _inproj(x2, g1, w_in_bf, qkg, cos_t, sin_t, ones_bd, fill, seq_len):
    n = x2.shape[0]
    pos_tiles = seq_len // TM_OUT
    row = lambda i: (i, 0)
    const = lambda i: (0, 0)
    return pl.pallas_call(
        _inproj_body,
        grid=(n // TM_OUT,),
        in_specs=[
            pl.BlockSpec((TM_OUT, D_MODEL), row),
            pl.BlockSpec((1, D_MODEL), const),
            pl.BlockSpec((D_MODEL, D_IN), const),
            pl.BlockSpec((1, D_Q + D_KV), const),
            pl.BlockSpec((TM_OUT, LANES), lambda i: (i % pos_tiles, 0)),
            pl.BlockSpec((TM_OUT, LANES), lambda i: (i % pos_tiles, 0)),
            pl.BlockSpec((LANES, LANES), const),
            pl.BlockSpec((2, LANES), const),
        ],
        out_specs=[
            pl.BlockSpec((TM_OUT, D_CONV), row),
            pl.BlockSpec((TM_OUT, N_HEADS * LANES), row),
            pl.BlockSpec((TM_OUT, N_KV_HEADS * LANES), row),
            pl.BlockSpec((N_KV_HEADS * LANES, TM_OUT), lambda i: (0, i)),
        ],
        out_shape=[
            jax.ShapeDtypeStruct((n, D_CONV), BF16),
            jax.ShapeDtypeStruct((n, N_HEADS * LANES), BF16),
            jax.ShapeDtypeStruct((n, N_KV_HEADS * LANES), BF16),
            jax.ShapeDtypeStruct((N_KV_HEADS * LANES, n), BF16),
        ],
        compiler_params=_cparams(("parallel",)),
        name="inproj",
    )(x2, g1, w_in_bf, qkg, cos_t, sin_t, ones_bd, fill)


CONV_ROWS = 64
CONV_SPAN = TT + 2 * HALO - F32_SUBLANES


def _conv_body(left_ref, main_ref, right_ref, w_ref, b_ref, lng_ref, lnb_ref, og_ref,
               out_ref, win_ref, conv_ref):
    i = pl.program_id(1)
    last = pl.num_programs(1) - 1
    left = left_ref[0].astype(F32)
    right = right_ref[0].astype(F32)
    win_ref[0, 0:HALO, :] = jnp.where(i > 0, left, 0.0)
    win_ref[0, HALO:HALO + TT, :] = main_ref[0].astype(F32)
    win_ref[0, HALO + TT:HALO + TT + HALO, :] = jnp.where(i < last, right, 0.0)
    for r in range(1, F32_SUBLANES):
        win_ref[r, 0:CONV_SPAN, :] = win_ref[0, r:r + CONV_SPAN, :]
    base = HALO - CONV_PAD

    def row_block(rr, carry):
        r0 = pl.multiple_of(rr * CONV_ROWS, CONV_ROWS)
        groups = CONV_ROWS // F32_SUBLANES
        reach = (base + CONV_WIDTH - 1) // F32_SUBLANES + 1
        for c in range(D_CONV // LANES):
            ls = slice(c * LANES, (c + 1) * LANES)
            accs = [None] * groups
            for shift in range(F32_SUBLANES):
                for q in range(groups + reach - 1):
                    uses = [(q - a, a * F32_SUBLANES + shift - base) for a in range(reach)
                            if 0 <= a * F32_SUBLANES + shift - base < CONV_WIDTH and 0 <= q - a < groups]
                    if not uses:
                        continue
                    tile = win_ref[shift, pl.ds(r0 + q * F32_SUBLANES, F32_SUBLANES), ls]
                    for g, k in uses:
                        term = tile * w_ref[k:k + 1, ls]
                        accs[g] = term if accs[g] is None else accs[g] + term
            for g in range(groups):
                conv_ref[pl.ds(r0 + g * F32_SUBLANES, F32_SUBLANES), ls] = accs[g]
        return carry

    lax.fori_loop(0, TT // CONV_ROWS, row_block, 0)
    cv = conv_ref[...] + b_ref[...]
    mu = jnp.mean(cv, axis=-1, keepdims=True)
    d = cv - mu
    var = jnp.mean(d * d, axis=-1, keepdims=True)
    y = d * lax.rsqrt(var + EPS) * lng_ref[...] + lnb_ref[...]
    y = y * _sigmoid(y)
    y = y * lax.rsqrt(jnp.mean(y * y, axis=-1, keepdims=True) + EPS) * og_ref[...]
    out_ref[0] = y.astype(BF16)


def _conv(gl3, dw_w, dw_b, ln_g, ln_b, out_g):
    b, t, _ = gl3.shape
    hb = TT // HALO
    n_halo = t // HALO
    const = lambda bb, i: (0, 0)
    return pl.pallas_call(
        _conv_body,
        grid=(b, t // TT),
        in_specs=[
            pl.BlockSpec((1, HALO, D_CONV), lambda bb, i: (bb, jnp.maximum(i * hb - 1, 0), 0)),
            pl.BlockSpec((1, TT, D_CONV), lambda bb, i: (bb, i, 0)),
            pl.BlockSpec((1, HALO, D_CONV), lambda bb, i: (bb, jnp.minimum((i + 1) * hb, n_halo - 1), 0)),
            pl.BlockSpec((CONV_WIDTH, D_CONV), const),
            pl.BlockSpec((1, D_CONV), const),
            pl.BlockSpec((1, D_CONV), const),
            pl.BlockSpec((1, D_CONV), const),
            pl.BlockSpec((1, D_CONV), const),
        ],
        out_specs=pl.BlockSpec((1, TT, D_CONV), lambda bb, i: (bb, i, 0)),
        out_shape=jax.ShapeDtypeStruct((b, t, D_CONV), BF16),
        scratch_shapes=[pltpu.VMEM((F32_SUBLANES, TT + 2 * HALO, D_CONV), F32),
                        pltpu.VMEM((TT, D_CONV), F32)],
        compiler_params=_cparams(("parallel", "parallel")),
        name="conv",
    )(gl3, gl3, gl3, dw_w, dw_b, ln_g, ln_b, out_g)


def _attn_body(q_ref, k_ref, vt_ref, o_ref, qs_ref, m_ref, acc_ref):
    t = k_ref.shape[1]
    nt_dims = (((1,), (1,)), ((), ()))
    for h in range(KV_GROUP):
        qs_ref[h * TQ:(h + 1) * TQ, :] = q_ref[0, :, h * LANES:(h + 1) * LANES]
    m_ref[...] = jnp.full(m_ref.shape, -jnp.inf, F32)
    acc_ref[...] = jnp.zeros(acc_ref.shape, F32)

    def step(kt, carry):
        start = pl.multiple_of(kt * TK, TK)
        s = lax.dot_general(qs_ref[...], k_ref[0, pl.ds(start, TK), :], nt_dims,
                            preferred_element_type=F32)
        m_old = m_ref[...]
        m_new = jnp.maximum(m_old, jnp.max(s, axis=-1, keepdims=True))
        p = jnp.exp2(s - m_new[:, 0:1])
        alpha = jnp.exp2(m_old - m_new)
        acc_ref[...] = alpha * acc_ref[...] + lax.dot_general(
            p.astype(BF16), vt_ref[:, pl.ds(start, TK)], nt_dims, preferred_element_type=F32)
        m_ref[...] = m_new
        return carry

    lax.fori_loop(0, t // TK, step, 0)
    acc = acc_ref[...]
    res = acc / pltpu.roll(acc, HEAD_DIM, 1)
    _attn_store(o_ref, [res[h * TQ:(h + 1) * TQ, :] for h in range(KV_GROUP)])


def _attn_shifted_body(q_ref, k_ref, vt_ref, o_ref, qs_ref, acc_ref):
    t = k_ref.shape[1]
    for h in range(KV_GROUP):
        qs_ref[h * TQ:(h + 1) * TQ, :] = q_ref[0, :, h * LANES:(h + 1) * LANES]
    acc_ref[...] = jnp.zeros(acc_ref.shape, F32)

    chunks = math.gcd(KV_CHUNKS, t // TK)

    def step(kt, carry):
        qs = qs_ref[...]
        part = None
        for c in range(chunks):
            start = pl.multiple_of(kt * (chunks * TK) + c * TK, TK)
            st = lax.dot_general(k_ref[0, pl.ds(start, TK), :], qs, (((1,), (1,)), ((), ())),
                                 preferred_element_type=F32)
            pv = jnp.dot(vt_ref[:, pl.ds(start, TK)], jnp.exp2(st).astype(BF16),
                         preferred_element_type=F32)
            part = pv if part is None else part + pv
        acc_ref[...] += part
        return carry

    lax.fori_loop(0, t // (chunks * TK), step, 0)
    acc = acc_ref[...]
    res = acc * (1.0 / acc[HEAD_DIM:HEAD_DIM + 1, :])
    _attn_store(o_ref, [res[:, h * TQ:(h + 1) * TQ].T for h in range(KV_GROUP)])


def _attn_store(o_ref, heads):
    lane = lax.broadcasted_iota(jnp.int32, (TQ, LANES), 1)
    low = lane < HEAD_DIM
    for hp in range(KV_GROUP // 2):
        o_ref[0, :, hp * LANES:(hp + 1) * LANES] = jnp.where(
            low, heads[2 * hp], pltpu.roll(heads[2 * hp + 1], HEAD_DIM, 1)).astype(BF16)


def _attention_call(body, name, scratch, q3, k3, vt):
    b, t, _ = q3.shape
    gw = KV_GROUP * LANES
    return pl.pallas_call(
        body,
        grid=(b, N_KV_HEADS, t // TQ),
        in_specs=[
            pl.BlockSpec((1, TQ, gw), lambda bb, j, i: (bb, i, j)),
            pl.BlockSpec((1, t, LANES), lambda bb, j, i: (bb, 0, j)),
            pl.BlockSpec((LANES, t), lambda bb, j, i: (j, bb)),
        ],
        out_specs=pl.BlockSpec((1, TQ, KV_GROUP * HEAD_DIM), lambda bb, j, i: (bb, i, j)),
        out_shape=jax.ShapeDtypeStruct((b, t, D_Q), BF16),
        scratch_shapes=[pltpu.VMEM((KV_GROUP * TQ, LANES), BF16)] + scratch,
        compiler_params=_cparams(("parallel", "parallel", "parallel")),
        name=name,
    )(q3, k3, vt)


def _attention_shifted(q3, k3, vt):
    return _attention_call(_attn_shifted_body, "attention_shifted",
                           [pltpu.VMEM((LANES, KV_GROUP * TQ), F32)], q3, k3, vt)


def _attention(q3, k3, vt):
    return _attention_call(_attn_body, "attention",
                           [pltpu.VMEM((KV_GROUP * TQ, LANES), F32),
                            pltpu.VMEM((KV_GROUP * TQ, LANES), F32)], q3, k3, vt)


def _outproj_body(x_ref, cn_ref, ao_ref, ag_ref, wc_ref, wa_ref, g2_ref, wr_ref,
                  h_ref, xn_ref, aff_ref):
    for blk in range(TM_OUT // TM):
        rows = slice(blk * TM, (blk + 1) * TM)
        ao = ao_ref[rows, :].astype(F32)
        an = ao * lax.rsqrt(jnp.mean(ao * ao, axis=-1, keepdims=True) + EPS) * ag_ref[...]
        h = (x_ref[rows, :]
             + jnp.dot(cn_ref[rows, :], wc_ref[...], preferred_element_type=F32)
             + jnp.dot(an.astype(BF16), wa_ref[...], preferred_element_type=F32))
        h_ref[rows, :] = h
        xn = h * lax.rsqrt(jnp.mean(h * h, axis=-1, keepdims=True) + EPS) * g2_ref[...]
        xn_hi = xn.astype(BF16)
        xn_ref[rows, :] = xn_hi
        xn_lo = (xn - xn_hi.astype(F32)).astype(BF16)
        parts = (jnp.dot(xn_hi, wr_ref[...], preferred_element_type=F32)
                 + jnp.dot(xn_lo, wr_ref[...], preferred_element_type=F32))
        parts_t = parts.T
        logits = parts_t[0:N_EXPERTS, :] + parts_t[N_EXPERTS:2 * N_EXPERTS, :]
        mx = jnp.max(logits, axis=0, keepdims=True)
        ex = jnp.exp(logits - mx)
        aff_ref[:, rows] = ex / jnp.sum(ex, axis=0, keepdims=True)


def _outproj(x2, cn, ao, ag, wc, wa, g2, wr):
    n = x2.shape[0]
    row = lambda i: (i, 0)
    const = lambda i: (0, 0)
    return pl.pallas_call(
        _outproj_body,
        grid=(n // TM_OUT,),
        in_specs=[
            pl.BlockSpec((TM_OUT, D_MODEL), row),
            pl.BlockSpec((TM_OUT, D_CONV), row),
            pl.BlockSpec((TM_OUT, D_Q), row),
            pl.BlockSpec((1, D_Q), const),
            pl.BlockSpec((D_CONV, D_MODEL), const),
            pl.BlockSpec((D_Q, D_MODEL), const),
            pl.BlockSpec((1, D_MODEL), const),
            pl.BlockSpec((D_MODEL, LANES), const),
        ],
        out_specs=[
            pl.BlockSpec((TM_OUT, D_MODEL), row),
            pl.BlockSpec((TM_OUT, D_MODEL), row),
            pl.BlockSpec((N_EXPERTS, TM_OUT), lambda i: (0, i)),
        ],
        out_shape=[
            jax.ShapeDtypeStruct((n, D_MODEL), F32),
            jax.ShapeDtypeStruct((n, D_MODEL), BF16),
            jax.ShapeDtypeStruct((N_EXPERTS, n), F32),
        ],
        compiler_params=_cparams(("parallel",)),
        name="outproj",
    )(x2, cn, ao, ag, wc, wa, g2, wr)


def _threshold_body(cap, aff_ref, thr_ref, need_ref):
    def step(it, lo):
        cand = lo | (jnp.int32(1) << (30 - it))
        bits = pltpu.bitcast(aff_ref[...], jnp.int32)
        cnt = jnp.sum((bits >= cand).astype(jnp.int32), axis=1, keepdims=True)
        return jnp.where(cnt >= cap, cand, lo)

    thr = lax.fori_loop(0, 31, step, jnp.zeros((N_EXPERTS, 1), jnp.int32))
    bits = pltpu.bitcast(aff_ref[...], jnp.int32)
    n_gt = jnp.sum((bits > thr).astype(jnp.int32), axis=1, keepdims=True)
    thr_ref[...] = jnp.broadcast_to(thr, thr_ref.shape)
    need_ref[...] = jnp.broadcast_to(cap - n_gt, need_ref.shape)


def _threshold(aff_t, cap):
    n = aff_t.shape[1]
    full = lambda: (0, 0)
    return pl.pallas_call(
        functools.partial(_threshold_body, cap),
        in_specs=[pl.BlockSpec((N_EXPERTS, n), full)],
        out_specs=[pl.BlockSpec((N_EXPERTS, LANES), full), pl.BlockSpec((N_EXPERTS, LANES), full)],
        out_shape=[jax.ShapeDtypeStruct((N_EXPERTS, LANES), jnp.int32),
                   jax.ShapeDtypeStruct((N_EXPERTS, LANES), jnp.int32)],
        compiler_params=pltpu.CompilerParams(vmem_limit_bytes=VMEM_LIMIT),
        name="threshold",
    )(aff_t)


def _mask_body(aff_ref, thr_ref, need_ref, tri_ref, gsel_ref, lpos_ref, cnt_ref, eqc_ref):
    @pl.when(pl.program_id(0) == 0)
    def _():
        eqc_ref[...] = jnp.zeros(eqc_ref.shape, F32)

    aff = aff_ref[...]
    bits = pltpu.bitcast(aff, jnp.int32)
    thr = thr_ref[:, 0:1]
    need = need_ref[:, 0:1].astype(F32)
    gt = bits > thr
    eq = bits == thr
    eq_f = jnp.where(eq, 1.0, 0.0)
    eq_rank = eqc_ref[:, 0:1] + jnp.dot(eq_f.astype(BF16), tri_ref[...], preferred_element_type=F32)
    sel = gt | (eq & (eq_rank < need))
    sel_f = jnp.where(sel, 1.0, 0.0)
    gsel_ref[...] = jnp.where(sel, aff, -1.0)
    lpos_ref[...] = jnp.dot(sel_f.astype(BF16), tri_ref[...], preferred_element_type=F32)
    cnt = jnp.sum(sel_f, axis=1, keepdims=True)
    cnt_ref[0] = jnp.broadcast_to(cnt, (N_EXPERTS, LANES)).astype(jnp.int32)
    eqc_ref[...] = eqc_ref[...] + jnp.sum(eq_f, axis=1, keepdims=True)


def _masks(aff_t, thr, need, tri):
    n = aff_t.shape[1]
    nt = n // TR
    const = lambda i: (0, 0)
    tile = lambda i: (0, i)
    return pl.pallas_call(
        _mask_body,
        grid=(nt,),
        in_specs=[
            pl.BlockSpec((N_EXPERTS, TR), tile),
            pl.BlockSpec((N_EXPERTS, LANES), const),
            pl.BlockSpec((N_EXPERTS, LANES), const),
            pl.BlockSpec((TR, TR), const),
        ],
        out_specs=[
            pl.BlockSpec((N_EXPERTS, TR), tile),
            pl.BlockSpec((N_EXPERTS, TR), tile),
            pl.BlockSpec((1, N_EXPERTS, LANES), lambda i: (i, 0, 0)),
        ],
        out_shape=[
            jax.ShapeDtypeStruct((N_EXPERTS, n), F32),
            jax.ShapeDtypeStruct((N_EXPERTS, n), F32),
            jax.ShapeDtypeStruct((nt, N_EXPERTS, LANES), jnp.int32),
        ],
        scratch_shapes=[pltpu.VMEM((N_EXPERTS, LANES), F32)],
        compiler_params=_cparams(("arbitrary",)),
        name="masks",
    )(aff_t, thr, need, tri)


def _onehot_rows(gsel_row, lpos_row, first_slot, value):
    slot = lax.broadcasted_iota(jnp.int32, (CH, TR), 0).astype(F32) + first_slot.astype(F32)
    return jnp.where((gsel_row >= 0.0) & (lpos_row == slot), value, 0.0).astype(BF16)


def _dispatch_body(cap, base_ref, shift_ref, nch_ref, keepc_ref, keepr_ref, xn_ref, gsel_ref, lpos_ref,
                   xe_ref, stage_ref, carry_ref, hot_ref, sem_ref, xsem_ref):
    n = pl.program_id(0)
    nt = pl.num_programs(0)

    def copy(e, slot, sem, chunk):
        row0 = pl.multiple_of(base_ref[e * nt + n] + chunk * CH, BF16_SUBLANES)
        return pltpu.make_async_copy(stage_ref.at[slot], xe_ref.at[e, pl.ds(row0, CH)], sem)

    @pl.when(n == 0)
    def _():
        carry_ref[...] = jnp.zeros(carry_ref.shape, BF16)
        stage_ref[N_EXPERTS] = jnp.zeros((CH, D_MODEL), BF16)
        rows_total = xe_ref.shape[1]
        n_fill = -(-(rows_total - cap) // CH)
        starts = [rows_total - (j + 1) * CH for j in range(n_fill)]
        fills = [pltpu.make_async_copy(stage_ref.at[N_EXPERTS], xe_ref.at[e, pl.ds(row0, CH)], xsem_ref.at[0])
                 for e in range(N_EXPERTS) for row0 in starts]
        for cp in fills:
            cp.start()
        for cp in fills:
            cp.wait()

    def rows_of(e, chunk):
        hot = _onehot_rows(gsel_ref[pl.ds(e, 1), :], lpos_ref[pl.ds(e, 1), :],
                           chunk * CH - shift_ref[e * nt + n], 1.0)
        return jnp.dot(hot, xn_ref[...], preferred_element_type=F32)

    def kept_group(e, slot):
        row0 = pl.multiple_of(keepr_ref[e * nt + n], BF16_SUBLANES)
        return stage_ref[slot, pl.ds(row0, BF16_SUBLANES), :]

    for e in range(N_EXPERTS):
        hot_ref[e * CH:(e + 1) * CH, :] = _onehot_rows(gsel_ref[e:e + 1, :], lpos_ref[e:e + 1, :],
                                                       -shift_ref[e * nt + n], 1.0)
    rows_all = jnp.dot(hot_ref[...], xn_ref[...], preferred_element_type=F32)

    @pl.when(n > 0)
    def _():
        for e in range(N_EXPERTS):
            copy(e, e, sem_ref.at[e], 0).wait()

    for e in range(N_EXPERTS):
        rows = rows_all[e * CH:(e + 1) * CH]
        stage_ref[e, 0:BF16_SUBLANES, :] = (rows[0:BF16_SUBLANES] + carry_ref[e].astype(F32)).astype(BF16)
        stage_ref[e, BF16_SUBLANES:CH, :] = rows[BF16_SUBLANES:CH].astype(BF16)
        copy(e, e, sem_ref.at[e], 0).start()
        carry_ref[e] = jnp.where(keepc_ref[e * nt + n] == 0, kept_group(e, e), jnp.zeros((), BF16))

    @pl.when(n == nt - 1)
    def _():
        for e in range(N_EXPERTS):
            copy(e, e, sem_ref.at[e], 0).wait()

    def per_expert(e, carry):
        def per_chunk(chunk, c):
            stage_ref[N_EXPERTS] = rows_of(e, chunk).astype(BF16)
            cp = copy(e, N_EXPERTS, xsem_ref.at[0], chunk)
            cp.start()

            @pl.when(keepc_ref[e * nt + n] == chunk)
            def _():
                carry_ref[e] = kept_group(e, N_EXPERTS)
            cp.wait()
            return c
        return lax.fori_loop(1, nch_ref[e * nt + n], per_chunk, carry)

    lax.fori_loop(0, N_EXPERTS, per_expert, 0)


def _dispatch(base, shift, nch, keepc, keepr, xn2, gsel, lpos, cap, cap_rows):
    n = xn2.shape[0]
    nt = n // TR
    grid_spec = pltpu.PrefetchScalarGridSpec(
        num_scalar_prefetch=5,
        grid=(nt,),
        in_specs=[
            pl.BlockSpec((TR, D_MODEL), lambda i, *_: (i, 0)),
            pl.BlockSpec((N_EXPERTS, TR), lambda i, *_: (0, i)),
            pl.BlockSpec((N_EXPERTS, TR), lambda i, *_: (0, i)),
        ],
        out_specs=pl.BlockSpec(memory_space=pl.ANY),
        scratch_shapes=[pltpu.VMEM((N_EXPERTS + 1, CH, D_MODEL), BF16),
                        pltpu.VMEM((N_EXPERTS, BF16_SUBLANES, D_MODEL), BF16),
                        pltpu.VMEM((N_EXPERTS * CH, TR), BF16),
                        pltpu.SemaphoreType.DMA((N_EXPERTS,)),
                        pltpu.SemaphoreType.DMA((1,))],
    )
    return pl.pallas_call(
        functools.partial(_dispatch_body, cap),
        grid_spec=grid_spec,
        out_shape=jax.ShapeDtypeStruct((N_EXPERTS, cap_rows, D_MODEL), BF16),
        compiler_params=_cparams(("arbitrary",)),
        name="dispatch",
    )(base, shift, nch, keepc, keepr, xn2, gsel, lpos)


def _ffn_body(ntile_ref, x_ref, wg_ref, wu_ref, wd_ref, y_ref):
    e = pl.program_id(0)
    i = pl.program_id(1)

    @pl.when(i < ntile_ref[e])
    def _():
        x = x_ref[0]
        g = jnp.dot(x, wg_ref[0], preferred_element_type=F32)
        u = jnp.dot(x, wu_ref[0], preferred_element_type=F32)
        hid = (g * _sigmoid(g) * u).astype(BF16)
        y_ref[0] = jnp.dot(hid, wd_ref[0], preferred_element_type=F32).astype(BF16)

    @pl.when(i >= ntile_ref[e])
    def _():
        y_ref[...] = jnp.zeros(y_ref.shape, BF16)


def _ffn(ntile, xe, wg, wu, wd):
    cap_rows = xe.shape[1]
    rows = lambda e, i, nt_ref: (e, jnp.minimum(i, nt_ref[e] - 1), 0)
    wmap = lambda e, i, nt_ref: (e, 0, 0)
    grid_spec = pltpu.PrefetchScalarGridSpec(
        num_scalar_prefetch=1,
        grid=(N_EXPERTS, cap_rows // TMF),
        in_specs=[
            pl.BlockSpec((1, TMF, D_MODEL), rows),
            pl.BlockSpec((1, D_MODEL, D_FF_EXPERT), wmap),
            pl.BlockSpec((1, D_MODEL, D_FF_EXPERT), wmap),
            pl.BlockSpec((1, D_FF_EXPERT, D_MODEL), wmap),
        ],
        out_specs=pl.BlockSpec((1, TMF, D_MODEL), lambda e, i, nt_ref: (e, i, 0)),
    )
    return pl.pallas_call(
        _ffn_body,
        grid_spec=grid_spec,
        out_shape=jax.ShapeDtypeStruct(xe.shape, BF16),
        compiler_params=_cparams(("arbitrary", "arbitrary")),
        name="expert_ffn",
    )(ntile, xe, wg, wu, wd)


def _combine_body(base_ref, shift_ref, nch_ref, h_ref, gsel_ref, lpos_ref, p_ref, wpp_ref, pg_ref,
                  wpg_ref, bpg_ref, ye_ref, y_ref, ybuf_ref, xbuf_ref, hot_ref, acc_ref, sem_ref, xsem_ref):
    n = pl.program_id(0)
    nt = pl.num_programs(0)
    tn = (((0,), (0,)), ((), ()))
    cur = n % 2

    def first_chunk(e, step, buf):
        row0 = pl.multiple_of(base_ref[e * nt + step], BF16_SUBLANES)
        return pltpu.make_async_copy(ye_ref.at[e, pl.ds(row0, CH)], ybuf_ref.at[buf, pl.ds(e * CH, CH)],
                                     sem_ref.at[buf, e])

    def later_chunk(e, chunk):
        row0 = pl.multiple_of(base_ref[e * nt + n] + chunk * CH, BF16_SUBLANES)
        return pltpu.make_async_copy(ye_ref.at[e, pl.ds(row0, CH)], xbuf_ref, xsem_ref.at[0])

    def gated_hot(e, chunk):
        g = gsel_ref[pl.ds(e, 1), :]
        return _onehot_rows(g, lpos_ref[pl.ds(e, 1), :], chunk * CH - shift_ref[e * nt + n], g)

    @pl.when(n == 0)
    def _():
        for e in range(N_EXPERTS):
            first_chunk(e, 0, 0).start()

    @pl.when(n + 1 < nt)
    def _():
        for e in range(N_EXPERTS):
            first_chunk(e, n + 1, 1 - cur).start()

    for e in range(N_EXPERTS):
        hot_ref[e * CH:(e + 1) * CH, :] = gated_hot(e, jnp.int32(0))
    for e in range(N_EXPERTS):
        first_chunk(e, n, cur).wait()
    acc_ref[...] = h_ref[...] + lax.dot_general(hot_ref[...], ybuf_ref[cur], tn, preferred_element_type=F32)

    def per_expert(e, carry):
        def per_chunk(chunk, c):
            cp = later_chunk(e, chunk)
            cp.start()
            cp.wait()
            acc_ref[...] += lax.dot_general(gated_hot(e, chunk), xbuf_ref[...], tn, preferred_element_type=F32)
            return c
        return lax.fori_loop(1, nch_ref[e * nt + n], per_chunk, carry)

    lax.fori_loop(0, N_EXPERTS, per_expert, 0)

    h2 = acc_ref[...]
    emb = jnp.dot(p_ref[...].astype(BF16), wpp_ref[...], preferred_element_type=F32)
    hn = h2 * lax.rsqrt(jnp.mean(h2 * h2, axis=-1, keepdims=True) + EPS) * pg_ref[...]
    gate = _sigmoid(jnp.dot(hn.astype(BF16), wpg_ref[...], preferred_element_type=F32) + bpg_ref[...])
    y_ref[...] = h2 + gate * emb


def _combine(base, shift, nch, h1, gsel, lpos, p2, wpp, pg, wpg, bpg, ye):
    n = h1.shape[0]
    nt = n // TR
    row = lambda i, *_: (i, 0)
    tile = lambda i, *_: (0, i)
    const = lambda i, *_: (0, 0)
    grid_spec = pltpu.PrefetchScalarGridSpec(
        num_scalar_prefetch=3,
        grid=(nt,),
        in_specs=[
            pl.BlockSpec((TR, D_MODEL), row),
            pl.BlockSpec((N_EXPERTS, TR), tile),
            pl.BlockSpec((N_EXPERTS, TR), tile),
            pl.BlockSpec((TR, D_PLE), row),
            pl.BlockSpec((D_PLE, D_MODEL), const),
            pl.BlockSpec((1, D_MODEL), const),
            pl.BlockSpec((D_MODEL, D_MODEL), const),
            pl.BlockSpec((1, D_MODEL), const),
            pl.BlockSpec(memory_space=pl.ANY),
        ],
        out_specs=pl.BlockSpec((TR, D_MODEL), row),
        scratch_shapes=[pltpu.VMEM((2, N_EXPERTS * CH, D_MODEL), BF16),
                        pltpu.VMEM((CH, D_MODEL), BF16),
                        pltpu.VMEM((N_EXPERTS * CH, TR), BF16),
                        pltpu.VMEM((TR, D_MODEL), F32),
                        pltpu.SemaphoreType.DMA((2, N_EXPERTS)),
                        pltpu.SemaphoreType.DMA((1,))],
    )
    return pl.pallas_call(
        _combine_body,
        grid_spec=grid_spec,
        out_shape=jax.ShapeDtypeStruct((n, D_MODEL), F32),
        compiler_params=_cparams(("arbitrary",)),
        name="combine",
    )(base, shift, nch, h1, gsel, lpos, p2, wpp, pg, wpg, bpg, ye)


def _rope_tables(t):
    rows = t // GRID_W
    row_idx = jnp.repeat(jnp.arange(rows, dtype=F32), GRID_W)
    col_idx = jnp.tile(jnp.arange(GRID_W, dtype=F32), rows)
    freqs = 1.0 / (ROPE_THETA ** (jnp.arange(0, HALF_ROT, 2, dtype=F32) / HALF_ROT))
    ang_r = row_idx[:, None] * freqs[None, :]
    ang_c = col_idx[:, None] * freqs[None, :]
    cr, sr, cc, sc = jnp.cos(ang_r), jnp.sin(ang_r), jnp.cos(ang_c), jnp.sin(ang_c)
    cos_h = jnp.concatenate([cr, cr, cc, cc], axis=-1)
    sin_h = jnp.concatenate([-sr, sr, -sc, sc], axis=-1)
    return jnp.tile(cos_h, (1, LANES // HEAD_DIM)), jnp.tile(sin_h, (1, LANES // HEAD_DIM))


def _prepare_weights(norm1_g, w_in, conv_dw_w, conv_dw_b, conv_ln_g, conv_ln_b, q_norm_g, k_norm_g,
                     conv_out_g, attn_out_g, w_out, norm2_g, w_router, w_gate, w_up, w_down,
                     ple_proj, ple_norm_g, ple_gate_w, ple_gate_b):
    i = 0
    q_scale = (HEAD_DIM ** -0.5) * math.log2(math.e)
    qkg = jnp.concatenate([jnp.tile(q_norm_g[i] * q_scale, N_HEADS), jnp.tile(k_norm_g[i], N_KV_HEADS)])
    bound = HEAD_DIM * jnp.max(jnp.abs(q_norm_g[i] * q_scale)) * jnp.max(jnp.abs(k_norm_g[i]))
    use_shift = bound <= MAX_SCORE_SHIFT
    lane = jnp.arange(LANES)
    fill = jnp.stack([jnp.where(lane == HEAD_DIM, 1.0, 0.0),
                      jnp.where(lane == HEAD_DIM, -jnp.where(use_shift, bound, 0.0), 0.0)]).astype(F32)
    ones_bd = (lane[:, None] // HEAD_DIM == lane[None, :] // HEAD_DIM).astype(BF16)
    tok = jnp.arange(TR)
    wr_hi = w_router[i].astype(BF16)
    wr_lo = (w_router[i] - wr_hi.astype(F32)).astype(BF16)
    wr = jnp.concatenate([wr_hi, wr_lo, jnp.zeros((D_MODEL, LANES - 2 * N_EXPERTS), BF16)], axis=1)
    return dict(
        g1=norm1_g[i][None, :], w_in=w_in[i].astype(BF16), qkg=qkg[None, :], ones_bd=ones_bd,
        fill=fill, use_shift=use_shift,
        dw_w=conv_dw_w[i], dw_b=conv_dw_b[i][None, :], ln_g=conv_ln_g[i][None, :], ln_b=conv_ln_b[i][None, :],
        conv_out_g=conv_out_g[i][None, :], attn_out_g=attn_out_g[i][None, :],
        w_out_c=w_out[i, :D_CONV].astype(BF16), w_out_a=w_out[i, D_CONV:].astype(BF16),
        g2=norm2_g[i][None, :], w_router=wr,
        wg=w_gate[i].astype(BF16), wu=w_up[i].astype(BF16), wd=w_down[i].astype(BF16),
        tri=(tok[:, None] < tok[None, :]).astype(BF16),
        wpp=ple_proj[i].astype(BF16), pg=ple_norm_g[i][None, :], wpg=ple_gate_w[i].astype(BF16),
        bpg=ple_gate_b[i][None, :],
    )


def _trunk(x, p, w):
    b, t, _ = x.shape
    n = b * t
    assert t % TM_OUT == 0 and t % TK == 0 and t % TT == 0 and t % TQ == 0 and n % TR == 0
    cap = CAPACITY_FACTOR * n // N_EXPERTS
    nt = n // TR
    x2 = x.reshape(n, D_MODEL)
    cos_t, sin_t = _rope_tables(t)

    gl, q, k, v = _inproj(x2, w["g1"], w["w_in"], w["qkg"], cos_t, sin_t, w["ones_bd"], w["fill"], t)
    cn = _conv(gl.reshape(b, t, D_CONV), w["dw_w"], w["dw_b"], w["ln_g"], w["ln_b"], w["conv_out_g"])
    ao = lax.cond(w["use_shift"], _attention_shifted, _attention,
                  q.reshape(b, t, -1), k.reshape(b, t, -1), v)
    h1, xn2, aff_t = _outproj(x2, cn.reshape(n, D_CONV), ao.reshape(n, D_Q), w["attn_out_g"],
                              w["w_out_c"], w["w_out_a"], w["g2"], w["w_router"])

    thr, need = _threshold(aff_t, cap)
    gsel, lpos, cnt = _masks(aff_t, thr, need, w["tri"])
    cnt = cnt[:, :, 0].T
    ends = jnp.cumsum(cnt, axis=1)
    start = ends - cnt
    base = start // BF16_SUBLANES * BF16_SUBLANES
    shift = start - base
    nch = jnp.maximum((shift + cnt + CH - 1) // CH, 1)
    keep = ends // BF16_SUBLANES * BF16_SUBLANES - base
    flat = lambda a: a.astype(jnp.int32).reshape(-1)
    base, shift, nch, keepc, keepr = flat(base), flat(shift), flat(nch), flat(keep // CH), flat(keep % CH)
    full_tiles = (cap + TMF - 1) // TMF
    cap_rows = (full_tiles + 1) * TMF
    ntile = jnp.full((N_EXPERTS,), full_tiles, jnp.int32)

    xe = _dispatch(base, shift, nch, keepc, keepr, xn2, gsel, lpos, cap, cap_rows)
    ye = _ffn(ntile, xe, w["wg"], w["wu"], w["wd"])
    y = _combine(base, shift, nch, h1, gsel, lpos, p.reshape(n, D_PLE), w["wpp"], w["pg"], w["wpg"],
                 w["bpg"], ye)
    return y.reshape(b, t, D_MODEL)


def kernel(x_prompt, x_sample, p_prompt, p_sample, norm1_g, w_in, conv_dw_w, conv_dw_b, conv_ln_g, conv_ln_b, q_norm_g, k_norm_g, conv_out_g, attn_out_g, w_out, norm2_g, w_router, w_gate, w_up, w_down, ple_proj, ple_norm_g, ple_gate_w, ple_gate_b):
    w = _prepare_weights(norm1_g, w_in, conv_dw_w, conv_dw_b, conv_ln_g, conv_ln_b, q_norm_g, k_norm_g,
                         conv_out_g, attn_out_g, w_out, norm2_g, w_router, w_gate, w_up, w_down,
                         ple_proj, ple_norm_g, ple_gate_w, ple_gate_b)
    y_prompt = _trunk(x_prompt, p_prompt[0], w)
    y_sample = _trunk(x_sample, p_sample[0], w)
    return (y_prompt, y_sample)
```

```python
import functools
import math

import jax
import jax.numpy as jnp
from jax import lax
from jax.experimental import pallas as pl
from jax.experimental.pallas import tpu as pltpu

D_MODEL = 1024
D_CONV = 512
CONV_WIDTH = 31
CONV_PAD = CONV_WIDTH // 2
N_HEADS = 8
N_KV_HEADS = 2
HEAD_DIM = 64
KV_GROUP = N_HEADS // N_KV_HEADS
D_Q = N_HEADS * HEAD_DIM
D_KV = N_KV_HEADS * HEAD_DIM
D_IN = 2 * D_CONV + D_Q + 2 * D_KV
HALF_ROT = HEAD_DIM // 2
ROPE_THETA = 10000.0
GRID_W = 64
N_EXPERTS = 16
CAPACITY_FACTOR = 2
D_FF_EXPERT = 2048
D_PLE = 256
EPS = 1e-6

LANES = 128
F32_SUBLANES = 8
BF16_SUBLANES = 16
VMEM_LIMIT = 56 * 1024 * 1024

TM = 512
TM_OUT = 1024
TT = 256
HALO = 16
TQ = 1024
TK = 512
KV_CHUNKS = 8
MAX_SCORE_SHIFT = 40.0
TR = 512
CH = 128
TMF = 512

F32 = jnp.float32
BF16 = jnp.bfloat16


def _cparams(sem):
    return pltpu.CompilerParams(dimension_semantics=sem, vmem_limit_bytes=VMEM_LIMIT)


def _sigmoid(x):
    return 1.0 / (1.0 + jnp.exp(-x))


def _inproj_body(x_ref, g1_ref, w_ref, qkg_ref, cos_ref, sin_ref, ones_ref, fill_ref,
                 gl_ref, q_ref, k_ref, v_ref):
    for blk in range(TM_OUT // TM):
        _inproj_rows(slice(blk * TM, (blk + 1) * TM), x_ref, g1_ref, w_ref, qkg_ref, cos_ref, sin_ref,
                     ones_ref, fill_ref, gl_ref, q_ref, k_ref, v_ref)


def _inproj_rows(rows, x_ref, g1_ref, w_ref, qkg_ref, cos_ref, sin_ref, ones_ref, fill_ref,
                 gl_ref, q_ref, k_ref, v_ref):
    x = x_ref[rows, :]
    a = x * lax.rsqrt(jnp.mean(x * x, axis=-1, keepdims=True) + EPS) * g1_ref[...]
    z = jnp.dot(a.astype(BF16), w_ref[...], preferred_element_type=F32)
    val = z[:, :D_CONV]
    gate = z[:, D_CONV:2 * D_CONV]
    gl_ref[rows, :] = (val * _sigmoid(gate)).astype(BF16)

    lane = lax.broadcasted_iota(jnp.int32, (x.shape[0], LANES), 1)
    first_half = (lane % HALF_ROT) < (HALF_ROT // 2)
    low_head = lane < HEAD_DIM
    cos = cos_ref[rows, :]
    sin = sin_ref[rows, :]
    o0 = 2 * D_CONV
    n_chunks = (D_Q + D_KV) // LANES
    for c in range(n_chunks):
        qc = z[:, o0 + c * LANES:o0 + (c + 1) * LANES]
        ssum = jnp.dot((qc * qc).astype(BF16), ones_ref[...], preferred_element_type=F32)
        qn = qc * lax.rsqrt(ssum * (1.0 / HEAD_DIM) + EPS) * qkg_ref[:, c * LANES:(c + 1) * LANES]
        partner = jnp.where(first_half, pltpu.roll(qn, LANES - HALF_ROT // 2, 1),
                            pltpu.roll(qn, HALF_ROT // 2, 1))
        qr = qn * cos + partner * sin
        fill = fill_ref[0:1, :] if c < D_Q // LANES else fill_ref[1:2, :]
        even = jnp.where(low_head, qr, fill).astype(BF16)
        odd = jnp.where(low_head, pltpu.roll(qr, HEAD_DIM, 1), fill).astype(BF16)
        if c < D_Q // LANES:
            q_ref[rows, (2 * c) * LANES:(2 * c + 1) * LANES] = even
            q_ref[rows, (2 * c + 1) * LANES:(2 * c + 2) * LANES] = odd
        else:
            k_ref[rows, 0:LANES] = even
            k_ref[rows, LANES:2 * LANES] = odd
    vv = z[:, o0 + D_Q + D_KV:]
    v_ref[0:LANES, rows] = jnp.where(low_head, vv, 1.0).T.astype(BF16)
    v_ref[LANES:2 * LANES, rows] = jnp.where(low_head, pltpu.roll(vv, HEAD_DIM, 1), 1.0).T.astype(BF16)


def _inproj(x2, g1, w_in_bf, qkg, cos_t, sin_t, ones_bd, fill, seq_len):
    n = x2.shape[0]
    pos_tiles = seq_len // TM_OUT
    row = lambda i: (i, 0)
    const = lambda i: (0, 0)
    return pl.pallas_call(
        _inproj_body,
        grid=(n // TM_OUT,),
        in_specs=[
            pl.BlockSpec((TM_OUT, D_MODEL), row),
            pl.BlockSpec((1, D_MODEL), const),
            pl.BlockSpec((D_MODEL, D_IN), const),
            pl.BlockSpec((1, D_Q + D_KV), const),
            pl.BlockSpec((TM_OUT, LANES), lambda i: (i % pos_tiles, 0)),
            pl.BlockSpec((TM_OUT, LANES), lambda i: (i % pos_tiles, 0)),
            pl.BlockSpec((LANES, LANES), const),
            pl.BlockSpec((2, LANES), const),
        ],
        out_specs=[
            pl.BlockSpec((TM_OUT, D_CONV), row),
            pl.BlockSpec((TM_OUT, N_HEADS * LANES), row),
            pl.BlockSpec((TM_OUT, N_KV_HEADS * LANES), row),
            pl.BlockSpec((N_KV_HEADS * LANES, TM_OUT), lambda i: (0, i)),
        ],
        out_shape=[
            jax.ShapeDtypeStruct((n, D_CONV), BF16),
            jax.ShapeDtypeStruct((n, N_HEADS * LANES), BF16),
            jax.ShapeDtypeStruct((n, N_KV_HEADS * LANES), BF16),
            jax.ShapeDtypeStruct((N_KV_HEADS * LANES, n), BF16),
        ],
        compiler_params=_cparams(("parallel",)),
        name="inproj",
    )(x2, g1, w_in_bf, qkg, cos_t, sin_t, ones_bd, fill)


CONV_ROWS = 64
CONV_SPAN = TT + 2 * HALO - F32_SUBLANES


def _conv_body(left_ref, main_ref, right_ref, w_ref, b_ref, lng_ref, lnb_ref, og_ref,
               out_ref, win_ref, conv_ref):
    i = pl.program_id(1)
    last = pl.num_programs(1) - 1
    left = left_ref[0].astype(F32)
    right = right_ref[0].astype(F32)
    win_ref[0, 0:HALO, :] = jnp.where(i > 0, left, 0.0)
    win_ref[0, HALO:HALO + TT, :] = main_ref[0].astype(F32)
    win_ref[0, HALO + TT:HALO + TT + HALO, :] = jnp.where(i < last, right, 0.0)
    for r in range(1, F32_SUBLANES):
        win_ref[r, 0:CONV_SPAN, :] = win_ref[0, r:r + CONV_SPAN, :]
    base = HALO - CONV_PAD

    def row_block(rr, carry):
        r0 = pl.multiple_of(rr * CONV_ROWS, CONV_ROWS)
        groups = CONV_ROWS // F32_SUBLANES
        reach = (base + CONV_WIDTH - 1) // F32_SUBLANES + 1
        for c in range(D_CONV // LANES):
            ls = slice(c * LANES, (c + 1) * LANES)
            accs = [None] * groups
            for shift in range(F32_SUBLANES):
                for q in range(groups + reach - 1):
                    uses = [(q - a, a * F32_SUBLANES + shift - base) for a in range(reach)
                            if 0 <= a * F32_SUBLANES + shift - base < CONV_WIDTH and 0 <= q - a < groups]
                    if not uses:
                        continue
                    tile = win_ref[shift, pl.ds(r0 + q * F32_SUBLANES, F32_SUBLANES), ls]
                    for g, k in uses:
                        term = tile * w_ref[k:k + 1, ls]
                        accs[g] = term if accs[g] is None else accs[g] + term
            for g in range(groups):
                conv_ref[pl.ds(r0 + g * F32_SUBLANES, F32_SUBLANES), ls] = accs[g]
        return carry

    lax.fori_loop(0, TT // CONV_ROWS, row_block, 0)
    cv = conv_ref[...] + b_ref[...]
    mu = jnp.mean(cv, axis=-1, keepdims=True)
    d = cv - mu
    var = jnp.mean(d * d, axis=-1, keepdims=True)
    y = d * lax.rsqrt(var + EPS) * lng_ref[...] + lnb_ref[...]
    y = y * _sigmoid(y)
    y = y * lax.rsqrt(jnp.mean(y * y, axis=-1, keepdims=True) + EPS) * og_ref[...]
    out_ref[0] = y.astype(BF16)


def _conv(gl3, dw_w, dw_b, ln_g, ln_b, out_g):
    b, t, _ = gl3.shape
    hb = TT // HALO
    n_halo = t // HALO
    const = lambda bb, i: (0, 0)
    return pl.pallas_call(
        _conv_body,
        grid=(b, t // TT),
        in_specs=[
            pl.BlockSpec((1, HALO, D_CONV), lambda bb, i: (bb, jnp.maximum(i * hb - 1, 0), 0)),
            pl.BlockSpec((1, TT, D_CONV), lambda bb, i: (bb, i, 0)),
            pl.BlockSpec((1, HALO, D_CONV), lambda bb, i: (bb, jnp.minimum((i + 1) * hb, n_halo - 1), 0)),
            pl.BlockSpec((CONV_WIDTH, D_CONV), const),
            pl.BlockSpec((1, D_CONV), const),
            pl.BlockSpec((1, D_CONV), const),
            pl.BlockSpec((1, D_CONV), const),
            pl.BlockSpec((1, D_CONV), const),
        ],
        out_specs=pl.BlockSpec((1, TT, D_CONV), lambda bb, i: (bb, i, 0)),
        out_shape=jax.ShapeDtypeStruct((b, t, D_CONV), BF16),
        scratch_shapes=[pltpu.VMEM((F32_SUBLANES, TT + 2 * HALO, D_CONV), F32),
                        pltpu.VMEM((TT, D_CONV), F32)],
        compiler_params=_cparams(("parallel", "parallel")),
        name="conv",
    )(gl3, gl3, gl3, dw_w, dw_b, ln_g, ln_b, out_g)


def _attn_body(q_ref, k_ref, vt_ref, o_ref, qs_ref, m_ref, acc_ref):
    t = k_ref.shape[1]
    nt_dims = (((1,), (1,)), ((), ()))
    for h in range(KV_GROUP):
        qs_ref[h * TQ:(h + 1) * TQ, :] = q_ref[0, :, h * LANES:(h + 1) * LANES]
    m_ref[...] = jnp.full(m_ref.shape, -jnp.inf, F32)
    acc_ref[...] = jnp.zeros(acc_ref.shape, F32)

    def step(kt, carry):
        start = pl.multiple_of(kt * TK, TK)
        s = lax.dot_general(qs_ref[...], k_ref[0, pl.ds(start, TK), :], nt_dims,
                            preferred_element_type=F32)
        m_old = m_ref[...]
        m_new = jnp.maximum(m_old, jnp.max(s, axis=-1, keepdims=True))
        p = jnp.exp2(s - m_new[:, 0:1])
        alpha = jnp.exp2(m_old - m_new)
        acc_ref[...] = alpha * acc_ref[...] + lax.dot_general(
            p.astype(BF16), vt_ref[:, pl.ds(start, TK)], nt_dims, preferred_element_type=F32)
        m_ref[...] = m_new
        return carry

    lax.fori_loop(0, t // TK, step, 0)
    acc = acc_ref[...]
    res = acc / pltpu.roll(acc, HEAD_DIM, 1)
    _attn_store(o_ref, [res[h * TQ:(h + 1) * TQ, :] for h in range(KV_GROUP)])


def _attn_shifted_body(q_ref, k_ref, vt_ref, o_ref, qs_ref, acc_ref):
    t = k_ref.shape[1]
    for h in range(KV_GROUP):
        qs_ref[h * TQ:(h + 1) * TQ, :] = q_ref[0, :, h * LANES:(h + 1) * LANES]
    acc_ref[...] = jnp.zeros(acc_ref.shape, F32)

    chunks = math.gcd(KV_CHUNKS, t // TK)

    def step(kt, carry):
        qs = qs_ref[...]
        part = None
        for c in range(chunks):
            start = pl.multiple_of(kt * (chunks * TK) + c * TK, TK)
            st = lax.dot_general(k_ref[0, pl.ds(start, TK), :], qs, (((1,), (1,)), ((), ())),
                                 preferred_element_type=F32)
            pv = jnp.dot(vt_ref[:, pl.ds(start, TK)], jnp.exp2(st).astype(BF16),
                         preferred_element_type=F32)
            part = pv if part is None else part + pv
        acc_ref[...] += part
        return carry

    lax.fori_loop(0, t // (chunks * TK), step, 0)
    acc = acc_ref[...]
    res = acc * (1.0 / acc[HEAD_DIM:HEAD_DIM + 1, :])
    _attn_store(o_ref, [res[:, h * TQ:(h + 1) * TQ].T for h in range(KV_GROUP)])


def _attn_store(o_ref, heads):
    lane = lax.broadcasted_iota(jnp.int32, (TQ, LANES), 1)
    low = lane < HEAD_DIM
    for hp in range(KV_GROUP // 2):
        o_ref[0, :, hp * LANES:(hp + 1) * LANES] = jnp.where(
            low, heads[2 * hp], pltpu.roll(heads[2 * hp + 1], HEAD_DIM, 1)).astype(BF16)


def _attention_call(body, name, scratch, q3, k3, vt):
    b, t, _ = q3.shape
    gw = KV_GROUP * LANES
    return pl.pallas_call(
        body,
        grid=(b, N_KV_HEADS, t // TQ),
        in_specs=[
            pl.BlockSpec((1, TQ, gw), lambda bb, j, i: (bb, i, j)),
            pl.BlockSpec((1, t, LANES), lambda bb, j, i: (bb, 0, j)),
            pl.BlockSpec((LANES, t), lambda bb, j, i: (j, bb)),
        ],
        out_specs=pl.BlockSpec((1, TQ, KV_GROUP * HEAD_DIM), lambda bb, j, i: (bb, i, j)),
        out_shape=jax.ShapeDtypeStruct((b, t, D_Q), BF16),
        scratch_shapes=[pltpu.VMEM((KV_GROUP * TQ, LANES), BF16)] + scratch,
        compiler_params=_cparams(("parallel", "parallel", "parallel")),
        name=name,
    )(q3, k3, vt)


def _attention_shifted(q3, k3, vt):
    return _attention_call(_attn_shifted_body, "attention_shifted",
                           [pltpu.VMEM((LANES, KV_GROUP * TQ), F32)], q3, k3, vt)


def _attention(q3, k3, vt):
    return _attention_call(_attn_body, "attention",
                           [pltpu.VMEM((KV_GROUP * TQ, LANES), F32),
                            pltpu.VMEM((KV_GROUP * TQ, LANES), F32)], q3, k3, vt)


def _outproj_body(x_ref, cn_ref, ao_ref, ag_ref, wc_ref, wa_ref, g2_ref, wr_ref,
                  h_ref, xn_ref, aff_ref):
    for blk in range(TM_OUT // TM):
        rows = slice(blk * TM, (blk + 1) * TM)
        ao = ao_ref[rows, :].astype(F32)
        an = ao * lax.rsqrt(jnp.mean(ao * ao, axis=-1, keepdims=True) + EPS) * ag_ref[...]
        h = (x_ref[rows, :]
             + jnp.dot(cn_ref[rows, :], wc_ref[...], preferred_element_type=F32)
             + jnp.dot(an.astype(BF16), wa_ref[...], preferred_element_type=F32))
        h_ref[rows, :] = h
        xn = h * lax.rsqrt(jnp.mean(h * h, axis=-1, keepdims=True) + EPS) * g2_ref[...]
        xn_hi = xn.astype(BF16)
        xn_ref[rows, :] = xn_hi
        xn_lo = (xn - xn_hi.astype(F32)).astype(BF16)
        parts = (jnp.dot(xn_hi, wr_ref[...], preferred_element_type=F32)
                 + jnp.dot(xn_lo, wr_ref[...], preferred_element_type=F32))
        parts_t = parts.T
        logits = parts_t[0:N_EXPERTS, :] + parts_t[N_EXPERTS:2 * N_EXPERTS, :]
        mx = jnp.max(logits, axis=0, keepdims=True)
        ex = jnp.exp(logits - mx)
        aff_ref[:, rows] = ex / jnp.sum(ex, axis=0, keepdims=True)


def _outproj(x2, cn, ao, ag, wc, wa, g2, wr):
    n = x2.shape[0]
    row = lambda i: (i, 0)
    const = lambda i: (0, 0)
    return pl.pallas_call(
        _outproj_body,
        grid=(n // TM_OUT,),
        in_specs=[
            pl.BlockSpec((TM_OUT, D_MODEL), row),
            pl.BlockSpec((TM_OUT, D_CONV), row),
            pl.BlockSpec((TM_OUT, D_Q), row),
            pl.BlockSpec((1, D_Q), const),
            pl.BlockSpec((D_CONV, D_MODEL), const),
            pl.BlockSpec((D_Q, D_MODEL), const),
            pl.BlockSpec((1, D_MODEL), const),
            pl.BlockSpec((D_MODEL, LANES), const),
        ],
        out_specs=[
            pl.BlockSpec((TM_OUT, D_MODEL), row),
            pl.BlockSpec((TM_OUT, D_MODEL), row),
            pl.BlockSpec((N_EXPERTS, TM_OUT), lambda i: (0, i)),
        ],
        out_shape=[
            jax.ShapeDtypeStruct((n, D_MODEL), F32),
            jax.ShapeDtypeStruct((n, D_MODEL), BF16),
            jax.ShapeDtypeStruct((N_EXPERTS, n), F32),
        ],
        compiler_params=_cparams(("parallel",)),
        name="outproj",
    )(x2, cn, ao, ag, wc, wa, g2, wr)


def _threshold_body(cap, aff_ref, thr_ref, need_ref):
    def step(it, lo):
        cand = lo | (jnp.int32(1) << (30 - it))
        bits = pltpu.bitcast(aff_ref[...], jnp.int32)
        cnt = jnp.sum((bits >= cand).astype(jnp.int32), axis=1, keepdims=True)
        return jnp.where(cnt >= cap, cand, lo)

    thr = lax.fori_loop(0, 31, step, jnp.zeros((N_EXPERTS, 1), jnp.int32))
    bits = pltpu.bitcast(aff_ref[...], jnp.int32)
    n_gt = jnp.sum((bits > thr).astype(jnp.int32), axis=1, keepdims=True)
    thr_ref[...] = jnp.broadcast_to(thr, thr_ref.shape)
    need_ref[...] = jnp.broadcast_to(cap - n_gt, need_ref.shape)


def _threshold(aff_t, cap):
    n = aff_t.shape[1]
    full = lambda: (0, 0)
    return pl.pallas_call(
        functools.partial(_threshold_body, cap),
        in_specs=[pl.BlockSpec((N_EXPERTS, n), full)],
        out_specs=[pl.BlockSpec((N_EXPERTS, LANES), full), pl.BlockSpec((N_EXPERTS, LANES), full)],
        out_shape=[jax.ShapeDtypeStruct((N_EXPERTS, LANES), jnp.int32),
                   jax.ShapeDtypeStruct((N_EXPERTS, LANES), jnp.int32)],
        compiler_params=pltpu.CompilerParams(vmem_limit_bytes=VMEM_LIMIT),
        name="threshold",
    )(aff_t)


def _mask_body(aff_ref, thr_ref, need_ref, tri_ref, gsel_ref, lpos_ref, cnt_ref, eqc_ref):
    @pl.when(pl.program_id(0) == 0)
    def _():
        eqc_ref[...] = jnp.zeros(eqc_ref.shape, F32)

    aff = aff_ref[...]
    bits = pltpu.bitcast(aff, jnp.int32)
    thr = thr_ref[:, 0:1]
    need = need_ref[:, 0:1].astype(F32)
    gt = bits > thr
    eq = bits == thr
    eq_f = jnp.where(eq, 1.0, 0.0)
    eq_rank = eqc_ref[:, 0:1] + jnp.dot(eq_f.astype(BF16), tri_ref[...], preferred_element_type=F32)
    sel = gt | (eq & (eq_rank < need))
    sel_f = jnp.where(sel, 1.0, 0.0)
    gsel_ref[...] = jnp.where(sel, aff, -1.0)
    lpos_ref[...] = jnp.dot(sel_f.astype(BF16), tri_ref[...], preferred_element_type=F32)
    cnt = jnp.sum(sel_f, axis=1, keepdims=True)
    cnt_ref[0] = jnp.broadcast_to(cnt, (N_EXPERTS, LANES)).astype(jnp.int32)
    eqc_ref[...] = eqc_ref[...] + jnp.sum(eq_f, axis=1, keepdims=True)


def _masks(aff_t, thr, need, tri):
    n = aff_t.shape[1]
    nt = n // TR
    const = lambda i: (0, 0)
    tile = lambda i: (0, i)
    return pl.pallas_call(
        _mask_body,
        grid=(nt,),
        in_specs=[
            pl.BlockSpec((N_EXPERTS, TR), tile),
            pl.BlockSpec((N_EXPERTS, LANES), const),
            pl.BlockSpec((N_EXPERTS, LANES), const),
            pl.BlockSpec((TR, TR), const),
        ],
        out_specs=[
            pl.BlockSpec((N_EXPERTS, TR), tile),
            pl.BlockSpec((N_EXPERTS, TR), tile),
            pl.BlockSpec((1, N_EXPERTS, LANES), lambda i: (i, 0, 0)),
        ],
        out_shape=[
            jax.ShapeDtypeStruct((N_EXPERTS, n), F32),
            jax.ShapeDtypeStruct((N_EXPERTS, n), F32),
            jax.ShapeDtypeStruct((nt, N_EXPERTS, LANES), jnp.int32),
        ],
        scratch_shapes=[pltpu.VMEM((N_EXPERTS, LANES), F32)],
        compiler_params=_cparams(("arbitrary",)),
        name="masks",
    )(aff_t, thr, need, tri)


def _onehot_rows(gsel_row, lpos_row, first_slot, value):
    slot = lax.broadcasted_iota(jnp.int32, (CH, TR), 0).astype(F32) + first_slot.astype(F32)
    return jnp.where((gsel_row >= 0.0) & (lpos_row == slot), value, 0.0).astype(BF16)


def _dispatch_body(cap, base_ref, shift_ref, nch_ref, keepc_ref, keepr_ref, xn_ref, gsel_ref, lpos_ref,
                   xe_ref, stage_ref, carry_ref, hot_ref, sem_ref, xsem_ref):
    n = pl.program_id(0)
    nt = pl.num_programs(0)

    def copy(e, slot, sem, chunk):
        row0 = pl.multiple_of(base_ref[e * nt + n] + chunk * CH, BF16_SUBLANES)
        return pltpu.make_async_copy(stage_ref.at[slot], xe_ref.at[e, pl.ds(row0, CH)], sem)

    @pl.when(n == 0)
    def _():
        carry_ref[...] = jnp.zeros(carry_ref.shape, BF16)
        stage_ref[N_EXPERTS] = jnp.zeros((CH, D_MODEL), BF16)
        rows_total = xe_ref.shape[1]
        n_fill = -(-(rows_total - cap) // CH)
        starts = [rows_total - (j + 1) * CH for j in range(n_fill)]
        fills = [pltpu.make_async_copy(stage_ref.at[N_EXPERTS], xe_ref.at[e, pl.ds(row0, CH)], xsem_ref.at[0])
                 for e in range(N_EXPERTS) for row0 in starts]
        for cp in fills:
            cp.start()
        for cp in fills:
            cp.wait()

    def rows_of(e, chunk):
        hot = _onehot_rows(gsel_ref[pl.ds(e, 1), :], lpos_ref[pl.ds(e, 1), :],
                           chunk * CH - shift_ref[e * nt + n], 1.0)
        return jnp.dot(hot, xn_ref[...], preferred_element_type=F32)

    def kept_group(e, slot):
        row0 = pl.multiple_of(keepr_ref[e * nt + n], BF16_SUBLANES)
        return stage_ref[slot, pl.ds(row0, BF16_SUBLANES), :]

    for e in range(N_EXPERTS):
        hot_ref[e * CH:(e + 1) * CH, :] = _onehot_rows(gsel_ref[e:e + 1, :], lpos_ref[e:e + 1, :],
                                                       -shift_ref[e * nt + n], 1.0)
    rows_all = jnp.dot(hot_ref[...], xn_ref[...], preferred_element_type=F32)

    @pl.when(n > 0)
    def _():
        for e in range(N_EXPERTS):
            copy(e, e, sem_ref.at[e], 0).wait()

    for e in range(N_EXPERTS):
        rows = rows_all[e * CH:(e + 1) * CH]
        stage_ref[e, 0:BF16_SUBLANES, :] = (rows[0:BF16_SUBLANES] + carry_ref[e].astype(F32)).astype(BF16)
        stage_ref[e, BF16_SUBLANES:CH, :] = rows[BF16_SUBLANES:CH].astype(BF16)
        copy(e, e, sem_ref.at[e], 0).start()
        carry_ref[e] = jnp.where(keepc_ref[e * nt + n] == 0, kept_group(e, e), jnp.zeros((), BF16))

    @pl.when(n == nt - 1)
    def _():
        for e in range(N_EXPERTS):
            copy(e, e, sem_ref.at[e], 0).wait()

    def per_expert(e, carry):
        def per_chunk(chunk, c):
            stage_ref[N_EXPERTS] = rows_of(e, chunk).astype(BF16)
            cp = copy(e, N_EXPERTS, xsem_ref.at[0], chunk)
            cp.start()

            @pl.when(keepc_ref[e * nt + n] == chunk)
            def _():
                carry_ref[e] = kept_group(e, N_EXPERTS)
            cp.wait()
            return c
        return lax.fori_loop(1, nch_ref[e * nt + n], per_chunk, carry)

    lax.fori_loop(0, N_EXPERTS, per_expert, 0)


def _dispatch(base, shift, nch, keepc, keepr, xn2, gsel, lpos, cap, cap_rows):
    n = xn2.shape[0]
    nt = n // TR
    grid_spec = pltpu.PrefetchScalarGridSpec(
        num_scalar_prefetch=5,
        grid=(nt,),
        in_specs=[
            pl.BlockSpec((TR, D_MODEL), lambda i, *_: (i, 0)),
            pl.BlockSpec((N_EXPERTS, TR), lambda i, *_: (0, i)),
            pl.BlockSpec((N_EXPERTS, TR), lambda i, *_: (0, i)),
        ],
        out_specs=pl.BlockSpec(memory_space=pl.ANY),
        scratch_shapes=[pltpu.VMEM((N_EXPERTS + 1, CH, D_MODEL), BF16),
                        pltpu.VMEM((N_EXPERTS, BF16_SUBLANES, D_MODEL), BF16),
                        pltpu.VMEM((N_EXPERTS * CH, TR), BF16),
                        pltpu.SemaphoreType.DMA((N_EXPERTS,)),
                        pltpu.SemaphoreType.DMA((1,))],
    )
    return pl.pallas_call(
        functools.partial(_dispatch_body, cap),
        grid_spec=grid_spec,
        out_shape=jax.ShapeDtypeStruct((N_EXPERTS, cap_rows, D_MODEL), BF16),
        compiler_params=_cparams(("arbitrary",)),
        name="dispatch",
    )(base, shift, nch, keepc, keepr, xn2, gsel, lpos)


def _ffn_body(ntile_ref, x_ref, wg_ref, wu_ref, wd_ref, y_ref):
    e = pl.program_id(0)
    i = pl.program_id(1)

    @pl.when(i < ntile_ref[e])
    def _():
        x = x_ref[0]
        g = jnp.dot(x, wg_ref[0], preferred_element_type=F32)
        u = jnp.dot(x, wu_ref[0], preferred_element_type=F32)
        hid = (g * _sigmoid(g) * u).astype(BF16)
        y_ref[0] = jnp.dot(hid, wd_ref[0].astype(BF16), preferred_element_type=F32).astype(BF16)

    @pl.when(i >= ntile_ref[e])
    def _():
        y_ref[...] = jnp.zeros(y_ref.shape, BF16)


def _ffn(ntile, xe, wg, wu, wd):
    cap_rows = xe.shape[1]
    rows = lambda e, i, nt_ref: (e, jnp.minimum(i, nt_ref[e] - 1), 0)
    wmap = lambda e, i, nt_ref: (e, 0, 0)
    grid_spec = pltpu.PrefetchScalarGridSpec(
        num_scalar_prefetch=1,
        grid=(N_EXPERTS, cap_rows // TMF),
        in_specs=[
            pl.BlockSpec((1, TMF, D_MODEL), rows),
            pl.BlockSpec((1, D_MODEL, D_FF_EXPERT), wmap),
            pl.BlockSpec((1, D_MODEL, D_FF_EXPERT), wmap),
            pl.BlockSpec((1, D_FF_EXPERT, D_MODEL), wmap),
        ],
        out_specs=pl.BlockSpec((1, TMF, D_MODEL), lambda e, i, nt_ref: (e, i, 0)),
    )
    return pl.pallas_call(
        _ffn_body,
        grid_spec=grid_spec,
        out_shape=jax.ShapeDtypeStruct(xe.shape, BF16),
        compiler_params=_cparams(("arbitrary", "arbitrary")),
        name="expert_ffn",
    )(ntile, xe, wg, wu, wd)


def _combine_body(base_ref, shift_ref, nch_ref, h_ref, gsel_ref, lpos_ref, p_ref, wpp_ref, pg_ref,
                  wpg_ref, bpg_ref, ye_ref, y_ref, ybuf_ref, xbuf_ref, hot_ref, acc_ref, sem_ref, xsem_ref):
    n = pl.program_id(0)
    nt = pl.num_programs(0)
    tn = (((0,), (0,)), ((), ()))
    cur = n % 2

    def first_chunk(e, step, buf):
        row0 = pl.multiple_of(base_ref[e * nt + step], BF16_SUBLANES)
        return pltpu.make_async_copy(ye_ref.at[e, pl.ds(row0, CH)], ybuf_ref.at[buf, pl.ds(e * CH, CH)],
                                     sem_ref.at[buf, e])

    def later_chunk(e, chunk):
        row0 = pl.multiple_of(base_ref[e * nt + n] + chunk * CH, BF16_SUBLANES)
        return pltpu.make_async_copy(ye_ref.at[e, pl.ds(row0, CH)], xbuf_ref, xsem_ref.at[0])

    def gated_hot(e, chunk):
        g = gsel_ref[pl.ds(e, 1), :]
        return _onehot_rows(g, lpos_ref[pl.ds(e, 1), :], chunk * CH - shift_ref[e * nt + n], g)

    @pl.when(n == 0)
    def _():
        for e in range(N_EXPERTS):
            first_chunk(e, 0, 0).start()

    @pl.when(n + 1 < nt)
    def _():
        for e in range(N_EXPERTS):
            first_chunk(e, n + 1, 1 - cur).start()

    for e in range(N_EXPERTS):
        hot_ref[e * CH:(e + 1) * CH, :] = gated_hot(e, jnp.int32(0))
    for e in range(N_EXPERTS):
        first_chunk(e, n, cur).wait()
    acc_ref[...] = h_ref[...] + lax.dot_general(hot_ref[...], ybuf_ref[cur], tn, preferred_element_type=F32)

    def per_expert(e, carry):
        def per_chunk(chunk, c):
            cp = later_chunk(e, chunk)
            cp.start()
            cp.wait()
            acc_ref[...] += lax.dot_general(gated_hot(e, chunk), xbuf_ref[...], tn, preferred_element_type=F32)
            return c
        return lax.fori_loop(1, nch_ref[e * nt + n], per_chunk, carry)

    lax.fori_loop(0, N_EXPERTS, per_expert, 0)

    h2 = acc_ref[...]
    emb = jnp.dot(p_ref[...].astype(BF16), wpp_ref[...], preferred_element_type=F32)
    hn = h2 * lax.rsqrt(jnp.mean(h2 * h2, axis=-1, keepdims=True) + EPS) * pg_ref[...]
    gate = _sigmoid(jnp.dot(hn.astype(BF16), wpg_ref[...], preferred_element_type=F32) + bpg_ref[...])
    y_ref[...] = h2 + gate * emb


def _combine(base, shift, nch, h1, gsel, lpos, p2, wpp, pg, wpg, bpg, ye):
    n = h1.shape[0]
    nt = n // TR
    row = lambda i, *_: (i, 0)
    tile = lambda i, *_: (0, i)
    const = lambda i, *_: (0, 0)
    grid_spec = pltpu.PrefetchScalarGridSpec(
        num_scalar_prefetch=3,
        grid=(nt,),
        in_specs=[
            pl.BlockSpec((TR, D_MODEL), row),
            pl.BlockSpec((N_EXPERTS, TR), tile),
            pl.BlockSpec((N_EXPERTS, TR), tile),
            pl.BlockSpec((TR, D_PLE), row),
            pl.BlockSpec((D_PLE, D_MODEL), const),
            pl.BlockSpec((1, D_MODEL), const),
            pl.BlockSpec((D_MODEL, D_MODEL), const),
            pl.BlockSpec((1, D_MODEL), const),
            pl.BlockSpec(memory_space=pl.ANY),
        ],
        out_specs=pl.BlockSpec((TR, D_MODEL), row),
        scratch_shapes=[pltpu.VMEM((2, N_EXPERTS * CH, D_MODEL), BF16),
                        pltpu.VMEM((CH, D_MODEL), BF16),
                        pltpu.VMEM((N_EXPERTS * CH, TR), BF16),
                        pltpu.VMEM((TR, D_MODEL), F32),
                        pltpu.SemaphoreType.DMA((2, N_EXPERTS)),
                        pltpu.SemaphoreType.DMA((1,))],
    )
    return pl.pallas_call(
        _combine_body,
        grid_spec=grid_spec,
        out_shape=jax.ShapeDtypeStruct((n, D_MODEL), F32),
        compiler_params=_cparams(("arbitrary",)),
        name="combine",
    )(base, shift, nch, h1, gsel, lpos, p2, wpp, pg, wpg, bpg, ye)


def _rope_tables(t):
    rows = t // GRID_W
    row_idx = jnp.repeat(jnp.arange(rows, dtype=F32), GRID_W)
    col_idx = jnp.tile(jnp.arange(GRID_W, dtype=F32), rows)
    freqs = 1.0 / (ROPE_THETA ** (jnp.arange(0, HALF_ROT, 2, dtype=F32) / HALF_ROT))
    ang_r = row_idx[:, None] * freqs[None, :]
    ang_c = col_idx[:, None] * freqs[None, :]
    cr, sr, cc, sc = jnp.cos(ang_r), jnp.sin(ang_r), jnp.cos(ang_c), jnp.sin(ang_c)
    cos_h = jnp.concatenate([cr, cr, cc, cc], axis=-1)
    sin_h = jnp.concatenate([-sr, sr, -sc, sc], axis=-1)
    return jnp.tile(cos_h, (1, LANES // HEAD_DIM)), jnp.tile(sin_h, (1, LANES // HEAD_DIM))


def _prepare_weights(norm1_g, w_in, conv_dw_w, conv_dw_b, conv_ln_g, conv_ln_b, q_norm_g, k_norm_g,
                     conv_out_g, attn_out_g, w_out, norm2_g, w_router, w_gate, w_up, w_down,
                     ple_proj, ple_norm_g, ple_gate_w, ple_gate_b):
    i = 0
    q_scale = (HEAD_DIM ** -0.5) * math.log2(math.e)
    qkg = jnp.concatenate([jnp.tile(q_norm_g[i] * q_scale, N_HEADS), jnp.tile(k_norm_g[i], N_KV_HEADS)])
    bound = HEAD_DIM * jnp.max(jnp.abs(q_norm_g[i] * q_scale)) * jnp.max(jnp.abs(k_norm_g[i]))
    use_shift = bound <= MAX_SCORE_SHIFT
    lane = jnp.arange(LANES)
    fill = jnp.stack([jnp.where(lane == HEAD_DIM, 1.0, 0.0),
                      jnp.where(lane == HEAD_DIM, -jnp.where(use_shift, bound, 0.0), 0.0)]).astype(F32)
    ones_bd = (lane[:, None] // HEAD_DIM == lane[None, :] // HEAD_DIM).astype(BF16)
    tok = jnp.arange(TR)
    wr_hi = w_router[i].astype(BF16)
    wr_lo = (w_router[i] - wr_hi.astype(F32)).astype(BF16)
    wr = jnp.concatenate([wr_hi, wr_lo, jnp.zeros((D_MODEL, LANES - 2 * N_EXPERTS), BF16)], axis=1)
    return dict(
        g1=norm1_g[i][None, :], w_in=w_in[i].astype(BF16), qkg=qkg[None, :], ones_bd=ones_bd,
        fill=fill, use_shift=use_shift,
        dw_w=conv_dw_w[i], dw_b=conv_dw_b[i][None, :], ln_g=conv_ln_g[i][None, :], ln_b=conv_ln_b[i][None, :],
        conv_out_g=conv_out_g[i][None, :], attn_out_g=attn_out_g[i][None, :],
        w_out_c=w_out[i, :D_CONV].astype(BF16), w_out_a=w_out[i, D_CONV:].astype(BF16),
        g2=norm2_g[i][None, :], w_router=wr,
        wg=w_gate[i].astype(BF16), wu=w_up[i].astype(BF16), wd=w_down[i],
        tri=(tok[:, None] < tok[None, :]).astype(BF16),
        wpp=ple_proj[i].astype(BF16), pg=ple_norm_g[i][None, :], wpg=ple_gate_w[i].astype(BF16),
        bpg=ple_gate_b[i][None, :],
    )


def _trunk(x, p, w):
    b, t, _ = x.shape
    n = b * t
    assert t % TM_OUT == 0 and t % TK == 0 and t % TT == 0 and t % TQ == 0 and n % TR == 0
    cap = CAPACITY_FACTOR * n // N_EXPERTS
    nt = n // TR
    x2 = x.reshape(n, D_MODEL)
    cos_t, sin_t = _rope_tables(t)

    gl, q, k, v = _inproj(x2, w["g1"], w["w_in"], w["qkg"], cos_t, sin_t, w["ones_bd"], w["fill"], t)
    cn = _conv(gl.reshape(b, t, D_CONV), w["dw_w"], w["dw_b"], w["ln_g"], w["ln_b"], w["conv_out_g"])
    ao = lax.cond(w["use_shift"], _attention_shifted, _attention,
                  q.reshape(b, t, -1), k.reshape(b, t, -1), v)
    h1, xn2, aff_t = _outproj(x2, cn.reshape(n, D_CONV), ao.reshape(n, D_Q), w["attn_out_g"],
                              w["w_out_c"], w["w_out_a"], w["g2"], w["w_router"])

    thr, need = _threshold(aff_t, cap)
    gsel, lpos, cnt = _masks(aff_t, thr, need, w["tri"])
    cnt = cnt[:, :, 0].T
    ends = jnp.cumsum(cnt, axis=1)
    start = ends - cnt
    base = start // BF16_SUBLANES * BF16_SUBLANES
    shift = start - base
    nch = jnp.maximum((shift + cnt + CH - 1) // CH, 1)
    keep = ends // BF16_SUBLANES * BF16_SUBLANES - base
    flat = lambda a: a.astype(jnp.int32).reshape(-1)
    base, shift, nch, keepc, keepr = flat(base), flat(shift), flat(nch), flat(keep // CH), flat(keep % CH)
    full_tiles = (cap + TMF - 1) // TMF
    cap_rows = (full_tiles + 1) * TMF
    ntile = jnp.full((N_EXPERTS,), full_tiles, jnp.int32)

    xe = _dispatch(base, shift, nch, keepc, keepr, xn2, gsel, lpos, cap, cap_rows)
    ye = _ffn(ntile, xe, w["wg"], w["wu"], w["wd"])
    y = _combine(base, shift, nch, h1, gsel, lpos, p.reshape(n, D_PLE), w["wpp"], w["pg"], w["wpg"],
                 w["bpg"], ye)
    return y.reshape(b, t, D_MODEL)


def kernel(x_prompt, x_sample, p_prompt, p_sample, norm1_g, w_in, conv_dw_w, conv_dw_b, conv_ln_g, conv_ln_b, q_norm_g, k_norm_g, conv_out_g, attn_out_g, w_out, norm2_g, w_router, w_gate, w_up, w_down, ple_proj, ple_norm_g, ple_gate_w, ple_gate_b):
    w = _prepare_weights(norm1_g, w_in, conv_dw_w, conv_dw_b, conv_ln_g, conv_ln_b, q_norm_g, k_norm_g,
                         conv_out_g, attn_out_g, w_out, norm2_g, w_router, w_gate, w_up, w_down,
                         ple_proj, ple_norm_g, ple_gate_w, ple_gate_b)
    y_prompt = _trunk(x_prompt, p_prompt[0], w)
    y_sample = _trunk(x_sample, p_sample[0], w)
    return (y_prompt, y_sample)
```
